```python
import math
import jax
import jax.numpy as jnp
from jax import lax
import numpy as np

D_MODEL = 2048
BATCH = 16
SEQ = 256
DEPTH = 2
DEC_BATCH = 4
DEC_SEQ = 1024
PAST_LEN = 512

GRID_W = 64
ROPE_BASE = 10000.0
NORM_EPS = 1e-6
GN_EPS = 64e-5
Q_BLOCK = 128

A_HEADS = 8
A_QK_DIM = 64
A_V_DIM = 2 * A_QK_DIM
B_HEADS = 8
B_NOPE = 128
B_ROPE = 64
B_V = 128
B_Q_LORA = 512
B_KV_LORA = 256
AB_WIDTH = A_HEADS * A_V_DIM + B_HEADS * B_V
L0_IN = 2 * A_HEADS * 2 * A_QK_DIM + A_HEADS * A_V_DIM + B_Q_LORA + B_KV_LORA + B_ROPE + AB_WIDTH
C_HEAD = 64
C_HEADS = D_MODEL // C_HEAD
C_WIDTH = C_HEADS * C_HEAD
C_DECAY_LORA = 96
C_ICLR_LORA = 96
L1_IN = 4 * C_WIDTH + 2 * C_DECAY_LORA + 2 * C_ICLR_LORA

kernel_name = 'hybrid_diffusion_diffattn_mla_rwkv7_step'


def rmsnorm(x, g, eps=NORM_EPS):
    xf = x.astype(jnp.float32)
    y = xf * lax.rsqrt(jnp.mean(xf * xf, axis=-1, keepdims=True) + eps)
    return (y * g.astype(jnp.float32)).astype(x.dtype)


def adaln(cond, w, b):
    m = (jax.nn.silu(cond) @ w + b)[:, None, :]
    return jnp.split(m, 3, axis=-1)


def rope_1d(x, pos):
    half = x.shape[-1] // 2
    freqs = ROPE_BASE ** (-jnp.arange(half, dtype=jnp.float32) / half)
    ang = pos.astype(jnp.float32)[:, None] * freqs
    cos = jnp.cos(ang)[None, :, None, :]
    sin = jnp.sin(ang)[None, :, None, :]
    xf = x.astype(jnp.float32)
    x1, x2 = xf[..., :half], xf[..., half:]
    return jnp.concatenate([x1 * cos - x2 * sin, x1 * sin + x2 * cos], axis=-1).astype(x.dtype)


def rope_2d(x):
    t_len = x.shape[1]
    n_rows = t_len // GRID_W
    row = jnp.repeat(jnp.arange(n_rows), GRID_W)
    col = jnp.tile(jnp.arange(GRID_W), n_rows)
    half = x.shape[-1] // 2
    return jnp.concatenate([rope_1d(x[..., :half], row), rope_1d(x[..., half:], col)], axis=-1)


def map_query_blocks(fn, *qs):
    bsz, t_len = qs[0].shape[:2]
    blk = min(Q_BLOCK, t_len)
    n_blk = t_len // blk
    blocks = tuple(jnp.moveaxis(q.reshape((bsz, n_blk, blk) + q.shape[2:]), 1, 0) for q in qs)
    out = lax.map(lambda qb: fn(*qb), blocks)
    out = jnp.moveaxis(out, 0, 1)
    return out.reshape((bsz, t_len) + out.shape[3:])


def mixer_ab(h, params, ctx, layer):
    w_in, w_out, diff_lambda, subln_g, q_norm_g, w_uq, kv_norm_g, w_ukv = params
    f32 = jnp.float32
    bsz, t_len, _ = h.shape
    sizes = (A_HEADS * 2 * A_QK_DIM, A_HEADS * 2 * A_QK_DIM, A_HEADS * A_V_DIM,
             B_Q_LORA, B_KV_LORA, B_ROPE, AB_WIDTH)
    a_q, a_k, a_v, cq, ckv, kpe, gate = jnp.split(h @ w_in, np.cumsum(sizes)[:-1].tolist(), axis=-1)
    a_q = a_q.reshape(bsz, t_len, A_HEADS * 2, A_QK_DIM)
    a_k = a_k.reshape(bsz, t_len, A_HEADS * 2, A_QK_DIM)
    a_v = a_v.reshape(bsz, t_len, A_HEADS, A_V_DIM)
    q_b = (rmsnorm(cq, q_norm_g) @ w_uq).reshape(bsz, t_len, B_HEADS, B_NOPE + B_ROPE)
    q_nope, q_pe = q_b[..., :B_NOPE], q_b[..., B_NOPE:]
    ckv = rmsnorm(ckv, kv_norm_g)
    kpe = kpe[:, :, None, :]
    if ctx is None:
        new = (a_k.reshape(bsz, t_len, A_HEADS, 2, A_QK_DIM), a_v, ckv, kpe[:, :, 0])
        a_q = a_q.reshape(bsz, t_len, A_HEADS, 2, A_QK_DIM)
        k_all, v_all, ckv_all, kpe_all = new
    else:
        k_ctx, v_ctx, ckv_ctx, kpe_ctx = ctx
        a_q = rope_2d(a_q).reshape(bsz, t_len, A_HEADS, 2, A_QK_DIM)
        a_k = rope_2d(a_k).reshape(bsz, t_len, A_HEADS, 2, A_QK_DIM)
        q_pe = rope_2d(q_pe)
        kpe = rope_2d(kpe)[:, :, 0]
        k_all = jnp.concatenate([a_k, k_ctx.astype(a_k.dtype)], axis=1)
        v_all = jnp.concatenate([a_v, v_ctx.astype(a_v.dtype)], axis=1)
        ckv_all = jnp.concatenate([ckv, ckv_ctx.astype(ckv.dtype)], axis=1)
        kpe_all = jnp.concatenate([kpe, kpe_ctx.astype(kpe.dtype)], axis=1)
        new = None

    lam_init = 0.8 - 0.6 * math.exp(-0.3 * layer)
    lp = diff_lambda.astype(f32)
    lam = jnp.exp(jnp.sum(lp[0] * lp[1])) - jnp.exp(jnp.sum(lp[2] * lp[3])) + lam_init
    v_all32 = v_all.astype(f32)

    def diff_block(q):
        s = jnp.einsum('bqhcd,bkhcd->bhcqk', q, k_all, preferred_element_type=f32) * (A_QK_DIM ** -0.5)
        p = jax.nn.softmax(s, axis=-1)
        return jnp.einsum('bhqk,bkhe->bqhe', p[:, :, 0] - lam * p[:, :, 1], v_all32)

    o_a = map_query_blocks(diff_block, a_q)
    o_a = rmsnorm(o_a, subln_g) * (1.0 - lam_init)

    kv_b = (ckv_all @ w_ukv).reshape(bsz, -1, B_HEADS, B_NOPE + B_V)
    k_nope, v_b = kv_b[..., :B_NOPE], kv_b[..., B_NOPE:].astype(f32)

    def mla_block(qn, qp):
        s = (jnp.einsum('bqhd,bkhd->bhqk', qn, k_nope, preferred_element_type=f32)
             + jnp.einsum('bqhd,bkd->bhqk', qp, kpe_all, preferred_element_type=f32)) * ((B_NOPE + B_ROPE) ** -0.5)
        p = jax.nn.softmax(s, axis=-1)
        return jnp.einsum('bhqk,bkhd->bqhd', p, v_b)

    o_b = map_query_blocks(mla_block, q_nope, q_pe)
    y = jnp.concatenate([o_a.reshape(bsz, t_len, -1), o_b.reshape(bsz, t_len, -1)], axis=-1).astype(h.dtype)
    return (y * jax.nn.silu(gate)) @ w_out, new


def token_shift_bidir(x, mu):
    prev = jnp.pad(x[:, :-1], ((0, 0), (1, 0), (0, 0)))
    nxt = jnp.pad(x[:, 1:], ((0, 0), (0, 1), (0, 0)))
    return x + mu[0] * (prev - x) + mu[1] * (nxt - x)


def wkv7_scan(s0, r, decay, k, v, kk, a, reverse):
    def step(s, inp):
        r_t, w_t, k_t, v_t, kk_t, a_t = inp
        sa = jnp.einsum('bhij,bhj->bhi', s, -kk_t)
        s = s * w_t[:, :, None, :] + sa[..., None] * (kk_t * a_t)[:, :, None, :] + v_t[..., None] * k_t[:, :, None, :]
        return s, jnp.einsum('bhij,bhj->bhi', s, r_t)
    xs = tuple(jnp.moveaxis(t, 1, 0) for t in (r, decay, k, v, kk, a))
    s_fin, ys = lax.scan(step, s0, xs, reverse=reverse)
    return s_fin, jnp.moveaxis(ys, 0, 1)


def mixer_rwkv(h, params, ctx, layer):
    w_in, w_out, mu, w0, w2, a0, a2, k_k, k_a, r_k, ln_w, ln_b = params
    f32 = jnp.float32
    bsz, t_len, _ = h.shape
    proj = token_shift_bidir(h @ w_in, mu)
    sizes = (C_WIDTH, C_WIDTH, C_WIDTH, C_WIDTH, 2 * C_DECAY_LORA, 2 * C_ICLR_LORA)
    r, k, v, gate, wd, ad = jnp.split(proj, np.cumsum(sizes)[:-1].tolist(), axis=-1)

    def heads(t):
        return t.astype(f32).reshape(bsz, t_len, C_HEADS, C_HEAD)

    r, k, v = heads(r), heads(k), heads(v)
    kk = k * k_k.astype(f32)
    kk = kk * lax.rsqrt(jnp.maximum(jnp.sum(kk * kk, axis=-1, keepdims=True), 1e-12))
    wd = jnp.tanh(wd.astype(f32).reshape(bsz, t_len, 2, C_DECAY_LORA))
    ad = ad.astype(f32).reshape(bsz, t_len, 2, C_ICLR_LORA)
    w_log = -jax.nn.softplus(-(w0.astype(f32) + jnp.einsum('btzr,zrd->btzd', wd, w2.astype(f32)))) - 0.5
    decay = jnp.exp(-jnp.exp(w_log)).reshape(bsz, t_len, 2, C_HEADS, C_HEAD)
    iclr = jax.nn.sigmoid(a0.astype(f32) + jnp.einsum('btzr,zrd->btzd', ad, a2.astype(f32)))
    iclr = iclr.reshape(bsz, t_len, 2, C_HEADS, C_HEAD)
    if ctx is None:
        zero = jnp.zeros((bsz, C_HEADS, C_HEAD, C_HEAD), f32)
        s_init = (zero, zero)
    else:
        s_init = (ctx[0].astype(f32), ctx[1].astype(f32))
    outs = []
    finals = []
    for z in range(2):
        a_z = iclr[:, :, z]
        k_z = k * (1.0 + (a_z - 1.0) * k_a.astype(f32))
        s_fin, y = wkv7_scan(s_init[z], r, decay[:, :, z], k_z, v, kk, a_z, z == 1)
        mean = jnp.mean(y, axis=-1, keepdims=True)
        var = jnp.mean(jnp.square(y - mean), axis=-1, keepdims=True)
        y = (y - mean) * lax.rsqrt(var + GN_EPS) * ln_w.astype(f32) + ln_b.astype(f32)
        bonus = jnp.sum(r * k_z * r_k.astype(f32), axis=-1, keepdims=True) * v
        outs.append(y + bonus)
        finals.append(s_fin)
    y = (outs[0] + outs[1]).reshape(bsz, t_len, C_WIDTH).astype(h.dtype) * jax.nn.silu(gate)
    new = (finals[0], finals[1]) if ctx is None else None
    return y @ w_out, new


def setup_inputs(seed: int = 0) -> dict:
    key = jax.random.key(seed)
    ks = iter(jax.random.split(key, 34))

    def nrm(shape, scale=1.0):
        return jax.random.normal(next(ks), shape, jnp.float32) * scale

    def unif(shape, lo, hi):
        return jax.random.uniform(next(ks), shape, jnp.float32, lo, hi)

    d = D_MODEL
    return {
        'x_prompt': nrm((BATCH, SEQ, d)),
        'x_sample': nrm((DEC_BATCH, DEC_SEQ, d)),
        'cache_l0_a_k': nrm((DEC_BATCH, PAST_LEN, A_HEADS, 2, A_QK_DIM)),
        'cache_l0_a_v': nrm((DEC_BATCH, PAST_LEN, A_HEADS, A_V_DIM)),
        'cache_l0_mla_ckv': nrm((DEC_BATCH, PAST_LEN, B_KV_LORA)),
        'cache_l0_mla_kpe': nrm((DEC_BATCH, PAST_LEN, B_ROPE)),
        'state_l1_fwd': nrm((DEC_BATCH, C_HEADS, C_HEAD, C_HEAD), 0.5),
        'state_l1_bwd': nrm((DEC_BATCH, C_HEADS, C_HEAD, C_HEAD), 0.5),
        'c': nrm((DEC_BATCH, d)),
        'c_ctx': nrm((d,)),
        'mod_w': nrm((DEPTH, d, 3 * d), d ** -0.5),
        'mod_b': nrm((DEPTH, 3 * d), 0.01),
        'norm_g': 1.0 + nrm((DEPTH, d), 0.02),
        'final_norm_g': 1.0 + nrm((d,), 0.02),
        'l0_w_in': nrm((d, L0_IN), d ** -0.5),
        'l0_w_out': nrm((AB_WIDTH, d), AB_WIDTH ** -0.5),
        'l0_diff_lambda': nrm((4, A_QK_DIM), 0.1),
        'l0_subln_g': 1.0 + nrm((A_V_DIM,), 0.02),
        'l0_q_norm_g': 1.0 + nrm((B_Q_LORA,), 0.02),
        'l0_w_uq': nrm((B_Q_LORA, B_HEADS * (B_NOPE + B_ROPE)), B_Q_LORA ** -0.5),
        'l0_kv_norm_g': 1.0 + nrm((B_KV_LORA,), 0.02),
        'l0_w_ukv': nrm((B_KV_LORA, B_HEADS * (B_NOPE + B_V)), B_KV_LORA ** -0.5),
        'l1_w_in': nrm((d, L1_IN), d ** -0.5),
        'l1_w_out': nrm((C_WIDTH, d), C_WIDTH ** -0.5),
        'l1_mu': unif((2, L1_IN), 0.0, 0.5),
        'l1_w0': unif((2, C_WIDTH), -6.0, -1.0),
        'l1_w2': nrm((2, C_DECAY_LORA, C_WIDTH), 0.1 * C_DECAY_LORA ** -0.5),
        'l1_a0': nrm((2, C_WIDTH), 0.1),
        'l1_a2': nrm((2, C_ICLR_LORA, C_WIDTH), 0.5 * C_ICLR_LORA ** -0.5),
        'l1_k_k': 0.85 + nrm((C_HEADS, C_HEAD), 0.02),
        'l1_k_a': 1.0 + nrm((C_HEADS, C_HEAD), 0.02),
        'l1_r_k': nrm((C_HEADS, C_HEAD), 0.1),
        'l1_ln_w': 1.0 + nrm((C_HEADS, C_HEAD), 0.02),
        'l1_ln_b': nrm((C_HEADS, C_HEAD), 0.01),
    }


def reference(x_prompt, x_sample, cache_l0_a_k, cache_l0_a_v, cache_l0_mla_ckv, cache_l0_mla_kpe,
              state_l1_fwd, state_l1_bwd, c, c_ctx, mod_w, mod_b, norm_g, final_norm_g,
              l0_w_in, l0_w_out, l0_diff_lambda, l0_subln_g, l0_q_norm_g, l0_w_uq, l0_kv_norm_g, l0_w_ukv,
              l1_w_in, l1_w_out, l1_mu, l1_w0, l1_w2, l1_a0, l1_a2, l1_k_k, l1_k_a, l1_r_k, l1_ln_w, l1_ln_b):
    layer_params = (
        (l0_w_in, l0_w_out, l0_diff_lambda, l0_subln_g, l0_q_norm_g, l0_w_uq, l0_kv_norm_g, l0_w_ukv),
        (l1_w_in, l1_w_out, l1_mu, l1_w0, l1_w2, l1_a0, l1_a2, l1_k_k, l1_k_a, l1_r_k, l1_ln_w, l1_ln_b),
    )
    caches = (
        (cache_l0_a_k, cache_l0_a_v, cache_l0_mla_ckv, cache_l0_mla_kpe),
        (state_l1_fwd, state_l1_bwd),
    )
    xp, xs = x_prompt, x_sample
    new_states = []
    for layer in range(DEPTH):
        mixer = mixer_ab if layer % 2 == 0 else mixer_rwkv
        shift_p, scale_p, gate_p = adaln(c_ctx[None, :], mod_w[layer], mod_b[layer])
        shift_s, scale_s, gate_s = adaln(c, mod_w[layer], mod_b[layer])
        hp = rmsnorm(xp, norm_g[layer]) * (1.0 + scale_p) + shift_p
        hs = rmsnorm(xs, norm_g[layer]) * (1.0 + scale_s) + shift_s
        op, new = mixer(hp, layer_params[layer], None, layer)
        os_, _ = mixer(hs, layer_params[layer], caches[layer], layer)
        xp = xp + gate_p * op
        xs = xs + gate_s * os_
        new_states.append(new)
    y_prompt = rmsnorm(xp, final_norm_g)
    y_sample = rmsnorm(xs, final_norm_g)
    new_l0_a_k, new_l0_a_v, new_l0_mla_ckv, new_l0_mla_kpe = new_states[0]
    new_l1_state_fwd, new_l1_state_bwd = new_states[1]
    return (y_prompt, y_sample, new_l0_a_k, new_l0_a_v, new_l0_mla_ckv, new_l0_mla_kpe, new_l1_state_fwd, new_l1_state_bwd)
```

```python
import functools
import math

import jax
import jax.numpy as jnp
from jax import lax
from jax.experimental import pallas as pl
from jax.experimental.pallas import tpu as pltpu

F32 = jnp.float32
BF16 = jnp.bfloat16

D_MODEL = 2048
GRID_W = 64
ROPE_BASE = 10000.0
NORM_EPS = 1e-6
GN_EPS = 64e-5

A_HEADS = 8
A_QK_DIM = 64
A_V_DIM = 128
B_HEADS = 8
B_NOPE = 128
B_ROPE = 64
B_V = 128
B_Q_LORA = 512
B_KV_LORA = 256
AB_WIDTH = A_HEADS * A_V_DIM + B_HEADS * B_V
C_HEAD = 64
C_HEADS = D_MODEL // C_HEAD
C_WIDTH = C_HEADS * C_HEAD
C_LORA = 96
LAM_INIT_L0 = 0.8 - 0.6 * math.exp(-0.3 * 0)

LANES = 128
SUBLANES = 8
LORA_PAD = 128
MOD_ROWS = 8
VMEM_LIMIT = 56 * 1024 * 1024

L0_COLS = 6144
L0_GATE_A, L0_GATE_B, L0_AQ, L0_AK, L0_AV = 0, 1, 2, 3, 4
L0_CQ = 10
L0_CKV = 22
L0_KPE = 46
L1_COLS = 4 * C_WIDTH + 4 * LORA_PAD
L1_LORA = 16


def _cparams(*sem):
    return pltpu.CompilerParams(dimension_semantics=sem, vmem_limit_bytes=VMEM_LIMIT)


def _silu(x):
    return x / (1.0 + jnp.exp(-x))


def _dot(a, b):
    return jnp.dot(a, b, preferred_element_type=F32)


def _dot_nt(a, b):
    return lax.dot_general(a, b, (((1,), (1,)), ((), ())), preferred_element_type=F32)


def _split_dot(x, w):
    hi = x.astype(BF16)
    lo = (x - hi.astype(F32)).astype(BF16)
    return _dot(hi, w) + _dot(lo, w)


def _rope(x, cos, sin):
    lane = lax.broadcasted_iota(jnp.int32, x.shape, 1)
    low = (lane % 32) < 16
    rot = jnp.where(low, pltpu.roll(x, LANES - 16, 1), pltpu.roll(x, 16, 1))
    return x * cos + rot * sin


def _adaln_body(c_ref, w_ref, b_ref, o_ref):
    s = _silu(c_ref[...]).astype(BF16)
    o_ref[0] = _dot(s, w_ref[0].astype(BF16)) + b_ref[0]


def _adaln(cond, mod_w, mod_b):
    depth, d, n = mod_w.shape
    tn = 1024
    return pl.pallas_call(
        _adaln_body,
        grid=(depth, n // tn),
        in_specs=[
            pl.BlockSpec((MOD_ROWS, d), lambda l, j: (0, 0)),
            pl.BlockSpec((1, d, tn), lambda l, j: (l, 0, j)),
            pl.BlockSpec((1, 1, tn), lambda l, j: (l, 0, j)),
        ],
        out_specs=pl.BlockSpec((1, MOD_ROWS, tn), lambda l, j: (l, 0, j)),
        out_shape=jax.ShapeDtypeStruct((depth, MOD_ROWS, n), F32),
        compiler_params=_cparams("parallel", "parallel"),
        name="adaln",
    )(cond, mod_w, mod_b.reshape(depth, 1, n))


def _inproj_body(x_ref, sh_ref, sc_ref, g_ref, w_ref, *rest, seq_len):
    if seq_len:
        mu_ref, o_ref, h_ref = rest
    else:
        o_ref, h_ref = rest

    @pl.when(pl.program_id(1) == 0)
    def _():
        x = x_ref[...]
        y = x * lax.rsqrt(jnp.mean(x * x, axis=-1, keepdims=True) + NORM_EPS) * g_ref[...]
        h_ref[...] = (y * (1.0 + sc_ref[0]) + sh_ref[0]).astype(BF16)

    p = _dot(h_ref[...], w_ref[...])
    if seq_len:
        tm = p.shape[0]
        row = lax.broadcasted_iota(jnp.int32, (tm, 1), 0) % seq_len
        prev = jnp.where(row == 0, 0.0, pltpu.roll(p, 1, 0))
        nxt = jnp.where(row == seq_len - 1, 0.0, pltpu.roll(p, tm - 1, 0))
        p = p + mu_ref[0:1, :] * (prev - p) + mu_ref[1:2, :] * (nxt - p)
    o_ref[...] = p


def _inproj(x, shift, scale, g, w, mu, *, rows_per_mod, seq_len, tm, tn):
    m, d = x.shape
    n = w.shape[1]
    mod_map = lambda i, j: ((i * tm) // rows_per_mod, 0, 0)
    in_specs = [
        pl.BlockSpec((tm, d), lambda i, j: (i, 0)),
        pl.BlockSpec((1, 1, d), mod_map),
        pl.BlockSpec((1, 1, d), mod_map),
        pl.BlockSpec((1, d), lambda i, j: (0, 0)),
        pl.BlockSpec((d, tn), lambda i, j: (0, j)),
    ]
    args = [x, shift, scale, g, w]
    if mu is not None:
        in_specs.append(pl.BlockSpec((2, tn), lambda i, j: (0, j)))
        args.append(mu)
    return pl.pallas_call(
        functools.partial(_inproj_body, seq_len=seq_len if mu is not None else 0),
        grid=(m // tm, n // tn),
        in_specs=in_specs,
        out_specs=pl.BlockSpec((tm, tn), lambda i, j: (i, j)),
        out_shape=jax.ShapeDtypeStruct((m, n), F32),
        scratch_shapes=[pltpu.VMEM((tm, d), BF16)],
        compiler_params=_cparams("parallel", "arbitrary"),
        name="inproj",
    )(*args)


def _rms(x, g):
    return x * lax.rsqrt(jnp.mean(x * x, axis=-1, keepdims=True) + NORM_EPS) * g


def _mla_prep_body(cq_ref, ckv_ref, qg_ref, kg_ref, wuq_ref, wukv_ref, qb_ref, ckvn_ref, kvb_ref):
    qb_ref[...] = _dot(_rms(cq_ref[...], qg_ref[...]).astype(BF16), wuq_ref[...])
    ckv = _rms(ckv_ref[...], kg_ref[...])
    ckvn_ref[...] = ckv
    kvb_ref[...] = _dot(ckv.astype(BF16), wukv_ref[...])


def _mla_prep(proj, q_norm_g, kv_norm_g, w_uq, w_ukv, *, tm):
    m = proj.shape[0]
    nq, nkv = w_uq.shape[1], w_ukv.shape[1]
    const = lambda i: (0, 0)
    return pl.pallas_call(
        _mla_prep_body,
        grid=(m // tm,),
        in_specs=[
            pl.BlockSpec((tm, B_Q_LORA), lambda i: (i, L0_CQ)),
            pl.BlockSpec((tm, B_KV_LORA), lambda i: (i, L0_CKV)),
            pl.BlockSpec((1, B_Q_LORA), const),
            pl.BlockSpec((1, B_KV_LORA), const),
            pl.BlockSpec((B_Q_LORA, nq), const),
            pl.BlockSpec((B_KV_LORA, nkv), const),
        ],
        out_specs=[
            pl.BlockSpec((tm, nq), lambda i: (i, 0)),
            pl.BlockSpec((tm, B_KV_LORA), lambda i: (i, 0)),
            pl.BlockSpec((tm, nkv), lambda i: (i, 0)),
        ],
        out_shape=[
            jax.ShapeDtypeStruct((m, nq), F32),
            jax.ShapeDtypeStruct((m, B_KV_LORA), F32),
            jax.ShapeDtypeStruct((m, nkv), F32),
        ],
        compiler_params=_cparams("parallel"),
        name="mla_prep",
    )(proj, proj, q_norm_g, kv_norm_g, w_uq, w_ukv)


def _matmul_body(x_ref, w_ref, o_ref):
    o_ref[...] = _dot(x_ref[...].astype(BF16), w_ref[...])


def _matmul(x, w, *, tm):
    m, k = x.shape
    n = w.shape[1]
    return pl.pallas_call(
        _matmul_body,
        grid=(m // tm,),
        in_specs=[pl.BlockSpec((tm, k), lambda i: (i, 0)), pl.BlockSpec((k, n), lambda i: (0, 0))],
        out_specs=pl.BlockSpec((tm, n), lambda i: (i, 0)),
        out_shape=jax.ShapeDtypeStruct((m, n), F32),
        compiler_params=_cparams("parallel"),
        name="matmul",
    )(x, w)


def _softmax_parts(scores):
    m = functools.reduce(jnp.maximum, [jnp.max(s, axis=-1, keepdims=True) for s in scores])
    ps = [jnp.exp(s - m) for s in scores]
    denom = functools.reduce(jnp.add, [jnp.sum(p, axis=-1, keepdims=True) for p in ps])
    return ps, denom


def _diff_attn_body(*refs, dec, tq):
    if dec:
        (q_ref, k_ref, v_ref, kc_ref, vc_ref, gate_ref, cos_ref, sin_ref, lam_ref, sg_ref,
         o_ref, kb, vb, kcb, vcb) = refs
    else:
        q_ref, k_ref, v_ref, gate_ref, lam_ref, sg_ref, o_ref, kb, vb = refs
    qi = pl.program_id(1)

    @pl.when(qi == 0)
    def _():
        for h in range(A_HEADS):
            hs = slice(h * LANES, (h + 1) * LANES)
            kt = k_ref[0, :, hs]
            if dec:
                kt = _rope(kt, cos_ref[...], sin_ref[...])
                kcb[:, hs] = kc_ref[0, :, hs].astype(BF16)
                vcb[:, hs] = vc_ref[0, :, hs].astype(BF16)
            kb[:, hs] = kt.astype(BF16)
            vb[:, hs] = v_ref[0, :, hs].astype(BF16)

    lp = lam_ref[...]
    lam = (jnp.exp(jnp.sum(lp[0:1] * lp[1:2], keepdims=True))
           - jnp.exp(jnp.sum(lp[2:3] * lp[3:4], keepdims=True)) + LAM_INIT_L0)
    if dec:
        row0 = pl.multiple_of(qi * tq, tq)
        cq, sq = cos_ref[pl.ds(row0, tq), :], sin_ref[pl.ds(row0, tq), :]
    first = lax.broadcasted_iota(jnp.int32, (1, LANES), 1) < A_QK_DIM
    for h in range(A_HEADS):
        hs = slice(h * LANES, (h + 1) * LANES)
        qh = q_ref[0, :, hs]
        if dec:
            qh = _rope(qh, cq, sq)
        qh = qh * (A_QK_DIM ** -0.5)
        q1 = jnp.where(first, qh, 0.0).astype(BF16)
        q2 = jnp.where(first, 0.0, qh).astype(BF16)
        keys = [kb[:, hs]] + ([kcb[:, hs]] if dec else [])
        vals = [vb[:, hs]] + ([vcb[:, hs]] if dec else [])
        p1, l1 = _softmax_parts([_dot_nt(q1, kp) for kp in keys])
        p2, l2 = _softmax_parts([_dot_nt(q2, kp) for kp in keys])
        a1 = 1.0 / l1
        a2 = lam / l2
        o = functools.reduce(jnp.add, [_dot((x1 * a1 - x2 * a2).astype(BF16), vp)
                                       for x1, x2, vp in zip(p1, p2, vals)])
        o = _rms(o, sg_ref[...]) * (1.0 - LAM_INIT_L0)
        o_ref[0, :, hs] = (o * _silu(gate_ref[0, :, hs])).astype(BF16)


def _diff_attn(proj, ctx_k, ctx_v, cos, sin, diff_lambda, subln_g, *, tq):
    bsz, t_len, _ = proj.shape
    dec = ctx_k is not None
    w = A_HEADS * LANES
    full = lambda col: pl.BlockSpec((1, t_len, w), lambda b, i: (b, 0, col))
    in_specs = [pl.BlockSpec((1, tq, w), lambda b, i: (b, i, L0_AQ)), full(L0_AK), full(L0_AV)]
    args = [proj, proj, proj]
    scratch = [pltpu.VMEM((t_len, w), BF16), pltpu.VMEM((t_len, w), BF16)]
    if dec:
        p_len = ctx_k.shape[1]
        in_specs += [pl.BlockSpec((1, p_len, w), lambda b, i: (b, 0, 0))] * 2
        args += [ctx_k, ctx_v]
        scratch += [pltpu.VMEM((p_len, w), BF16), pltpu.VMEM((p_len, w), BF16)]
    in_specs.append(pl.BlockSpec((1, tq, w), lambda b, i: (b, i, L0_GATE_A)))
    args.append(proj)
    if dec:
        in_specs += [pl.BlockSpec((t_len, LANES), lambda b, i: (0, 0))] * 2
        args += [cos, sin]
    in_specs += [pl.BlockSpec((4, A_QK_DIM), lambda b, i: (0, 0)),
                 pl.BlockSpec((1, A_V_DIM), lambda b, i: (0, 0))]
    args += [diff_lambda, subln_g]
    return pl.pallas_call(
        functools.partial(_diff_attn_body, dec=dec, tq=tq),
        grid=(bsz, t_len // tq),
        in_specs=in_specs,
        out_specs=pl.BlockSpec((1, tq, w), lambda b, i: (b, i, 0)),
        out_shape=jax.ShapeDtypeStruct((bsz, t_len, w), BF16),
        scratch_shapes=scratch,
        compiler_params=_cparams("parallel", "arbitrary"),
        name="diff_attn",
    )(*args)


def _mla_attn_body(*refs, dec, tq):
    if dec:
        (qn_ref, qp_ref, kv_ref, kpe_ref, kvc_ref, kpec_ref, gate_ref, cos_ref, sin_ref,
         o_ref, kvb, kpeb, kvcb, kpecb) = refs
    else:
        qn_ref, qp_ref, kv_ref, kpe_ref, gate_ref, o_ref, kvb, kpeb = refs
    qi = pl.program_id(1)

    @pl.when(qi == 0)
    def _():
        kvb[...] = kv_ref[0].astype(BF16)
        kp = kpe_ref[0]
        if dec:
            kp = _rope(kp, cos_ref[...], sin_ref[...])
            kvcb[...] = kvc_ref[0].astype(BF16)
            kpecb[...] = kpec_ref[0].astype(BF16)
        kpeb[...] = kp.astype(BF16)

    if dec:
        row0 = pl.multiple_of(qi * tq, tq)
        cq, sq = cos_ref[pl.ds(row0, tq), :], sin_ref[pl.ds(row0, tq), :]
    scale = (B_NOPE + B_ROPE) ** -0.5
    first = lax.broadcasted_iota(jnp.int32, (1, LANES), 1) < B_ROPE
    for h in range(B_HEADS):
        hs = slice(h * LANES, (h + 1) * LANES)
        if h % 2 == 0:
            pair = qp_ref[0, :, (h // 2) * LANES:(h // 2 + 1) * LANES]
            if dec:
                pair = _rope(pair, cq, sq)
            pair = pair * scale
        qp = (jnp.where(first, pair, 0.0) if h % 2 == 0 else jnp.where(first, 0.0, pair)).astype(BF16)
        qn = (qn_ref[0, :, hs] * scale).astype(BF16)
        ks = slice(2 * h * LANES, (2 * h + 1) * LANES)
        vs = slice((2 * h + 1) * LANES, (2 * h + 2) * LANES)
        scores = [_dot_nt(qn, kvb[:, ks]) + _dot_nt(qp, kpeb[...])]
        vals = [kvb[:, vs]]
        if dec:
            scores.append(_dot_nt(qn, kvcb[:, ks]) + _dot_nt(qp, kpecb[...]))
            vals.append(kvcb[:, vs])
        ps, denom = _softmax_parts(scores)
        inv = 1.0 / denom
        o = functools.reduce(jnp.add, [_dot((p * inv).astype(BF16), vp) for p, vp in zip(ps, vals)])
        o_ref[0, :, hs] = (o * _silu(gate_ref[0, :, hs])).astype(BF16)


def _mla_attn(proj, q_b, kv_b, ctx_kv, ctx_kpe, cos, sin, *, tq):
    bsz, t_len, _ = proj.shape
    dec = ctx_kv is not None
    w = B_HEADS * LANES
    nkv = kv_b.shape[2]
    in_specs = [
        pl.BlockSpec((1, tq, w), lambda b, i: (b, i, 0)),
        pl.BlockSpec((1, tq, B_HEADS * B_ROPE), lambda b, i: (b, i, 2)),
        pl.BlockSpec((1, t_len, nkv), lambda b, i: (b, 0, 0)),
        pl.BlockSpec((1, t_len, LANES), lambda b, i: (b, 0, L0_KPE)),
    ]
    args = [q_b, q_b, kv_b, proj]
    scratch = [pltpu.VMEM((t_len, nkv), BF16), pltpu.VMEM((t_len, LANES), BF16)]
    if dec:
        p_len = ctx_kv.shape[1]
        in_specs += [pl.BlockSpec((1, p_len, nkv), lambda b, i: (b, 0, 0)),
                     pl.BlockSpec((1, p_len, LANES), lambda b, i: (b, 0, 0))]
        args += [ctx_kv, ctx_kpe]
        scratch += [pltpu.VMEM((p_len, nkv), BF16), pltpu.VMEM((p_len, LANES), BF16)]
    in_specs.append(pl.BlockSpec((1, tq, w), lambda b, i: (b, i, L0_GATE_B)))
    args.append(proj)
    if dec:
        in_specs += [pl.BlockSpec((t_len, LANES), lambda b, i: (0, 0))] * 2
        args += [cos, sin]
    return pl.pallas_call(
        functools.partial(_mla_attn_body, dec=dec, tq=tq),
        grid=(bsz, t_len // tq),
        in_specs=in_specs,
        out_specs=pl.BlockSpec((1, tq, w), lambda b, i: (b, i, 0)),
        out_shape=jax.ShapeDtypeStruct((bsz, t_len, w), BF16),
        scratch_shapes=scratch,
        compiler_params=_cparams("parallel", "arbitrary"),
        name="mla_attn",
    )(*args)


def _outproj_body(x_ref, ya_ref, yb_ref, w_ref, gate_ref, o_ref):
    half = ya_ref.shape[1]
    acc = _dot(ya_ref[...], w_ref[0:half, :]) + _dot(yb_ref[...], w_ref[half:, :])
    o_ref[...] = x_ref[...] + gate_ref[0] * acc


def _outproj(x, ya, yb, w, gate, *, rows_per_mod, tm, tn):
    m, d = x.shape
    k = w.shape[0]
    return pl.pallas_call(
        _outproj_body,
        grid=(m // tm, d // tn),
        in_specs=[
            pl.BlockSpec((tm, tn), lambda i, j: (i, j)),
            pl.BlockSpec((tm, k // 2), lambda i, j: (i, 0)),
            pl.BlockSpec((tm, k // 2), lambda i, j: (i, 0)),
            pl.BlockSpec((k, tn), lambda i, j: (0, j)),
            pl.BlockSpec((1, 1, tn), lambda i, j: ((i * tm) // rows_per_mod, 0, j)),
        ],
        out_specs=pl.BlockSpec((tm, tn), lambda i, j: (i, j)),
        out_shape=jax.ShapeDtypeStruct((m, d), F32),
        compiler_params=_cparams("parallel", "parallel"),
        name="outproj",
    )(x, ya, yb, w, gate)


def _head_sums(x, ones_bd):
    return _split_dot(x, ones_bd)


def _rwkv_prep_body(r_ref, k_ref, v_ref, lora_ref, w0_ref, w2_ref, a0_ref, a2_ref, kk_ref_p, ka_ref, rk_ref,
                    bd_ref, kk_out, w_out, kz_out, kka_out, bonus_out):
    ones_bd = bd_ref[...]
    lora = lora_ref[...]
    wl, al = [], []
    for z in range(2):
        wd = jnp.tanh(lora[:, z * LORA_PAD:(z + 1) * LORA_PAD]).astype(BF16)
        ad = lora[:, (2 + z) * LORA_PAD:(3 + z) * LORA_PAD].astype(BF16)
        wl.append(_dot(wd, w2_ref[z]))
        al.append(_dot(ad, a2_ref[z]))
    for c in range(C_WIDTH // LANES):
        cs = slice(c * LANES, (c + 1) * LANES)
        r, k, v = r_ref[:, cs], k_ref[:, cs], v_ref[:, cs]
        kk = k * kk_ref_p[:, cs]
        kk = kk * lax.rsqrt(jnp.maximum(_head_sums(kk * kk, ones_bd), 1e-12))
        kk_out[:, cs] = kk
        bonus = jnp.zeros_like(v)
        for z in range(2):
            u = -(w0_ref[z:z + 1, cs] + wl[z][:, cs])
            softplus = jnp.maximum(u, 0.0) + jnp.log(1.0 + jnp.exp(-jnp.abs(u)))
            w_out[z, :, cs] = jnp.exp(-jnp.exp(-softplus - 0.5))
            a = 1.0 / (1.0 + jnp.exp(-(a0_ref[z:z + 1, cs] + al[z][:, cs])))
            kz = k * (1.0 + (a - 1.0) * ka_ref[:, cs])
            kz_out[z, :, cs] = kz
            kka_out[z, :, cs] = kk * a
            bonus = bonus + _head_sums(r * kz * rk_ref[:, cs], ones_bd) * v
        bonus_out[:, cs] = bonus


def _rwkv_prep(proj, w0, w2, a0, a2, k_k, k_a, r_k, ones_bd, *, tm):
    m = proj.shape[0]
    const2 = lambda i: (0, 0)
    const3 = lambda i: (0, 0, 0)
    col = lambda c: pl.BlockSpec((tm, C_WIDTH), lambda i: (i, c))
    per_dir = jax.ShapeDtypeStruct((2, m, C_WIDTH), F32)
    single = jax.ShapeDtypeStruct((m, C_WIDTH), F32)
    return pl.pallas_call(
        _rwkv_prep_body,
        grid=(m // tm,),
        in_specs=[
            col(0), col(1), col(2),
            pl.BlockSpec((tm, 4 * LORA_PAD), lambda i: (i, L1_LORA)),
            pl.BlockSpec((2, C_WIDTH), const2),
            pl.BlockSpec((2, LORA_PAD, C_WIDTH), const3),
            pl.BlockSpec((2, C_WIDTH), const2),
            pl.BlockSpec((2, LORA_PAD, C_WIDTH), const3),
            pl.BlockSpec((1, C_WIDTH), const2),
            pl.BlockSpec((1, C_WIDTH), const2),
            pl.BlockSpec((1, C_WIDTH), const2),
            pl.BlockSpec((LANES, LANES), const2),
        ],
        out_specs=[
            pl.BlockSpec((tm, C_WIDTH), lambda i: (i, 0)),
            pl.BlockSpec((2, tm, C_WIDTH), lambda i: (0, i, 0)),
            pl.BlockSpec((2, tm, C_WIDTH), lambda i: (0, i, 0)),
            pl.BlockSpec((2, tm, C_WIDTH), lambda i: (0, i, 0)),
            pl.BlockSpec((tm, C_WIDTH), lambda i: (i, 0)),
        ],
        out_shape=[single, per_dir, per_dir, per_dir, single],
        compiler_params=_cparams("parallel"),
        name="rwkv_prep",
    )(proj, proj, proj, proj, w0, w2, a0, a2, k_k, k_a, r_k, ones_bd)


def _scan_body(r_ref, w_ref, k_ref, v_ref, kk_ref, kka_ref, s0_ref, y_ref, sfin_ref, *, t_len, reverse, pairs):
    lane = lax.broadcasted_iota(jnp.int32, (C_HEAD, LANES), 1)
    sub = lax.broadcasted_iota(jnp.int32, (C_HEAD, LANES), 0)
    left = lane < C_HEAD
    diag = jnp.where(left, lane, lane - C_HEAD) == sub

    def group_sum(x):
        a = jnp.sum(jnp.where(left, x, 0.0), axis=1, keepdims=True)
        b = jnp.sum(jnp.where(left, 0.0, x), axis=1, keepdims=True)
        return jnp.where(left, a, b)

    init = tuple(jnp.concatenate([s0_ref[0, 2 * p], s0_ref[0, 2 * p + 1]], axis=1) for p in range(pairs))

    n_blk = t_len // SUBLANES
    order = range(SUBLANES - 1, -1, -1) if reverse else range(SUBLANES)

    def block(ib, states):
        base = pl.multiple_of(((n_blk - 1 - ib) if reverse else ib) * SUBLANES, SUBLANES)
        out = []
        for p in range(pairs):
            ls = slice(p * LANES, (p + 1) * LANES)
            r, w, k, v, kk, kka = (ref[0, pl.ds(base, SUBLANES), ls]
                                   for ref in (r_ref, w_ref, k_ref, v_ref, kk_ref, kka_ref))
            s = states[p]
            ys = [None] * SUBLANES
            for j in order:
                row = lambda a: a[j:j + 1, :]
                sa = group_sum(s * (-row(kk)))
                vcol = group_sum(jnp.where(diag, row(v), 0.0))
                s = s * row(w) + sa * row(kka) + vcol * row(k)
                yb = group_sum(s * row(r))
                ys[j] = jnp.sum(jnp.where(diag, yb, 0.0), axis=0, keepdims=True)
            y_ref[0, pl.ds(base, SUBLANES), ls] = jnp.concatenate(ys, axis=0)
            out.append(s)
        return tuple(out)

    fin = lax.fori_loop(0, n_blk, block, init)
    for p in range(pairs):
        sfin_ref[0, 2 * p] = fin[p][:, :C_HEAD]
        sfin_ref[0, 2 * p + 1] = fin[p][:, C_HEAD:]


def _scan(proj, w, kz, kk, kka, s0, *, reverse, pairs):
    bsz, t_len, _ = proj.shape
    wl = pairs * LANES
    v_col0 = 2 * C_WIDTH // wl
    blk = lambda off: pl.BlockSpec((1, t_len, wl), lambda b, g: (b, 0, off + g))
    st = pl.BlockSpec((1, 2 * pairs, C_HEAD, C_HEAD), lambda b, g: (b, g, 0, 0))
    return pl.pallas_call(
        functools.partial(_scan_body, t_len=t_len, reverse=reverse, pairs=pairs),
        grid=(bsz, C_WIDTH // wl),
        in_specs=[blk(0), blk(0), blk(0), blk(v_col0), blk(0), blk(0), st],
        out_specs=[blk(0), st],
        out_shape=[jax.ShapeDtypeStruct((bsz, t_len, C_WIDTH), F32),
                   jax.ShapeDtypeStruct((bsz, C_HEADS, C_HEAD, C_HEAD), F32)],
        compiler_params=_cparams("parallel", "parallel"),
        name="wkv_scan",
    )(proj, w, kz, proj, kk, kka, s0)


def _rwkv_out_body(x_ref, y0_ref, y1_ref, bonus_ref, gate_ref, lnw_ref, lnb_ref, bd_ref, w_ref, mg_ref, fg_ref,
                   o_ref, yg_ref):
    ones_bd = bd_ref[...]
    inv = 1.0 / C_HEAD
    for c in range(C_WIDTH // LANES):
        cs = slice(c * LANES, (c + 1) * LANES)
        tot = bonus_ref[:, cs]
        for y_ref in (y0_ref, y1_ref):
            y = y_ref[:, cs]
            dev = y - _head_sums(y, ones_bd) * inv
            var = _head_sums(dev * dev, ones_bd) * inv
            tot = tot + dev * lax.rsqrt(var + GN_EPS) * lnw_ref[:, cs] + lnb_ref[:, cs]
        yg_ref[:, cs] = (tot * _silu(gate_ref[:, cs])).astype(BF16)
    xn = x_ref[...] + mg_ref[0] * _dot(yg_ref[...], w_ref[...])
    o_ref[...] = _rms(xn, fg_ref[...])


def _rwkv_out(x, y0, y1, bonus, proj, ln_w, ln_b, ones_bd, w, gate, final_g, *, rows_per_mod, tm):
    m, d = x.shape
    row = lambda i: (i, 0)
    const = lambda i: (0, 0)
    tile = pl.BlockSpec((tm, d), row)
    vec = pl.BlockSpec((1, d), const)
    return pl.pallas_call(
        _rwkv_out_body,
        grid=(m // tm,),
        in_specs=[
            tile, tile, tile, tile,
            pl.BlockSpec((tm, C_WIDTH), lambda i: (i, 3)),
            vec, vec,
            pl.BlockSpec((LANES, LANES), const),
            pl.BlockSpec((C_WIDTH, d), const, pipeline_mode=pl.Buffered(1)),
            pl.BlockSpec((1, 1, d), lambda i: ((i * tm) // rows_per_mod, 0, 0)),
            vec,
        ],
        out_specs=tile,
        out_shape=jax.ShapeDtypeStruct((m, d), F32),
        scratch_shapes=[pltpu.VMEM((tm, C_WIDTH), BF16)],
        compiler_params=_cparams("parallel"),
        name="rwkv_out",
    )(x, y0, y1, bonus, proj, ln_w, ln_b, ones_bd, w, gate, final_g)


def _rope_tables(t_len):
    pos = jnp.arange(t_len)
    half = 16
    freqs = ROPE_BASE ** (-jnp.arange(half, dtype=F32) / half)
    ang_r = (pos // GRID_W).astype(F32)[:, None] * freqs
    ang_c = (pos % GRID_W).astype(F32)[:, None] * freqs
    cos = jnp.concatenate([jnp.cos(ang_r)] * 2 + [jnp.cos(ang_c)] * 2, axis=-1)
    sin = jnp.concatenate([-jnp.sin(ang_r), jnp.sin(ang_r), -jnp.sin(ang_c), jnp.sin(ang_c)], axis=-1)
    return jnp.tile(cos, (1, 2)), jnp.tile(sin, (1, 2))


def _l0_w_in_layout(w):
    d = w.shape[0]
    aq, ak, av, cq, ckv, kpe, gate = jnp.split(w, [1024, 2048, 3072, 3584, 3840, 3904], axis=1)
    pad = jnp.zeros((d, L0_COLS - 6016), w.dtype)
    return jnp.concatenate([gate, aq, ak, av, cq, ckv, kpe, kpe, pad], axis=1).astype(BF16)


def _l1_cols_layout(w):
    main, wd, ad = jnp.split(w, [4 * C_WIDTH, 4 * C_WIDTH + 2 * C_LORA], axis=1)
    pad = jnp.zeros((w.shape[0], LORA_PAD - C_LORA), w.dtype)
    parts = [main]
    for seg in (wd, ad):
        for z in range(2):
            parts += [seg[:, z * C_LORA:(z + 1) * C_LORA], pad]
    return jnp.concatenate(parts, axis=1)


def _layer0(x, mods, rows_per_mod, seq_len, weights, ctx, tables):
    w_in, w_out, diff_lambda, subln_g, q_norm_g, w_uq, kv_norm_g, w_ukv = weights
    shift, scale, gate, norm_g = mods
    m = x.shape[0]
    bsz = m // seq_len
    proj = _inproj(x, shift, scale, norm_g, w_in, None, rows_per_mod=rows_per_mod, seq_len=seq_len, tm=512, tn=1024)
    q_b, ckv_n, kv_b = _mla_prep(proj, q_norm_g, kv_norm_g, w_uq, w_ukv, tm=512)
    proj3 = proj.reshape(bsz, seq_len, L0_COLS)
    q_b3 = q_b.reshape(bsz, seq_len, -1)
    kv_b3 = kv_b.reshape(bsz, seq_len, -1)
    if ctx is None:
        ya = _diff_attn(proj3, None, None, None, None, diff_lambda, subln_g, tq=256)
        yb = _mla_attn(proj3, q_b3, kv_b3, None, None, None, None, tq=256)
    else:
        k_ctx, v_ctx, ckv_ctx, kpe_ctx = ctx
        p_len = k_ctx.shape[1]
        cos, sin = tables
        kv_ctx = _matmul(ckv_ctx.reshape(bsz * p_len, B_KV_LORA), w_ukv, tm=512).reshape(bsz, p_len, -1)
        ya = _diff_attn(proj3, k_ctx.reshape(bsz, p_len, -1), v_ctx.reshape(bsz, p_len, -1), cos, sin,
                        diff_lambda, subln_g, tq=256)
        yb = _mla_attn(proj3, q_b3, kv_b3, kv_ctx, jnp.concatenate([kpe_ctx, kpe_ctx], axis=-1), cos, sin, tq=256)
    x_new = _outproj(x, ya.reshape(m, -1), yb.reshape(m, -1), w_out, gate, rows_per_mod=rows_per_mod, tm=512, tn=1024)
    return x_new, proj, ckv_n


def _layer1(x, mods, rows_per_mod, seq_len, weights, states, final_g):
    w_in, w_out, mu, w0, w2, a0, a2, k_k, k_a, r_k, ln_w, ln_b, ones_bd = weights
    shift, scale, gate, norm_g = mods
    m = x.shape[0]
    bsz = m // seq_len
    proj = _inproj(x, shift, scale, norm_g, w_in, mu, rows_per_mod=rows_per_mod, seq_len=seq_len, tm=1024, tn=512)
    kk, w, kz, kka, bonus = _rwkv_prep(proj, w0, w2, a0, a2, k_k, k_a, r_k, ones_bd, tm=256)
    proj3 = proj.reshape(bsz, seq_len, L1_COLS)
    as3 = lambda a: a.reshape(bsz, seq_len, C_WIDTH)
    ys, finals = [], []
    for z in range(2):
        y, s_fin = _scan(proj3, as3(w[z]), as3(kz[z]), as3(kk), as3(kka[z]), states[z], reverse=(z == 1), pairs=2)
        ys.append(y.reshape(m, C_WIDTH))
        finals.append(s_fin)
    out = _rwkv_out(x, ys[0], ys[1], bonus, proj, ln_w, ln_b, ones_bd, w_out, gate, final_g,
                    rows_per_mod=rows_per_mod, tm=256)
    return out, finals


def kernel(x_prompt, x_sample, cache_l0_a_k, cache_l0_a_v, cache_l0_mla_ckv, cache_l0_mla_kpe, state_l1_fwd, state_l1_bwd, c, c_ctx, mod_w, mod_b, norm_g, final_norm_g, l0_w_in, l0_w_out, l0_diff_lambda, l0_subln_g, l0_q_norm_g, l0_w_uq, l0_kv_norm_g, l0_w_ukv, l1_w_in, l1_w_out, l1_mu, l1_w0, l1_w2, l1_a0, l1_a2, l1_k_k, l1_k_a, l1_r_k, l1_ln_w, l1_ln_b):
    d = D_MODEL
    bp, tp, _ = x_prompt.shape
    bs, ts, _ = x_sample.shape

    cond = jnp.concatenate([c_ctx[None, :], c, jnp.zeros((MOD_ROWS - 1 - bs, d), F32)], axis=0)
    mods = _adaln(cond, mod_w, mod_b)

    def mod_rows(layer, lo, hi):
        rows = mods[layer, lo:hi]
        shift, scale, gate = (rows[:, i * d:(i + 1) * d].reshape(hi - lo, 1, d) for i in range(3))
        return shift, scale, gate, norm_g[layer].reshape(1, d)

    w_uq = l0_w_uq.reshape(B_Q_LORA, B_HEADS, B_NOPE + B_ROPE)
    w_uq = jnp.concatenate([w_uq[:, :, :B_NOPE].reshape(B_Q_LORA, -1), w_uq[:, :, B_NOPE:].reshape(B_Q_LORA, -1)],
                           axis=1).astype(BF16)
    l0_weights = (_l0_w_in_layout(l0_w_in), l0_w_out.astype(BF16), l0_diff_lambda, l0_subln_g.reshape(1, -1),
                  l0_q_norm_g.reshape(1, -1), w_uq, l0_kv_norm_g.reshape(1, -1), l0_w_ukv.astype(BF16))
    lora_pad = jnp.zeros((2, LORA_PAD - C_LORA, C_WIDTH), F32)
    lane = jnp.arange(LANES)
    ones_bd = ((lane[:, None] // C_HEAD) == (lane[None, :] // C_HEAD)).astype(BF16)
    l1_weights = (_l1_cols_layout(l1_w_in).astype(BF16), l1_w_out.astype(BF16), _l1_cols_layout(l1_mu),
                  l1_w0, jnp.concatenate([l1_w2, lora_pad], axis=1).astype(BF16),
                  l1_a0, jnp.concatenate([l1_a2, lora_pad], axis=1).astype(BF16),
                  l1_k_k.reshape(1, -1), l1_k_a.reshape(1, -1), l1_r_k.reshape(1, -1),
                  l1_ln_w.reshape(1, -1), l1_ln_b.reshape(1, -1), ones_bd)
    tables = _rope_tables(ts)
    final_g = final_norm_g.reshape(1, d)

    xp = x_prompt.reshape(bp * tp, d)
    xs = x_sample.reshape(bs * ts, d)
    ctx0 = (cache_l0_a_k, cache_l0_a_v, cache_l0_mla_ckv, cache_l0_mla_kpe)

    xp1, proj_p, ckv_p = _layer0(xp, mod_rows(0, 0, 1), bp * tp, tp, l0_weights, None, None)
    xs1, _, _ = _layer0(xs, mod_rows(0, 1, 1 + bs), ts, ts, l0_weights, ctx0, tables)

    zero_state = jnp.zeros((bp, C_HEADS, C_HEAD, C_HEAD), F32)
    y_prompt, finals = _layer1(xp1, mod_rows(1, 0, 1), bp * tp, tp, l1_weights, (zero_state, zero_state), final_g)
    y_sample, _ = _layer1(xs1, mod_rows(1, 1, 1 + bs), ts, ts, l1_weights, (state_l1_fwd, state_l1_bwd), final_g)

    new_a_k = proj_p[:, 3072:4096].reshape(bp, tp, A_HEADS, 2, A_QK_DIM)
    new_a_v = proj_p[:, 4096:5120].reshape(bp, tp, A_HEADS, A_V_DIM)
    new_ckv = ckv_p.reshape(bp, tp, B_KV_LORA)
    new_kpe = proj_p[:, 5888:5888 + B_ROPE].reshape(bp, tp, B_ROPE)
    return (y_prompt.reshape(bp, tp, d), y_sample.reshape(bs, ts, d), new_a_k, new_a_v, new_ckv, new_kpe,
            finals[0], finals[1])
```

```python
import functools
import math

import jax
import jax.numpy as jnp
from jax import lax
from jax.experimental import pallas as pl
from jax.experimental.pallas import tpu as pltpu

F32 = jnp.float32
BF16 = jnp.bfloat16

D_MODEL = 2048
GRID_W = 64
ROPE_BASE = 10000.0
NORM_EPS = 1e-6
GN_EPS = 64e-5

A_HEADS = 8
A_QK_DIM = 64
A_V_DIM = 128
B_HEADS = 8
B_NOPE = 128
B_ROPE = 64
B_V = 128
B_Q_LORA = 512
B_KV_LORA = 256
AB_WIDTH = A_HEADS * A_V_DIM + B_HEADS * B_V
C_HEAD = 64
C_HEADS = D_MODEL // C_HEAD
C_WIDTH = C_HEADS * C_HEAD
C_LORA = 96
LAM_INIT_L0 = 0.8 - 0.6 * math.exp(-0.3 * 0)

LANES = 128
SUBLANES = 8
LORA_PAD = 128
MOD_ROWS = 8
VMEM_LIMIT = 56 * 1024 * 1024

L0_COLS = 6144
L0_GATE_A, L0_GATE_B, L0_AQ, L0_AK, L0_AV = 0, 1, 2, 3, 4
L0_CQ = 10
L0_CKV = 22
L0_KPE = 46
L1_COLS = 4 * C_WIDTH + 4 * LORA_PAD
L1_LORA = 16


def _cparams(*sem):
    return pltpu.CompilerParams(dimension_semantics=sem, vmem_limit_bytes=VMEM_LIMIT)


def _silu(x):
    return x / (1.0 + jnp.exp(-x))


def _dot(a, b):
    return jnp.dot(a, b, preferred_element_type=F32)


def _dot_nt(a, b):
    return lax.dot_general(a, b, (((1,), (1,)), ((), ())), preferred_element_type=F32)


def _split_dot(x, w):
    hi = x.astype(BF16)
    lo = (x - hi.astype(F32)).astype(BF16)
    return _dot(hi, w) + _dot(lo, w)


def _rope(x, cos, sin):
    lane = lax.broadcasted_iota(jnp.int32, x.shape, 1)
    low = (lane % 32) < 16
    rot = jnp.where(low, pltpu.roll(x, LANES - 16, 1), pltpu.roll(x, 16, 1))
    return x * cos + rot * sin


def _adaln_body(c_ref, w_ref, b_ref, o_ref):
    s = _silu(c_ref[...]).astype(BF16)
    o_ref[0] = _dot(s, w_ref[0].astype(BF16)) + b_ref[0]


def _adaln(cond, mod_w, mod_b):
    depth, d, n = mod_w.shape
    tn = 1024
    return pl.pallas_call(
        _adaln_body,
        grid=(depth, n // tn),
        in_specs=[
            pl.BlockSpec((MOD_ROWS, d), lambda l, j: (0, 0)),
            pl.BlockSpec((1, d, tn), lambda l, j: (l, 0, j)),
            pl.BlockSpec((1, 1, tn), lambda l, j: (l, 0, j)),
        ],
        out_specs=pl.BlockSpec((1, MOD_ROWS, tn), lambda l, j: (l, 0, j)),
        out_shape=jax.ShapeDtypeStruct((depth, MOD_ROWS, n), F32),
        compiler_params=_cparams("parallel", "parallel"),
        name="adaln",
    )(cond, mod_w, mod_b.reshape(depth, 1, n))


def _inproj_body(x_ref, sh_ref, sc_ref, g_ref, w_ref, *rest, seq_len):
    if seq_len:
        mu_ref, o_ref, h_ref = rest
    else:
        o_ref, h_ref = rest

    @pl.when(pl.program_id(1) == 0)
    def _():
        x = x_ref[...]
        y = x * lax.rsqrt(jnp.mean(x * x, axis=-1, keepdims=True) + NORM_EPS) * g_ref[...]
        h_ref[...] = (y * (1.0 + sc_ref[0]) + sh_ref[0]).astype(BF16)

    p = _dot(h_ref[...], w_ref[...])
    if seq_len:
        tm = p.shape[0]
        row = lax.broadcasted_iota(jnp.int32, (tm, 1), 0) % seq_len
        prev = jnp.where(row == 0, 0.0, pltpu.roll(p, 1, 0))
        nxt = jnp.where(row == seq_len - 1, 0.0, pltpu.roll(p, tm - 1, 0))
        p = p + mu_ref[0:1, :] * (prev - p) + mu_ref[1:2, :] * (nxt - p)
    o_ref[...] = p


def _inproj(x, shift, scale, g, w, mu, *, rows_per_mod, seq_len, tm, tn):
    m, d = x.shape
    n = w.shape[1]
    mod_map = lambda i, j: ((i * tm) // rows_per_mod, 0, 0)
    in_specs = [
        pl.BlockSpec((tm, d), lambda i, j: (i, 0)),
        pl.BlockSpec((1, 1, d), mod_map),
        pl.BlockSpec((1, 1, d), mod_map),
        pl.BlockSpec((1, d), lambda i, j: (0, 0)),
        pl.BlockSpec((d, tn), lambda i, j: (0, j)),
    ]
    args = [x, shift, scale, g, w]
    if mu is not None:
        in_specs.append(pl.BlockSpec((2, tn), lambda i, j: (0, j)))
        args.append(mu)
    return pl.pallas_call(
        functools.partial(_inproj_body, seq_len=seq_len if mu is not None else 0),
        grid=(m // tm, n // tn),
        in_specs=in_specs,
        out_specs=pl.BlockSpec((tm, tn), lambda i, j: (i, j)),
        out_shape=jax.ShapeDtypeStruct((m, n), F32),
        scratch_shapes=[pltpu.VMEM((tm, d), BF16)],
        compiler_params=_cparams("parallel", "arbitrary"),
        name="inproj",
    )(*args)


def _rms(x, g):
    return x * lax.rsqrt(jnp.mean(x * x, axis=-1, keepdims=True) + NORM_EPS) * g


def _mla_prep_body(cq_ref, ckv_ref, qg_ref, kg_ref, wuq_ref, wukv_ref, qb_ref, ckvn_ref, kvb_ref):
    qb_ref[...] = _dot(_rms(cq_ref[...], qg_ref[...]).astype(BF16), wuq_ref[...])
    ckv = _rms(ckv_ref[...], kg_ref[...])
    ckvn_ref[...] = ckv
    kvb_ref[...] = _dot(ckv.astype(BF16), wukv_ref[...])


def _mla_prep(proj, q_norm_g, kv_norm_g, w_uq, w_ukv, *, tm):
    m = proj.shape[0]
    nq, nkv = w_uq.shape[1], w_ukv.shape[1]
    const = lambda i: (0, 0)
    return pl.pallas_call(
        _mla_prep_body,
        grid=(m // tm,),
        in_specs=[
            pl.BlockSpec((tm, B_Q_LORA), lambda i: (i, L0_CQ)),
            pl.BlockSpec((tm, B_KV_LORA), lambda i: (i, L0_CKV)),
            pl.BlockSpec((1, B_Q_LORA), const),
            pl.BlockSpec((1, B_KV_LORA), const),
            pl.BlockSpec((B_Q_LORA, nq), const),
            pl.BlockSpec((B_KV_LORA, nkv), const),
        ],
        out_specs=[
            pl.BlockSpec((tm, nq), lambda i: (i, 0)),
            pl.BlockSpec((tm, B_KV_LORA), lambda i: (i, 0)),
            pl.BlockSpec((tm, nkv), lambda i: (i, 0)),
        ],
        out_shape=[
            jax.ShapeDtypeStruct((m, nq), F32),
            jax.ShapeDtypeStruct((m, B_KV_LORA), F32),
            jax.ShapeDtypeStruct((m, nkv), F32),
        ],
        compiler_params=_cparams("parallel"),
        name="mla_prep",
    )(proj, proj, q_norm_g, kv_norm_g, w_uq, w_ukv)


def _matmul_body(x_ref, w_ref, o_ref):
    o_ref[...] = _dot(x_ref[...].astype(BF16), w_ref[...])


def _matmul(x, w, *, tm):
    m, k = x.shape
    n = w.shape[1]
    return pl.pallas_call(
        _matmul_body,
        grid=(m // tm,),
        in_specs=[pl.BlockSpec((tm, k), lambda i: (i, 0)), pl.BlockSpec((k, n), lambda i: (0, 0))],
        out_specs=pl.BlockSpec((tm, n), lambda i: (i, 0)),
        out_shape=jax.ShapeDtypeStruct((m, n), F32),
        compiler_params=_cparams("parallel"),
        name="matmul",
    )(x, w)


def _softmax_parts(scores):
    m = functools.reduce(jnp.maximum, [jnp.max(s, axis=-1, keepdims=True) for s in scores])
    ps = [jnp.exp(s - m) for s in scores]
    denom = functools.reduce(jnp.add, [jnp.sum(p, axis=-1, keepdims=True) for p in ps])
    return ps, denom


def _diff_attn_body(*refs, dec, tq):
    if dec:
        (q_ref, k_ref, v_ref, kc_ref, vc_ref, gate_ref, cos_ref, sin_ref, lam_ref, sg_ref,
         o_ref, kb, vb, kcb, vcb) = refs
    else:
        q_ref, k_ref, v_ref, gate_ref, lam_ref, sg_ref, o_ref, kb, vb = refs
    qi = pl.program_id(1)

    @pl.when(qi == 0)
    def _():
        for h in range(A_HEADS):
            hs = slice(h * LANES, (h + 1) * LANES)
            kt = k_ref[0, :, hs]
            if dec:
                kt = _rope(kt, cos_ref[...], sin_ref[...])
                kcb[:, hs] = kc_ref[0, :, hs].astype(BF16)
                vcb[:, hs] = vc_ref[0, :, hs].astype(BF16)
            kb[:, hs] = kt.astype(BF16)
            vb[:, hs] = v_ref[0, :, hs].astype(BF16)

    lp = lam_ref[...]
    lam = (jnp.exp(jnp.sum(lp[0:1] * lp[1:2], keepdims=True))
           - jnp.exp(jnp.sum(lp[2:3] * lp[3:4], keepdims=True)) + LAM_INIT_L0)
    if dec:
        row0 = pl.multiple_of(qi * tq, tq)
        cq, sq = cos_ref[pl.ds(row0, tq), :], sin_ref[pl.ds(row0, tq), :]
    first = lax.broadcasted_iota(jnp.int32, (1, LANES), 1) < A_QK_DIM
    for h in range(A_HEADS):
        hs = slice(h * LANES, (h + 1) * LANES)
        qh = q_ref[0, :, hs]
        if dec:
            qh = _rope(qh, cq, sq)
        qh = qh * (A_QK_DIM ** -0.5)
        q1 = jnp.where(first, qh, 0.0).astype(BF16)
        q2 = jnp.where(first, 0.0, qh).astype(BF16)
        keys = [kb[:, hs]] + ([kcb[:, hs]] if dec else [])
        vals = [vb[:, hs]] + ([vcb[:, hs]] if dec else [])
        p1, l1 = _softmax_parts([_dot_nt(q1, kp) for kp in keys])
        p2, l2 = _softmax_parts([_dot_nt(q2, kp) for kp in keys])
        a1 = 1.0 / l1
        a2 = lam / l2
        o = functools.reduce(jnp.add, [_dot((x1 * a1 - x2 * a2).astype(BF16), vp)
                                       for x1, x2, vp in zip(p1, p2, vals)])
        o = _rms(o, sg_ref[...]) * (1.0 - LAM_INIT_L0)
        o_ref[0, :, hs] = (o * _silu(gate_ref[0, :, hs])).astype(BF16)


def _diff_attn(proj, ctx_k, ctx_v, cos, sin, diff_lambda, subln_g, *, tq):
    bsz, t_len, _ = proj.shape
    dec = ctx_k is not None
    w = A_HEADS * LANES
    full = lambda col: pl.BlockSpec((1, t_len, w), lambda b, i: (b, 0, col))
    in_specs = [pl.BlockSpec((1, tq, w), lambda b, i: (b, i, L0_AQ)), full(L0_AK), full(L0_AV)]
    args = [proj, proj, proj]
    scratch = [pltpu.VMEM((t_len, w), BF16), pltpu.VMEM((t_len, w), BF16)]
    if dec:
        p_len = ctx_k.shape[1]
        in_specs += [pl.BlockSpec((1, p_len, w), lambda b, i: (b, 0, 0))] * 2
        args += [ctx_k, ctx_v]
        scratch += [pltpu.VMEM((p_len, w), BF16), pltpu.VMEM((p_len, w), BF16)]
    in_specs.append(pl.BlockSpec((1, tq, w), lambda b, i: (b, i, L0_GATE_A)))
    args.append(proj)
    if dec:
        in_specs += [pl.BlockSpec((t_len, LANES), lambda b, i: (0, 0))] * 2
        args += [cos, sin]
    in_specs += [pl.BlockSpec((4, A_QK_DIM), lambda b, i: (0, 0)),
                 pl.BlockSpec((1, A_V_DIM), lambda b, i: (0, 0))]
    args += [diff_lambda, subln_g]
    return pl.pallas_call(
        functools.partial(_diff_attn_body, dec=dec, tq=tq),
        grid=(bsz, t_len // tq),
        in_specs=in_specs,
        out_specs=pl.BlockSpec((1, tq, w), lambda b, i: (b, i, 0)),
        out_shape=jax.ShapeDtypeStruct((bsz, t_len, w), BF16),
        scratch_shapes=scratch,
        compiler_params=_cparams("parallel", "arbitrary"),
        name="diff_attn",
    )(*args)


def _mla_attn_body(*refs, dec, tq):
    if dec:
        (qn_ref, qp_ref, kv_ref, kpe_ref, kvc_ref, kpec_ref, gate_ref, cos_ref, sin_ref,
         o_ref, kvb, kpeb, kvcb, kpecb) = refs
    else:
        qn_ref, qp_ref, kv_ref, kpe_ref, gate_ref, o_ref, kvb, kpeb = refs
    qi = pl.program_id(1)

    @pl.when(qi == 0)
    def _():
        kvb[...] = kv_ref[0].astype(BF16)
        kp = kpe_ref[0]
        if dec:
            kp = _rope(kp, cos_ref[...], sin_ref[...])
            kvcb[...] = kvc_ref[0].astype(BF16)
            kpecb[...] = kpec_ref[0].astype(BF16)
        kpeb[...] = kp.astype(BF16)

    if dec:
        row0 = pl.multiple_of(qi * tq, tq)
        cq, sq = cos_ref[pl.ds(row0, tq), :], sin_ref[pl.ds(row0, tq), :]
    scale = (B_NOPE + B_ROPE) ** -0.5
    first = lax.broadcasted_iota(jnp.int32, (1, LANES), 1) < B_ROPE
    for h in range(B_HEADS):
        hs = slice(h * LANES, (h + 1) * LANES)
        if h % 2 == 0:
            pair = qp_ref[0, :, (h // 2) * LANES:(h // 2 + 1) * LANES]
            if dec:
                pair = _rope(pair, cq, sq)
            pair = pair * scale
        qp = (jnp.where(first, pair, 0.0) if h % 2 == 0 else jnp.where(first, 0.0, pair)).astype(BF16)
        qn = (qn_ref[0, :, hs] * scale).astype(BF16)
        ks = slice(2 * h * LANES, (2 * h + 1) * LANES)
        vs = slice((2 * h + 1) * LANES, (2 * h + 2) * LANES)
        scores = [_dot_nt(qn, kvb[:, ks]) + _dot_nt(qp, kpeb[...])]
        vals = [kvb[:, vs]]
        if dec:
            scores.append(_dot_nt(qn, kvcb[:, ks]) + _dot_nt(qp, kpecb[...]))
            vals.append(kvcb[:, vs])
        ps, denom = _softmax_parts(scores)
        inv = 1.0 / denom
        o = functools.reduce(jnp.add, [_dot((p * inv).astype(BF16), vp) for p, vp in zip(ps, vals)])
        o_ref[0, :, hs] = (o * _silu(gate_ref[0, :, hs])).astype(BF16)


def _mla_attn(proj, q_b, kv_b, ctx_kv, ctx_kpe, cos, sin, *, tq):
    bsz, t_len, _ = proj.shape
    dec = ctx_kv is not None
    w = B_HEADS * LANES
    nkv = kv_b.shape[2]
    in_specs = [
        pl.BlockSpec((1, tq, w), lambda b, i: (b, i, 0)),
        pl.BlockSpec((1, tq, B_HEADS * B_ROPE), lambda b, i: (b, i, 2)),
        pl.BlockSpec((1, t_len, nkv), lambda b, i: (b, 0, 0)),
        pl.BlockSpec((1, t_len, LANES), lambda b, i: (b, 0, L0_KPE)),
    ]
    args = [q_b, q_b, kv_b, proj]
    scratch = [pltpu.VMEM((t_len, nkv), BF16), pltpu.VMEM((t_len, LANES), BF16)]
    if dec:
        p_len = ctx_kv.shape[1]
        in_specs += [pl.BlockSpec((1, p_len, nkv), lambda b, i: (b, 0, 0)),
                     pl.BlockSpec((1, p_len, LANES), lambda b, i: (b, 0, 0))]
        args += [ctx_kv, ctx_kpe]
        scratch += [pltpu.VMEM((p_len, nkv), BF16), pltpu.VMEM((p_len, LANES), BF16)]
    in_specs.append(pl.BlockSpec((1, tq, w), lambda b, i: (b, i, L0_GATE_B)))
    args.append(proj)
    if dec:
        in_specs += [pl.BlockSpec((t_len, LANES), lambda b, i: (0, 0))] * 2
        args += [cos, sin]
    return pl.pallas_call(
        functools.partial(_mla_attn_body, dec=dec, tq=tq),
        grid=(bsz, t_len // tq),
        in_specs=in_specs,
        out_specs=pl.BlockSpec((1, tq, w), lambda b, i: (b, i, 0)),
        out_shape=jax.ShapeDtypeStruct((bsz, t_len, w), BF16),
        scratch_shapes=scratch,
        compiler_params=_cparams("parallel", "arbitrary"),
        name="mla_attn",
    )(*args)


def _outproj_body(x_ref, ya_ref, yb_ref, w_ref, gate_ref, o_ref):
    half = ya_ref.shape[1]
    acc = _dot(ya_ref[...], w_ref[0:half, :]) + _dot(yb_ref[...], w_ref[half:, :])
    o_ref[...] = x_ref[...] + gate_ref[0] * acc


def _outproj(x, ya, yb, w, gate, *, rows_per_mod, tm, tn):
    m, d = x.shape
    k = w.shape[0]
    return pl.pallas_call(
        _outproj_body,
        grid=(m // tm, d // tn),
        in_specs=[
            pl.BlockSpec((tm, tn), lambda i, j: (i, j)),
            pl.BlockSpec((tm, k // 2), lambda i, j: (i, 0)),
            pl.BlockSpec((tm, k // 2), lambda i, j: (i, 0)),
            pl.BlockSpec((k, tn), lambda i, j: (0, j)),
            pl.BlockSpec((1, 1, tn), lambda i, j: ((i * tm) // rows_per_mod, 0, j)),
        ],
        out_specs=pl.BlockSpec((tm, tn), lambda i, j: (i, j)),
        out_shape=jax.ShapeDtypeStruct((m, d), F32),
        compiler_params=_cparams("parallel", "parallel"),
        name="outproj",
    )(x, ya, yb, w, gate)


def _head_sums(x, ones_bd):
    return _split_dot(x, ones_bd)


def _rwkv_prep_body(r_ref, k_ref, v_ref, lora_ref, w0_ref, w2_ref, a0_ref, a2_ref, kk_ref_p, ka_ref, rk_ref,
                    bd_ref, kk_out, w_out, kz_out, kka_out, bonus_out):
    ones_bd = bd_ref[...]
    lora = lora_ref[...]
    wl, al = [], []
    for z in range(2):
        wd = jnp.tanh(lora[:, z * LORA_PAD:(z + 1) * LORA_PAD]).astype(BF16)
        ad = lora[:, (2 + z) * LORA_PAD:(3 + z) * LORA_PAD].astype(BF16)
        wl.append(_dot(wd, w2_ref[z]))
        al.append(_dot(ad, a2_ref[z]))
    for c in range(C_WIDTH // LANES):
        cs = slice(c * LANES, (c + 1) * LANES)
        r, k, v = r_ref[:, cs], k_ref[:, cs], v_ref[:, cs]
        kk = k * kk_ref_p[:, cs]
        kk = kk * lax.rsqrt(jnp.maximum(_head_sums(kk * kk, ones_bd), 1e-12))
        kk_out[:, cs] = -kk
        bonus = jnp.zeros_like(v)
        for z in range(2):
            u = -(w0_ref[z:z + 1, cs] + wl[z][:, cs])
            softplus = jnp.maximum(u, 0.0) + jnp.log(1.0 + jnp.exp(-jnp.abs(u)))
            w_out[z, :, cs] = jnp.exp(-jnp.exp(-softplus - 0.5))
            a = 1.0 / (1.0 + jnp.exp(-(a0_ref[z:z + 1, cs] + al[z][:, cs])))
            kz = k * (1.0 + (a - 1.0) * ka_ref[:, cs])
            kz_out[z, :, cs] = kz
            kka_out[z, :, cs] = kk * a
            bonus = bonus + _head_sums(r * kz * rk_ref[:, cs], ones_bd) * v
        bonus_out[:, cs] = bonus


def _rwkv_prep(proj, w0, w2, a0, a2, k_k, k_a, r_k, ones_bd, *, tm):
    m = proj.shape[0]
    const2 = lambda i: (0, 0)
    const3 = lambda i: (0, 0, 0)
    col = lambda c: pl.BlockSpec((tm, C_WIDTH), lambda i: (i, c))
    per_dir = jax.ShapeDtypeStruct((2, m, C_WIDTH), F32)
    single = jax.ShapeDtypeStruct((m, C_WIDTH), F32)
    return pl.pallas_call(
        _rwkv_prep_body,
        grid=(m // tm,),
        in_specs=[
            col(0), col(1), col(2),
            pl.BlockSpec((tm, 4 * LORA_PAD), lambda i: (i, L1_LORA)),
            pl.BlockSpec((2, C_WIDTH), const2),
            pl.BlockSpec((2, LORA_PAD, C_WIDTH), const3),
            pl.BlockSpec((2, C_WIDTH), const2),
            pl.BlockSpec((2, LORA_PAD, C_WIDTH), const3),
            pl.BlockSpec((1, C_WIDTH), const2),
            pl.BlockSpec((1, C_WIDTH), const2),
            pl.BlockSpec((1, C_WIDTH), const2),
            pl.BlockSpec((LANES, LANES), const2),
        ],
        out_specs=[
            pl.BlockSpec((tm, C_WIDTH), lambda i: (i, 0)),
            pl.BlockSpec((2, tm, C_WIDTH), lambda i: (0, i, 0)),
            pl.BlockSpec((2, tm, C_WIDTH), lambda i: (0, i, 0)),
            pl.BlockSpec((2, tm, C_WIDTH), lambda i: (0, i, 0)),
            pl.BlockSpec((tm, C_WIDTH), lambda i: (i, 0)),
        ],
        out_shape=[single, per_dir, per_dir, per_dir, single],
        compiler_params=_cparams("parallel"),
        name="rwkv_prep",
    )(proj, proj, proj, proj, w0, w2, a0, a2, k_k, k_a, r_k, ones_bd)


def _scan_body(r_ref, nkk_ref, v_ref, w_ref, kka_ref, k_ref, s0_ref, y_ref, sfin_ref, s_scr, *, tc, reverse):
    c = pl.program_id(1)

    @pl.when(c == 0)
    def _():
        s_scr[...] = s0_ref[0]

    first = tc - 1 if reverse else 0
    sa0 = jnp.zeros((C_HEAD, LANES), F32)
    for j in range(C_HEAD):
        sa0 = sa0 + s_scr[j] * nkk_ref[0, first, j:j + 1, :]

    def step(i, sa):
        t = (tc - 1 - i) if reverse else i
        t_next = jnp.maximum(t - 1, 0) if reverse else jnp.minimum(t + 1, tc - 1)
        v = v_ref[0, t]
        y = jnp.zeros((C_HEAD, LANES), F32)
        sa_next = jnp.zeros((C_HEAD, LANES), F32)
        for j in range(C_HEAD):
            row = lambda ref, tt: ref[0, tt, j:j + 1, :]
            s = s_scr[j] * row(w_ref, t) + sa * row(kka_ref, t) + v * row(k_ref, t)
            s_scr[j] = s
            y = y + s * row(r_ref, t)
            sa_next = sa_next + s * row(nkk_ref, t_next)
        y_ref[0, t] = y
        return sa_next

    lax.fori_loop(0, tc, step, sa0)

    @pl.when(c == pl.num_programs(1) - 1)
    def _():
        sfin_ref[0] = s_scr[...]


def _scan(r, nkk, v, w, kka, k, s0, *, reverse, tc):
    g, t_len = r.shape[:2]
    n_c = t_len // tc
    tmap = (lambda gi, ci: (gi, n_c - 1 - ci, 0, 0)) if reverse else (lambda gi, ci: (gi, ci, 0, 0))
    blk = pl.BlockSpec((1, tc, C_HEAD, LANES), tmap)
    st = pl.BlockSpec((1, C_HEAD, C_HEAD, LANES), lambda gi, ci: (gi, 0, 0, 0))
    return pl.pallas_call(
        functools.partial(_scan_body, tc=tc, reverse=reverse),
        grid=(g, n_c),
        in_specs=[blk] * 6 + [st],
        out_specs=[blk, st],
        out_shape=[jax.ShapeDtypeStruct(r.shape, F32), jax.ShapeDtypeStruct(s0.shape, F32)],
        scratch_shapes=[pltpu.VMEM((C_HEAD, C_HEAD, LANES), F32)],
        compiler_params=_cparams("parallel", "arbitrary"),
        name="wkv_scan",
    )(r, nkk, v, w, kka, k, s0)


def _to_lanes(x, bsz, t_len):
    per = LANES // C_HEADS
    x = x.reshape(bsz // per, per, t_len, C_HEADS, C_HEAD)
    return x.transpose(0, 2, 4, 1, 3).reshape(bsz // per, t_len, C_HEAD, LANES)


def _from_lanes(y, bsz, t_len):
    per = LANES // C_HEADS
    y = y.reshape(bsz // per, t_len, C_HEAD, per, C_HEADS)
    return y.transpose(0, 3, 1, 4, 2).reshape(bsz * t_len, C_WIDTH)


def _state_to_lanes(s):
    per = LANES // C_HEADS
    bsz = s.shape[0]
    s = s.reshape(bsz // per, per, C_HEADS, C_HEAD, C_HEAD)
    return s.transpose(0, 4, 3, 1, 2).reshape(bsz // per, C_HEAD, C_HEAD, LANES)


def _state_from_lanes(s):
    per = LANES // C_HEADS
    g = s.shape[0]
    s = s.reshape(g, C_HEAD, C_HEAD, per, C_HEADS)
    return s.transpose(0, 3, 4, 2, 1).reshape(g * per, C_HEADS, C_HEAD, C_HEAD)


def _rwkv_out_body(x_ref, y0_ref, y1_ref, bonus_ref, gate_ref, lnw_ref, lnb_ref, bd_ref, w_ref, mg_ref, fg_ref,
                   o_ref, yg_ref):
    ones_bd = bd_ref[...]
    inv = 1.0 / C_HEAD
    for c in range(C_WIDTH // LANES):
        cs = slice(c * LANES, (c + 1) * LANES)
        tot = bonus_ref[:, cs]
        for y_ref in (y0_ref, y1_ref):
            y = y_ref[:, cs]
            dev = y - _head_sums(y, ones_bd) * inv
            var = _head_sums(dev * dev, ones_bd) * inv
            tot = tot + dev * lax.rsqrt(var + GN_EPS) * lnw_ref[:, cs] + lnb_ref[:, cs]
        yg_ref[:, cs] = (tot * _silu(gate_ref[:, cs])).astype(BF16)
    xn = x_ref[...] + mg_ref[0] * _dot(yg_ref[...], w_ref[...])
    o_ref[...] = _rms(xn, fg_ref[...])


def _rwkv_out(x, y0, y1, bonus, proj, ln_w, ln_b, ones_bd, w, gate, final_g, *, rows_per_mod, tm):
    m, d = x.shape
    row = lambda i: (i, 0)
    const = lambda i: (0, 0)
    tile = pl.BlockSpec((tm, d), row)
    vec = pl.BlockSpec((1, d), const)
    return pl.pallas_call(
        _rwkv_out_body,
        grid=(m // tm,),
        in_specs=[
            tile, tile, tile, tile,
            pl.BlockSpec((tm, C_WIDTH), lambda i: (i, 3)),
            vec, vec,
            pl.BlockSpec((LANES, LANES), const),
            pl.BlockSpec((C_WIDTH, d), const, pipeline_mode=pl.Buffered(1)),
            pl.BlockSpec((1, 1, d), lambda i: ((i * tm) // rows_per_mod, 0, 0)),
            vec,
        ],
        out_specs=tile,
        out_shape=jax.ShapeDtypeStruct((m, d), F32),
        scratch_shapes=[pltpu.VMEM((tm, C_WIDTH), BF16)],
        compiler_params=_cparams("parallel"),
        name="rwkv_out",
    )(x, y0, y1, bonus, proj, ln_w, ln_b, ones_bd, w, gate, final_g)


def _rope_tables(t_len):
    pos = jnp.arange(t_len)
    half = 16
    freqs = ROPE_BASE ** (-jnp.arange(half, dtype=F32) / half)
    ang_r = (pos // GRID_W).astype(F32)[:, None] * freqs
    ang_c = (pos % GRID_W).astype(F32)[:, None] * freqs
    cos = jnp.concatenate([jnp.cos(ang_r)] * 2 + [jnp.cos(ang_c)] * 2, axis=-1)
    sin = jnp.concatenate([-jnp.sin(ang_r), jnp.sin(ang_r), -jnp.sin(ang_c), jnp.sin(ang_c)], axis=-1)
    return jnp.tile(cos, (1, 2)), jnp.tile(sin, (1, 2))


def _l0_w_in_layout(w):
    d = w.shape[0]
    aq, ak, av, cq, ckv, kpe, gate = jnp.split(w, [1024, 2048, 3072, 3584, 3840, 3904], axis=1)
    pad = jnp.zeros((d, L0_COLS - 6016), w.dtype)
    return jnp.concatenate([gate, aq, ak, av, cq, ckv, kpe, kpe, pad], axis=1).astype(BF16)


def _l1_cols_layout(w):
    main, wd, ad = jnp.split(w, [4 * C_WIDTH, 4 * C_WIDTH + 2 * C_LORA], axis=1)
    pad = jnp.zeros((w.shape[0], LORA_PAD - C_LORA), w.dtype)
    parts = [main]
    for seg in (wd, ad):
        for z in range(2):
            parts += [seg[:, z * C_LORA:(z + 1) * C_LORA], pad]
    return jnp.concatenate(parts, axis=1)


def _layer0(x, mods, rows_per_mod, seq_len, weights, ctx, tables):
    w_in, w_out, diff_lambda, subln_g, q_norm_g, w_uq, kv_norm_g, w_ukv = weights
    shift, scale, gate, norm_g = mods
    m = x.shape[0]
    bsz = m // seq_len
    proj = _inproj(x, shift, scale, norm_g, w_in, None, rows_per_mod=rows_per_mod, seq_len=seq_len, tm=512, tn=1024)
    q_b, ckv_n, kv_b = _mla_prep(proj, q_norm_g, kv_norm_g, w_uq, w_ukv, tm=512)
    proj3 = proj.reshape(bsz, seq_len, L0_COLS)
    q_b3 = q_b.reshape(bsz, seq_len, -1)
    kv_b3 = kv_b.reshape(bsz, seq_len, -1)
    if ctx is None:
        ya = _diff_attn(proj3, None, None, None, None, diff_lambda, subln_g, tq=256)
        yb = _mla_attn(proj3, q_b3, kv_b3, None, None, None, None, tq=256)
    else:
        k_ctx, v_ctx, ckv_ctx, kpe_ctx = ctx
        p_len = k_ctx.shape[1]
        cos, sin = tables
        kv_ctx = _matmul(ckv_ctx.reshape(bsz * p_len, B_KV_LORA), w_ukv, tm=512).reshape(bsz, p_len, -1)
        ya = _diff_attn(proj3, k_ctx.reshape(bsz, p_len, -1), v_ctx.reshape(bsz, p_len, -1), cos, sin,
                        diff_lambda, subln_g, tq=256)
        yb = _mla_attn(proj3, q_b3, kv_b3, kv_ctx, jnp.concatenate([kpe_ctx, kpe_ctx], axis=-1), cos, sin, tq=256)
    x_new = _outproj(x, ya.reshape(m, -1), yb.reshape(m, -1), w_out, gate, rows_per_mod=rows_per_mod, tm=512, tn=1024)
    return x_new, proj, ckv_n


def _layer1(x, mods, rows_per_mod, seq_len, weights, states, final_g):
    w_in, w_out, mu, w0, w2, a0, a2, k_k, k_a, r_k, ln_w, ln_b, ones_bd = weights
    shift, scale, gate, norm_g = mods
    m = x.shape[0]
    bsz = m // seq_len
    proj = _inproj(x, shift, scale, norm_g, w_in, mu, rows_per_mod=rows_per_mod, seq_len=seq_len, tm=1024, tn=512)
    nkk, w, kz, kka, bonus = _rwkv_prep(proj, w0, w2, a0, a2, k_k, k_a, r_k, ones_bd, tm=256)
    lanes = lambda a: _to_lanes(a, bsz, seq_len)
    r_l, v_l, nkk_l = lanes(proj[:, :C_WIDTH]), lanes(proj[:, 2 * C_WIDTH:3 * C_WIDTH]), lanes(nkk)
    ys, finals = [], []
    for z in range(2):
        y, s_fin = _scan(r_l, nkk_l, v_l, lanes(w[z]), lanes(kka[z]), lanes(kz[z]), _state_to_lanes(states[z]),
                         reverse=(z == 1), tc=32)
        ys.append(_from_lanes(y, bsz, seq_len))
        finals.append(_state_from_lanes(s_fin))
    out = _rwkv_out(x, ys[0], ys[1], bonus, proj, ln_w, ln_b, ones_bd, w_out, gate, final_g,
                    rows_per_mod=rows_per_mod, tm=256)
    return out, finals


def kernel(x_prompt, x_sample, cache_l0_a_k, cache_l0_a_v, cache_l0_mla_ckv, cache_l0_mla_kpe, state_l1_fwd, state_l1_bwd, c, c_ctx, mod_w, mod_b, norm_g, final_norm_g, l0_w_in, l0_w_out, l0_diff_lambda, l0_subln_g, l0_q_norm_g, l0_w_uq, l0_kv_norm_g, l0_w_ukv, l1_w_in, l1_w_out, l1_mu, l1_w0, l1_w2, l1_a0, l1_a2, l1_k_k, l1_k_a, l1_r_k, l1_ln_w, l1_ln_b):
    d = D_MODEL
    bp, tp, _ = x_prompt.shape
    bs, ts, _ = x_sample.shape

    cond = jnp.concatenate([c_ctx[None, :], c, jnp.zeros((MOD_ROWS - 1 - bs, d), F32)], axis=0)
    mods = _adaln(cond, mod_w, mod_b)

    def mod_rows(layer, lo, hi):
        rows = mods[layer, lo:hi]
        shift, scale, gate = (rows[:, i * d:(i + 1) * d].reshape(hi - lo, 1, d) for i in range(3))
        return shift, scale, gate, norm_g[layer].reshape(1, d)

    w_uq = l0_w_uq.reshape(B_Q_LORA, B_HEADS, B_NOPE + B_ROPE)
    w_uq = jnp.concatenate([w_uq[:, :, :B_NOPE].reshape(B_Q_LORA, -1), w_uq[:, :, B_NOPE:].reshape(B_Q_LORA, -1)],
                           axis=1).astype(BF16)
    l0_weights = (_l0_w_in_layout(l0_w_in), l0_w_out.astype(BF16), l0_diff_lambda, l0_subln_g.reshape(1, -1),
                  l0_q_norm_g.reshape(1, -1), w_uq, l0_kv_norm_g.reshape(1, -1), l0_w_ukv.astype(BF16))
    lora_pad = jnp.zeros((2, LORA_PAD - C_LORA, C_WIDTH), F32)
    lane = jnp.arange(LANES)
    ones_bd = ((lane[:, None] // C_HEAD) == (lane[None, :] // C_HEAD)).astype(BF16)
    l1_weights = (_l1_cols_layout(l1_w_in).astype(BF16), l1_w_out.astype(BF16), _l1_cols_layout(l1_mu),
                  l1_w0, jnp.concatenate([l1_w2, lora_pad], axis=1).astype(BF16),
                  l1_a0, jnp.concatenate([l1_a2, lora_pad], axis=1).astype(BF16),
                  l1_k_k.reshape(1, -1), l1_k_a.reshape(1, -1), l1_r_k.reshape(1, -1),
                  l1_ln_w.reshape(1, -1), l1_ln_b.reshape(1, -1), ones_bd)
    tables = _rope_tables(ts)
    final_g = final_norm_g.reshape(1, d)

    xp = x_prompt.reshape(bp * tp, d)
    xs = x_sample.reshape(bs * ts, d)
    ctx0 = (cache_l0_a_k, cache_l0_a_v, cache_l0_mla_ckv, cache_l0_mla_kpe)

    xp1, proj_p, ckv_p = _layer0(xp, mod_rows(0, 0, 1), bp * tp, tp, l0_weights, None, None)
    xs1, _, _ = _layer0(xs, mod_rows(0, 1, 1 + bs), ts, ts, l0_weights, ctx0, tables)

    zero_state = jnp.zeros((bp, C_HEADS, C_HEAD, C_HEAD), F32)
    y_prompt, finals = _layer1(xp1, mod_rows(1, 0, 1), bp * tp, tp, l1_weights, (zero_state, zero_state), final_g)
    y_sample, _ = _layer1(xs1, mod_rows(1, 1, 1 + bs), ts, ts, l1_weights, (state_l1_fwd, state_l1_bwd), final_g)

    new_a_k = proj_p[:, 3072:4096].reshape(bp, tp, A_HEADS, 2, A_QK_DIM)
    new_a_v = proj_p[:, 4096:5120].reshape(bp, tp, A_HEADS, A_V_DIM)
    new_ckv = ckv_p.reshape(bp, tp, B_KV_LORA)
    new_kpe = proj_p[:, 5888:5888 + B_ROPE].reshape(bp, tp, B_ROPE)
    return (y_prompt.reshape(bp, tp, d), y_sample.reshape(bs, ts, d), new_a_k, new_a_v, new_ckv, new_kpe,
            finals[0], finals[1])
```

```python
import functools
import math

import jax
import jax.numpy as jnp
from jax import lax
from jax.experimental import pallas as pl
from jax.experimental.pallas import tpu as pltpu

F32 = jnp.float32
BF16 = jnp.bfloat16

D_MODEL = 2048
GRID_W = 64
ROPE_BASE = 10000.0
NORM_EPS = 1e-6
GN_EPS = 64e-5

A_HEADS = 8
A_QK_DIM = 64
A_V_DIM = 128
B_HEADS = 8
B_NOPE = 128
B_ROPE = 64
B_V = 128
B_Q_LORA = 512
B_KV_LORA = 256
AB_WIDTH = A_HEADS * A_V_DIM + B_HEADS * B_V
C_HEAD = 64
C_HEADS = D_MODEL // C_HEAD
C_WIDTH = C_HEADS * C_HEAD
C_LORA = 96
LAM_INIT_L0 = 0.8 - 0.6 * math.exp(-0.3 * 0)

LANES = 128
SUBLANES = 8
LORA_PAD = 128
MOD_ROWS = 8
VMEM_LIMIT = 56 * 1024 * 1024

L0_COLS = 6144
L0_GATE_A, L0_GATE_B, L0_AQ, L0_AK, L0_AV = 0, 1, 2, 3, 4
L0_CQ = 10
L0_CKV = 22
L0_KPE = 46
L1_COLS = 4 * C_WIDTH + 4 * LORA_PAD
PER = LANES // C_HEADS
PITCH = 72
TT = LANES
N_LANE_ARRAYS = 7


def _cparams(*sem):
    return pltpu.CompilerParams(dimension_semantics=sem, vmem_limit_bytes=VMEM_LIMIT)


def _silu(x):
    return x / (1.0 + jnp.exp(-x))


def _dot(a, b):
    return jnp.dot(a, b, preferred_element_type=F32)


def _dot_nt(a, b):
    return lax.dot_general(a, b, (((1,), (1,)), ((), ())), preferred_element_type=F32)


def _rope(x, cos, sin):
    lane = lax.broadcasted_iota(jnp.int32, x.shape, 1)
    low = (lane % 32) < 16
    rot = jnp.where(low, pltpu.roll(x, LANES - 16, 1), pltpu.roll(x, 16, 1))
    return x * cos + rot * sin


def _adaln_body(c_ref, w_ref, b_ref, o_ref):
    s = _silu(c_ref[...]).astype(BF16)
    o_ref[0] = _dot(s, w_ref[0].astype(BF16)) + b_ref[0]


def _adaln(cond, mod_w, mod_b):
    depth, d, n = mod_w.shape
    tn = 1024
    return pl.pallas_call(
        _adaln_body,
        grid=(depth, n // tn),
        in_specs=[
            pl.BlockSpec((MOD_ROWS, d), lambda l, j: (0, 0)),
            pl.BlockSpec((1, d, tn), lambda l, j: (l, 0, j)),
            pl.BlockSpec((1, 1, tn), lambda l, j: (l, 0, j)),
        ],
        out_specs=pl.BlockSpec((1, MOD_ROWS, tn), lambda l, j: (l, 0, j)),
        out_shape=jax.ShapeDtypeStruct((depth, MOD_ROWS, n), F32),
        compiler_params=_cparams("parallel", "parallel"),
        name="adaln",
    )(cond, mod_w, mod_b.reshape(depth, 1, n))


def _inproj_body(x_ref, sh_ref, sc_ref, g_ref, w_ref, *rest, seq_len):
    if seq_len:
        mu_ref, o_ref, h_ref = rest
    else:
        o_ref, h_ref = rest

    @pl.when(pl.program_id(1) == 0)
    def _():
        x = x_ref[...]
        y = x * lax.rsqrt(jnp.mean(x * x, axis=-1, keepdims=True) + NORM_EPS) * g_ref[...]
        h_ref[...] = (y * (1.0 + sc_ref[0]) + sh_ref[0]).astype(BF16)

    p = _dot(h_ref[...], w_ref[...])
    if seq_len:
        tm = p.shape[0]
        row = lax.broadcasted_iota(jnp.int32, (tm, 1), 0) % seq_len
        prev = jnp.where(row == 0, 0.0, pltpu.roll(p, 1, 0))
        nxt = jnp.where(row == seq_len - 1, 0.0, pltpu.roll(p, tm - 1, 0))
        p = p + mu_ref[0:1, :] * (prev - p) + mu_ref[1:2, :] * (nxt - p)
    o_ref[...] = p


def _inproj(x, shift, scale, g, w, mu, *, rows_per_mod, seq_len, tm, tn):
    m, d = x.shape
    n = w.shape[1]
    mod_map = lambda i, j: ((i * tm) // rows_per_mod, 0, 0)
    in_specs = [
        pl.BlockSpec((tm, d), lambda i, j: (i, 0)),
        pl.BlockSpec((1, 1, d), mod_map),
        pl.BlockSpec((1, 1, d), mod_map),
        pl.BlockSpec((1, d), lambda i, j: (0, 0)),
        pl.BlockSpec((d, tn), lambda i, j: (0, j)),
    ]
    args = [x, shift, scale, g, w]
    if mu is not None:
        in_specs.append(pl.BlockSpec((2, tn), lambda i, j: (0, j)))
        args.append(mu)
    return pl.pallas_call(
        functools.partial(_inproj_body, seq_len=seq_len if mu is not None else 0),
        grid=(m // tm, n // tn),
        in_specs=in_specs,
        out_specs=pl.BlockSpec((tm, tn), lambda i, j: (i, j)),
        out_shape=jax.ShapeDtypeStruct((m, n), F32),
        scratch_shapes=[pltpu.VMEM((tm, d), BF16)],
        compiler_params=_cparams("parallel", "arbitrary"),
        name="inproj",
    )(*args)


def _rms(x, g):
    return x * lax.rsqrt(jnp.mean(x * x, axis=-1, keepdims=True) + NORM_EPS) * g


def _mla_prep_body(cq_ref, ckv_ref, qg_ref, kg_ref, wuq_ref, wukv_ref, qb_ref, ckvn_ref, kvb_ref):
    qb_ref[...] = _dot(_rms(cq_ref[...], qg_ref[...]).astype(BF16), wuq_ref[...])
    ckv = _rms(ckv_ref[...], kg_ref[...])
    ckvn_ref[...] = ckv
    kvb_ref[...] = _dot(ckv.astype(BF16), wukv_ref[...])


def _mla_prep(proj, q_norm_g, kv_norm_g, w_uq, w_ukv, *, tm):
    m = proj.shape[0]
    nq, nkv = w_uq.shape[1], w_ukv.shape[1]
    const = lambda i: (0, 0)
    return pl.pallas_call(
        _mla_prep_body,
        grid=(m // tm,),
        in_specs=[
            pl.BlockSpec((tm, B_Q_LORA), lambda i: (i, L0_CQ)),
            pl.BlockSpec((tm, B_KV_LORA), lambda i: (i, L0_CKV)),
            pl.BlockSpec((1, B_Q_LORA), const),
            pl.BlockSpec((1, B_KV_LORA), const),
            pl.BlockSpec((B_Q_LORA, nq), const),
            pl.BlockSpec((B_KV_LORA, nkv), const),
        ],
        out_specs=[
            pl.BlockSpec((tm, nq), lambda i: (i, 0)),
            pl.BlockSpec((tm, B_KV_LORA), lambda i: (i, 0)),
            pl.BlockSpec((tm, nkv), lambda i: (i, 0)),
        ],
        out_shape=[
            jax.ShapeDtypeStruct((m, nq), F32),
            jax.ShapeDtypeStruct((m, B_KV_LORA), F32),
            jax.ShapeDtypeStruct((m, nkv), F32),
        ],
        compiler_params=_cparams("parallel"),
        name="mla_prep",
    )(proj, proj, q_norm_g, kv_norm_g, w_uq, w_ukv)


def _matmul_body(x_ref, w_ref, o_ref):
    o_ref[...] = _dot(x_ref[...].astype(BF16), w_ref[...])


def _matmul(x, w, *, tm):
    m, k = x.shape
    n = w.shape[1]
    return pl.pallas_call(
        _matmul_body,
        grid=(m // tm,),
        in_specs=[pl.BlockSpec((tm, k), lambda i: (i, 0)), pl.BlockSpec((k, n), lambda i: (0, 0))],
        out_specs=pl.BlockSpec((tm, n), lambda i: (i, 0)),
        out_shape=jax.ShapeDtypeStruct((m, n), F32),
        compiler_params=_cparams("parallel"),
        name="matmul",
    )(x, w)


def _softmax_parts(scores):
    m = functools.reduce(jnp.maximum, [jnp.max(s, axis=-1, keepdims=True) for s in scores])
    ps = [jnp.exp(s - m) for s in scores]
    denom = functools.reduce(jnp.add, [jnp.sum(p, axis=-1, keepdims=True) for p in ps])
    return ps, denom


def _diff_attn_body(*refs, dec, tq):
    if dec:
        (q_ref, k_ref, v_ref, kc_ref, vc_ref, gate_ref, cos_ref, sin_ref, lam_ref, sg_ref,
         o_ref, kb, vb, kcb, vcb) = refs
    else:
        q_ref, k_ref, v_ref, gate_ref, lam_ref, sg_ref, o_ref, kb, vb = refs
    qi = pl.program_id(1)

    @pl.when(qi == 0)
    def _():
        for h in range(A_HEADS):
            hs = slice(h * LANES, (h + 1) * LANES)
            kt = k_ref[0, :, hs]
            if dec:
                kt = _rope(kt, cos_ref[...], sin_ref[...])
                kcb[:, hs] = kc_ref[0, :, hs].astype(BF16)
                vcb[:, hs] = vc_ref[0, :, hs].astype(BF16)
            kb[:, hs] = kt.astype(BF16)
            vb[:, hs] = v_ref[0, :, hs].astype(BF16)

    lp = lam_ref[...]
    lam = (jnp.exp(jnp.sum(lp[0:1] * lp[1:2], keepdims=True))
           - jnp.exp(jnp.sum(lp[2:3] * lp[3:4], keepdims=True)) + LAM_INIT_L0)
    if dec:
        row0 = pl.multiple_of(qi * tq, tq)
        cq, sq = cos_ref[pl.ds(row0, tq), :], sin_ref[pl.ds(row0, tq), :]
    first = lax.broadcasted_iota(jnp.int32, (1, LANES), 1) < A_QK_DIM
    for h in range(A_HEADS):
        hs = slice(h * LANES, (h + 1) * LANES)
        qh = q_ref[0, :, hs]
        if dec:
            qh = _rope(qh, cq, sq)
        qh = qh * (A_QK_DIM ** -0.5)
        q1 = jnp.where(first, qh, 0.0).astype(BF16)
        q2 = jnp.where(first, 0.0, qh).astype(BF16)
        keys = [kb[:, hs]] + ([kcb[:, hs]] if dec else [])
        vals = [vb[:, hs]] + ([vcb[:, hs]] if dec else [])
        p1, l1 = _softmax_parts([_dot_nt(q1, kp) for kp in keys])
        p2, l2 = _softmax_parts([_dot_nt(q2, kp) for kp in keys])
        a1 = 1.0 / l1
        a2 = lam / l2
        o = functools.reduce(jnp.add, [_dot((x1 * a1 - x2 * a2).astype(BF16), vp)
                                       for x1, x2, vp in zip(p1, p2, vals)])
        o = _rms(o, sg_ref[...]) * (1.0 - LAM_INIT_L0)
        o_ref[0, :, hs] = (o * _silu(gate_ref[0, :, hs])).astype(BF16)


def _diff_attn(proj, ctx_k, ctx_v, cos, sin, diff_lambda, subln_g, *, tq):
    bsz, t_len, _ = proj.shape
    dec = ctx_k is not None
    w = A_HEADS * LANES
    full = lambda col: pl.BlockSpec((1, t_len, w), lambda b, i: (b, 0, col))
    in_specs = [pl.BlockSpec((1, tq, w), lambda b, i: (b, i, L0_AQ)), full(L0_AK), full(L0_AV)]
    args = [proj, proj, proj]
    scratch = [pltpu.VMEM((t_len, w), BF16), pltpu.VMEM((t_len, w), BF16)]
    if dec:
        p_len = ctx_k.shape[1]
        in_specs += [pl.BlockSpec((1, p_len, w), lambda b, i: (b, 0, 0))] * 2
        args += [ctx_k, ctx_v]
        scratch += [pltpu.VMEM((p_len, w), BF16), pltpu.VMEM((p_len, w), BF16)]
    in_specs.append(pl.BlockSpec((1, tq, w), lambda b, i: (b, i, L0_GATE_A)))
    args.append(proj)
    if dec:
        in_specs += [pl.BlockSpec((t_len, LANES), lambda b, i: (0, 0))] * 2
        args += [cos, sin]
    in_specs += [pl.BlockSpec((4, A_QK_DIM), lambda b, i: (0, 0)),
                 pl.BlockSpec((1, A_V_DIM), lambda b, i: (0, 0))]
    args += [diff_lambda, subln_g]
    return pl.pallas_call(
        functools.partial(_diff_attn_body, dec=dec, tq=tq),
        grid=(bsz, t_len // tq),
        in_specs=in_specs,
        out_specs=pl.BlockSpec((1, tq, w), lambda b, i: (b, i, 0)),
        out_shape=jax.ShapeDtypeStruct((bsz, t_len, w), BF16),
        scratch_shapes=scratch,
        compiler_params=_cparams("parallel", "arbitrary"),
        name="diff_attn",
    )(*args)


def _mla_attn_body(*refs, dec, tq):
    if dec:
        (qn_ref, qp_ref, kv_ref, kpe_ref, kvc_ref, kpec_ref, gate_ref, cos_ref, sin_ref,
         o_ref, kvb, kpeb, kvcb, kpecb) = refs
    else:
        qn_ref, qp_ref, kv_ref, kpe_ref, gate_ref, o_ref, kvb, kpeb = refs
    qi = pl.program_id(1)

    @pl.when(qi == 0)
    def _():
        kvb[...] = kv_ref[0].astype(BF16)
        kp = kpe_ref[0]
        if dec:
            kp = _rope(kp, cos_ref[...], sin_ref[...])
            kvcb[...] = kvc_ref[0].astype(BF16)
            kpecb[...] = kpec_ref[0].astype(BF16)
        kpeb[...] = kp.astype(BF16)

    if dec:
        row0 = pl.multiple_of(qi * tq, tq)
        cq, sq = cos_ref[pl.ds(row0, tq), :], sin_ref[pl.ds(row0, tq), :]
    scale = (B_NOPE + B_ROPE) ** -0.5
    first = lax.broadcasted_iota(jnp.int32, (1, LANES), 1) < B_ROPE
    for h in range(B_HEADS):
        hs = slice(h * LANES, (h + 1) * LANES)
        if h % 2 == 0:
            pair = qp_ref[0, :, (h // 2) * LANES:(h // 2 + 1) * LANES]
            if dec:
                pair = _rope(pair, cq, sq)
            pair = pair * scale
        qp = (jnp.where(first, pair, 0.0) if h % 2 == 0 else jnp.where(first, 0.0, pair)).astype(BF16)
        qn = (qn_ref[0, :, hs] * scale).astype(BF16)
        ks = slice(2 * h * LANES, (2 * h + 1) * LANES)
        vs = slice((2 * h + 1) * LANES, (2 * h + 2) * LANES)
        scores = [_dot_nt(qn, kvb[:, ks]) + _dot_nt(qp, kpeb[...])]
        vals = [kvb[:, vs]]
        if dec:
            scores.append(_dot_nt(qn, kvcb[:, ks]) + _dot_nt(qp, kpecb[...]))
            vals.append(kvcb[:, vs])
        ps, denom = _softmax_parts(scores)
        inv = 1.0 / denom
        o = functools.reduce(jnp.add, [_dot((p * inv).astype(BF16), vp) for p, vp in zip(ps, vals)])
        o_ref[0, :, hs] = (o * _silu(gate_ref[0, :, hs])).astype(BF16)


def _mla_attn(proj, q_b, kv_b, ctx_kv, ctx_kpe, cos, sin, *, tq):
    bsz, t_len, _ = proj.shape
    dec = ctx_kv is not None
    w = B_HEADS * LANES
    nkv = kv_b.shape[2]
    in_specs = [
        pl.BlockSpec((1, tq, w), lambda b, i: (b, i, 0)),
        pl.BlockSpec((1, tq, B_HEADS * B_ROPE), lambda b, i: (b, i, 2)),
        pl.BlockSpec((1, t_len, nkv), lambda b, i: (b, 0, 0)),
        pl.BlockSpec((1, t_len, LANES), lambda b, i: (b, 0, L0_KPE)),
    ]
    args = [q_b, q_b, kv_b, proj]
    scratch = [pltpu.VMEM((t_len, nkv), BF16), pltpu.VMEM((t_len, LANES), BF16)]
    if dec:
        p_len = ctx_kv.shape[1]
        in_specs += [pl.BlockSpec((1, p_len, nkv), lambda b, i: (b, 0, 0)),
                     pl.BlockSpec((1, p_len, LANES), lambda b, i: (b, 0, 0))]
        args += [ctx_kv, ctx_kpe]
        scratch += [pltpu.VMEM((p_len, nkv), BF16), pltpu.VMEM((p_len, LANES), BF16)]
    in_specs.append(pl.BlockSpec((1, tq, w), lambda b, i: (b, i, L0_GATE_B)))
    args.append(proj)
    if dec:
        in_specs += [pl.BlockSpec((t_len, LANES), lambda b, i: (0, 0))] * 2
        args += [cos, sin]
    return pl.pallas_call(
        functools.partial(_mla_attn_body, dec=dec, tq=tq),
        grid=(bsz, t_len // tq),
        in_specs=in_specs,
        out_specs=pl.BlockSpec((1, tq, w), lambda b, i: (b, i, 0)),
        out_shape=jax.ShapeDtypeStruct((bsz, t_len, w), BF16),
        scratch_shapes=scratch,
        compiler_params=_cparams("parallel", "arbitrary"),
        name="mla_attn",
    )(*args)


def _outproj_body(x_ref, ya_ref, yb_ref, w_ref, gate_ref, o_ref):
    half = ya_ref.shape[1]
    acc = _dot(ya_ref[...], w_ref[0:half, :]) + _dot(yb_ref[...], w_ref[half:, :])
    o_ref[...] = x_ref[...] + gate_ref[0] * acc


def _outproj(x, ya, yb, w, gate, *, rows_per_mod, tm, tn):
    m, d = x.shape
    k = w.shape[0]
    return pl.pallas_call(
        _outproj_body,
        grid=(m // tm, d // tn),
        in_specs=[
            pl.BlockSpec((tm, tn), lambda i, j: (i, j)),
            pl.BlockSpec((tm, k // 2), lambda i, j: (i, 0)),
            pl.BlockSpec((tm, k // 2), lambda i, j: (i, 0)),
            pl.BlockSpec((k, tn), lambda i, j: (0, j)),
            pl.BlockSpec((1, 1, tn), lambda i, j: ((i * tm) // rows_per_mod, 0, j)),
        ],
        out_specs=pl.BlockSpec((tm, tn), lambda i, j: (i, j)),
        out_shape=jax.ShapeDtypeStruct((m, d), F32),
        compiler_params=_cparams("parallel", "parallel"),
        name="outproj",
    )(x, ya, yb, w, gate)


def _relayout_body(main_ref, lora_ref, w_ref, o_ref, r_scr, q_scr):
    q = pl.program_id(2)

    def park(tile_of, prepare=None):
        for b in range(PER):
            if prepare is not None:
                prepare(b)
            for c in range(C_WIDTH // LANES):
                at = tile_of(b, c).T
                p0 = b * C_HEADS + 2 * c
                r_scr[pl.ds(p0 * PITCH, C_HEAD), :] = at[:C_HEAD]
                r_scr[pl.ds((p0 + 1) * PITCH, C_HEAD), :] = at[C_HEAD:]

    @pl.when(q < 3)
    def _():
        park(lambda b, c: main_ref[0, b, :, c * LANES:(c + 1) * LANES])

    @pl.when(q >= 3)
    def _():
        def expand(b):
            x = lora_ref[0, b]
            x = jnp.where(q < 5, jnp.tanh(x), x)
            q_scr[...] = _dot(x.astype(BF16), w_ref[0])
        park(lambda b, c: q_scr[:, c * LANES:(c + 1) * LANES], expand)

    for j in range(C_HEAD):
        m = r_scr[pl.ds(j, LANES, stride=PITCH), :]
        o_ref[0, 0, pl.ds(j, TT, stride=PITCH), :] = m.T
    zero = jnp.zeros((TT, LANES), F32)
    for j in range(C_HEAD, PITCH):
        o_ref[0, 0, pl.ds(j, TT, stride=PITCH), :] = zero


def _relayout(proj, lora_w, bsz, t_len):
    g = bsz // PER
    proj4 = proj.reshape(g, PER, t_len, L1_COLS)
    lora0 = 4 * C_WIDTH // LANES
    out = pl.pallas_call(
        _relayout_body,
        grid=(g, t_len // TT, N_LANE_ARRAYS),
        in_specs=[
            pl.BlockSpec((1, PER, TT, C_WIDTH), lambda gi, ti, q: (gi, 0, ti, jnp.minimum(q, 2))),
            pl.BlockSpec((1, PER, TT, LANES), lambda gi, ti, q: (gi, 0, ti, lora0 + jnp.clip(q - 3, 0, 3))),
            pl.BlockSpec((1, LORA_PAD, C_WIDTH), lambda gi, ti, q: (jnp.clip(q - 3, 0, 3), 0, 0)),
        ],
        out_specs=pl.BlockSpec((1, 1, TT * PITCH, LANES), lambda gi, ti, q: (q, gi, ti, 0)),
        out_shape=jax.ShapeDtypeStruct((N_LANE_ARRAYS, g, t_len * PITCH, LANES), F32),
        scratch_shapes=[pltpu.VMEM((LANES * PITCH, LANES), F32), pltpu.VMEM((TT, C_WIDTH), F32)],
        compiler_params=_cparams("parallel", "parallel", "arbitrary"),
        name="lane_relayout",
    )(proj4, proj4, lora_w)
    return out.reshape(N_LANE_ARRAYS, g, t_len, PITCH, LANES)


def _lane_tile(p):
    return jnp.tile(p.reshape(C_HEADS, C_HEAD).T, (1, PER))


def _scan_body(*refs, tc, reverse, has_prev):
    (r_ref, k_ref, v_ref, wl_ref, al_ref, w0_ref, a0_ref, kkp_ref, kap_ref, rkp_ref, lnw_ref, lnb_ref,
     s0_ref) = refs[:13]
    rest = refs[13:]
    if has_prev:
        prev_ref, rest = rest[0], rest[1:]
    o_ref, sfin_ref, s_scr, w_scr, kka_scr, kz_scr, nkk_scr, y_scr = rest
    c = pl.program_id(1)

    @pl.when(c == 0)
    def _():
        s_scr[...] = s0_ref[0]

    used = lambda ref: ref[0, 0, :, :C_HEAD, :]
    k = used(k_ref)
    u = -(w0_ref[...] + used(wl_ref))
    softplus = jnp.maximum(u, 0.0) + jnp.log(1.0 + jnp.exp(-jnp.abs(u)))
    w_scr[...] = jnp.exp(-jnp.exp(-softplus - 0.5))
    a = 1.0 / (1.0 + jnp.exp(-(a0_ref[...] + used(al_ref))))
    kk = k * kkp_ref[...]
    kk = kk * lax.rsqrt(jnp.maximum(jnp.sum(kk * kk, axis=1, keepdims=True), 1e-12))
    kz = k * (1.0 + (a - 1.0) * kap_ref[...])
    kz_scr[...] = kz
    kka_scr[...] = kk * a
    nkk_scr[...] = -kk
    bonus = jnp.sum(used(r_ref) * kz * rkp_ref[...], axis=1, keepdims=True) * used(v_ref)

    first = tc - 1 if reverse else 0
    sa0 = jnp.zeros((C_HEAD, LANES), F32)
    for j in range(C_HEAD):
        sa0 = sa0 + s_scr[j] * nkk_scr[first, j:j + 1, :]

    def step(i, sa):
        t = (tc - 1 - i) if reverse else i
        t_next = jnp.maximum(t - 1, 0) if reverse else jnp.minimum(t + 1, tc - 1)
        v = v_ref[0, 0, t, :C_HEAD, :]
        y = jnp.zeros((C_HEAD, LANES), F32)
        sa_next = jnp.zeros((C_HEAD, LANES), F32)
        for j in range(C_HEAD):
            s = s_scr[j] * w_scr[t, j:j + 1, :] + sa * kka_scr[t, j:j + 1, :] + v * kz_scr[t, j:j + 1, :]
            s_scr[j] = s
            y = y + s * r_ref[0, 0, t, j:j + 1, :]
            sa_next = sa_next + s * nkk_scr[t_next, j:j + 1, :]
        y_scr[t] = y
        return sa_next

    lax.fori_loop(0, tc, step, sa0)

    y = y_scr[...]
    dev = y - jnp.mean(y, axis=1, keepdims=True)
    var = jnp.mean(dev * dev, axis=1, keepdims=True)
    out = dev * lax.rsqrt(var + GN_EPS) * lnw_ref[...] + lnb_ref[...] + bonus
    if has_prev:
        out = out + prev_ref[0]
    o_ref[0] = out

    @pl.when(c == pl.num_programs(1) - 1)
    def _():
        sfin_ref[0] = s_scr[...]


def _scan(z7, z, params, s0, prev, *, reverse, tc):
    _, g, t_len = z7.shape[:3]
    n_c = t_len // tc
    tix = (lambda ci: n_c - 1 - ci) if reverse else (lambda ci: ci)
    qblk = lambda q: pl.BlockSpec((1, 1, tc, PITCH, LANES), lambda gi, ci: (q, gi, tix(ci), 0, 0))
    tile = pl.BlockSpec((C_HEAD, LANES), lambda gi, ci: (0, 0))
    st = pl.BlockSpec((1, C_HEAD, C_HEAD, LANES), lambda gi, ci: (gi, 0, 0, 0))
    oblk = pl.BlockSpec((1, tc, C_HEAD, LANES), lambda gi, ci: (gi, tix(ci), 0, 0))
    in_specs = [qblk(0), qblk(1), qblk(2), qblk(3 + z), qblk(5 + z)] + [tile] * len(params) + [st]
    args = [z7] * 5 + list(params) + [s0]
    if prev is not None:
        in_specs.append(oblk)
        args.append(prev)
    chunk = pltpu.VMEM((tc, C_HEAD, LANES), F32)
    return pl.pallas_call(
        functools.partial(_scan_body, tc=tc, reverse=reverse, has_prev=prev is not None),
        grid=(g, n_c),
        in_specs=in_specs,
        out_specs=[oblk, st],
        out_shape=[jax.ShapeDtypeStruct((g, t_len, C_HEAD, LANES), F32), jax.ShapeDtypeStruct(s0.shape, F32)],
        scratch_shapes=[pltpu.VMEM((C_HEAD, C_HEAD, LANES), F32)] + [chunk] * 5,
        compiler_params=_cparams("parallel", "arbitrary"),
        name="wkv_scan",
    )(*args)


def _from_lanes(y, bsz, t_len):
    per = LANES // C_HEADS
    y = y.reshape(bsz // per, t_len, C_HEAD, per, C_HEADS)
    return y.transpose(0, 3, 1, 4, 2).reshape(bsz * t_len, C_WIDTH)


def _state_to_lanes(s):
    per = LANES // C_HEADS
    bsz = s.shape[0]
    s = s.reshape(bsz // per, per, C_HEADS, C_HEAD, C_HEAD)
    return s.transpose(0, 4, 3, 1, 2).reshape(bsz // per, C_HEAD, C_HEAD, LANES)


def _state_from_lanes(s):
    per = LANES // C_HEADS
    g = s.shape[0]
    s = s.reshape(g, C_HEAD, C_HEAD, per, C_HEADS)
    return s.transpose(0, 3, 4, 2, 1).reshape(g * per, C_HEADS, C_HEAD, C_HEAD)


def _rwkv_out_body(x_ref, y_ref, gate_ref, w_ref, mg_ref, fg_ref, o_ref):
    yg = (y_ref[...] * _silu(gate_ref[...])).astype(BF16)
    xn = x_ref[...] + mg_ref[0] * _dot(yg, w_ref[...])
    o_ref[...] = _rms(xn, fg_ref[...])


def _rwkv_out(x, y, proj, w, gate, final_g, *, rows_per_mod, tm):
    m, d = x.shape
    row = lambda i: (i, 0)
    const = lambda i: (0, 0)
    tile = pl.BlockSpec((tm, d), row)
    return pl.pallas_call(
        _rwkv_out_body,
        grid=(m // tm,),
        in_specs=[
            tile, tile,
            pl.BlockSpec((tm, C_WIDTH), lambda i: (i, 3)),
            pl.BlockSpec((C_WIDTH, d), const, pipeline_mode=pl.Buffered(1)),
            pl.BlockSpec((1, 1, d), lambda i: ((i * tm) // rows_per_mod, 0, 0)),
            pl.BlockSpec((1, d), const),
        ],
        out_specs=tile,
        out_shape=jax.ShapeDtypeStruct((m, d), F32),
        compiler_params=_cparams("parallel"),
        name="rwkv_out",
    )(x, y, proj, w, gate, final_g)


def _rope_tables(t_len):
    pos = jnp.arange(t_len)
    half = 16
    freqs = ROPE_BASE ** (-jnp.arange(half, dtype=F32) / half)
    ang_r = (pos // GRID_W).astype(F32)[:, None] * freqs
    ang_c = (pos % GRID_W).astype(F32)[:, None] * freqs
    cos = jnp.concatenate([jnp.cos(ang_r)] * 2 + [jnp.cos(ang_c)] * 2, axis=-1)
    sin = jnp.concatenate([-jnp.sin(ang_r), jnp.sin(ang_r), -jnp.sin(ang_c), jnp.sin(ang_c)], axis=-1)
    return jnp.tile(cos, (1, 2)), jnp.tile(sin, (1, 2))


def _l0_w_in_layout(w):
    d = w.shape[0]
    aq, ak, av, cq, ckv, kpe, gate = jnp.split(w, [1024, 2048, 3072, 3584, 3840, 3904], axis=1)
    pad = jnp.zeros((d, L0_COLS - 6016), w.dtype)
    return jnp.concatenate([gate, aq, ak, av, cq, ckv, kpe, kpe, pad], axis=1).astype(BF16)


def _l1_cols_layout(w):
    main, wd, ad = jnp.split(w, [4 * C_WIDTH, 4 * C_WIDTH + 2 * C_LORA], axis=1)
    pad = jnp.zeros((w.shape[0], LORA_PAD - C_LORA), w.dtype)
    parts = [main]
    for seg in (wd, ad):
        for z in range(2):
            parts += [seg[:, z * C_LORA:(z + 1) * C_LORA], pad]
    return jnp.concatenate(parts, axis=1)


def _layer0(x, mods, rows_per_mod, seq_len, weights, ctx, tables):
    w_in, w_out, diff_lambda, subln_g, q_norm_g, w_uq, kv_norm_g, w_ukv = weights
    shift, scale, gate, norm_g = mods
    m = x.shape[0]
    bsz = m // seq_len
    proj = _inproj(x, shift, scale, norm_g, w_in, None, rows_per_mod=rows_per_mod, seq_len=seq_len, tm=512, tn=1024)
    q_b, ckv_n, kv_b = _mla_prep(proj, q_norm_g, kv_norm_g, w_uq, w_ukv, tm=512)
    proj3 = proj.reshape(bsz, seq_len, L0_COLS)
    q_b3 = q_b.reshape(bsz, seq_len, -1)
    kv_b3 = kv_b.reshape(bsz, seq_len, -1)
    if ctx is None:
        ya = _diff_attn(proj3, None, None, None, None, diff_lambda, subln_g, tq=256)
        yb = _mla_attn(proj3, q_b3, kv_b3, None, None, None, None, tq=256)
    else:
        k_ctx, v_ctx, ckv_ctx, kpe_ctx = ctx
        p_len = k_ctx.shape[1]
        cos, sin = tables
        kv_ctx = _matmul(ckv_ctx.reshape(bsz * p_len, B_KV_LORA), w_ukv, tm=512).reshape(bsz, p_len, -1)
        ya = _diff_attn(proj3, k_ctx.reshape(bsz, p_len, -1), v_ctx.reshape(bsz, p_len, -1), cos, sin,
                        diff_lambda, subln_g, tq=256)
        yb = _mla_attn(proj3, q_b3, kv_b3, kv_ctx, jnp.concatenate([kpe_ctx, kpe_ctx], axis=-1), cos, sin, tq=256)
    x_new = _outproj(x, ya.reshape(m, -1), yb.reshape(m, -1), w_out, gate, rows_per_mod=rows_per_mod, tm=512, tn=1024)
    return x_new, proj, ckv_n


def _layer1(x, mods, rows_per_mod, seq_len, weights, states, final_g):
    w_in, w_out, mu, lora_w, dir_params, shared_params = weights
    shift, scale, gate, norm_g = mods
    m = x.shape[0]
    bsz = m // seq_len
    proj = _inproj(x, shift, scale, norm_g, w_in, mu, rows_per_mod=rows_per_mod, seq_len=seq_len, tm=1024, tn=512)
    z7 = _relayout(proj, lora_w, bsz, seq_len)
    y, finals = None, []
    for z in range(2):
        y, s_fin = _scan(z7, z, dir_params[z] + shared_params, _state_to_lanes(states[z]), y,
                         reverse=(z == 1), tc=32)
        finals.append(_state_from_lanes(s_fin))
    out = _rwkv_out(x, _from_lanes(y, bsz, seq_len), proj, w_out, gate, final_g, rows_per_mod=rows_per_mod, tm=256)
    return out, finals


def kernel(x_prompt, x_sample, cache_l0_a_k, cache_l0_a_v, cache_l0_mla_ckv, cache_l0_mla_kpe, state_l1_fwd, state_l1_bwd, c, c_ctx, mod_w, mod_b, norm_g, final_norm_g, l0_w_in, l0_w_out, l0_diff_lambda, l0_subln_g, l0_q_norm_g, l0_w_uq, l0_kv_norm_g, l0_w_ukv, l1_w_in, l1_w_out, l1_mu, l1_w0, l1_w2, l1_a0, l1_a2, l1_k_k, l1_k_a, l1_r_k, l1_ln_w, l1_ln_b):
    d = D_MODEL
    bp, tp, _ = x_prompt.shape
    bs, ts, _ = x_sample.shape

    cond = jnp.concatenate([c_ctx[None, :], c, jnp.zeros((MOD_ROWS - 1 - bs, d), F32)], axis=0)
    mods = _adaln(cond, mod_w, mod_b)

    def mod_rows(layer, lo, hi):
        rows = mods[layer, lo:hi]
        shift, scale, gate = (rows[:, i * d:(i + 1) * d].reshape(hi - lo, 1, d) for i in range(3))
        return shift, scale, gate, norm_g[layer].reshape(1, d)

    w_uq = l0_w_uq.reshape(B_Q_LORA, B_HEADS, B_NOPE + B_ROPE)
    w_uq = jnp.concatenate([w_uq[:, :, :B_NOPE].reshape(B_Q_LORA, -1), w_uq[:, :, B_NOPE:].reshape(B_Q_LORA, -1)],
                           axis=1).astype(BF16)
    l0_weights = (_l0_w_in_layout(l0_w_in), l0_w_out.astype(BF16), l0_diff_lambda, l0_subln_g.reshape(1, -1),
                  l0_q_norm_g.reshape(1, -1), w_uq, l0_kv_norm_g.reshape(1, -1), l0_w_ukv.astype(BF16))
    lora_pad = jnp.zeros((2, LORA_PAD - C_LORA, C_WIDTH), F32)
    lora_w = jnp.concatenate([jnp.concatenate([l1_w2, lora_pad], axis=1),
                              jnp.concatenate([l1_a2, lora_pad], axis=1)], axis=0).astype(BF16)
    dir_params = [[_lane_tile(l1_w0[z]), _lane_tile(l1_a0[z])] for z in range(2)]
    shared_params = [_lane_tile(p) for p in (l1_k_k, l1_k_a, l1_r_k, l1_ln_w, l1_ln_b)]
    l1_weights = (_l1_cols_layout(l1_w_in).astype(BF16), l1_w_out.astype(BF16), _l1_cols_layout(l1_mu),
                  lora_w, dir_params, shared_params)
    tables = _rope_tables(ts)
    final_g = final_norm_g.reshape(1, d)

    xp = x_prompt.reshape(bp * tp, d)
    xs = x_sample.reshape(bs * ts, d)
    ctx0 = (cache_l0_a_k, cache_l0_a_v, cache_l0_mla_ckv, cache_l0_mla_kpe)

    xp1, proj_p, ckv_p = _layer0(xp, mod_rows(0, 0, 1), bp * tp, tp, l0_weights, None, None)
    xs1, _, _ = _layer0(xs, mod_rows(0, 1, 1 + bs), ts, ts, l0_weights, ctx0, tables)

    zero_state = jnp.zeros((bp, C_HEADS, C_HEAD, C_HEAD), F32)
    y_prompt, finals = _layer1(xp1, mod_rows(1, 0, 1), bp * tp, tp, l1_weights, (zero_state, zero_state), final_g)
    y_sample, _ = _layer1(xs1, mod_rows(1, 1, 1 + bs), ts, ts, l1_weights, (state_l1_fwd, state_l1_bwd), final_g)

    new_a_k = proj_p[:, 3072:4096].reshape(bp, tp, A_HEADS, 2, A_QK_DIM)
    new_a_v = proj_p[:, 4096:5120].reshape(bp, tp, A_HEADS, A_V_DIM)
    new_ckv = ckv_p.reshape(bp, tp, B_KV_LORA)
    new_kpe = proj_p[:, 5888:5888 + B_ROPE].reshape(bp, tp, B_ROPE)
    return (y_prompt.reshape(bp, tp, d), y_sample.reshape(bs, ts, d), new_a_k, new_a_v, new_ckv, new_kpe,
            finals[0], finals[1])
```

```python
import functools
import math

import jax
import jax.numpy as jnp
from jax import lax
from jax.experimental import pallas as pl
from jax.experimental.pallas import tpu as pltpu

F32 = jnp.float32
BF16 = jnp.bfloat16

D_MODEL = 2048
GRID_W = 64
ROPE_BASE = 10000.0
NORM_EPS = 1e-6
GN_EPS = 64e-5

A_HEADS = 8
A_QK_DIM = 64
A_V_DIM = 128
B_HEADS = 8
B_NOPE = 128
B_ROPE = 64
B_V = 128
B_Q_LORA = 512
B_KV_LORA = 256
AB_WIDTH = A_HEADS * A_V_DIM + B_HEADS * B_V
C_HEAD = 64
C_HEADS = D_MODEL // C_HEAD
C_WIDTH = C_HEADS * C_HEAD
C_LORA = 96
LAM_INIT_L0 = 0.8 - 0.6 * math.exp(-0.3 * 0)

LANES = 128
SUBLANES = 8
LORA_PAD = 128
MOD_ROWS = 8
VMEM_LIMIT = 56 * 1024 * 1024

L0_COLS = 6144
L0_GATE_A, L0_GATE_B, L0_AQ, L0_AK, L0_AV = 0, 1, 2, 3, 4
L0_CQ = 10
L0_CKV = 22
L0_KPE = 46
L1_COLS = 4 * C_WIDTH + 4 * LORA_PAD
PER = LANES // C_HEADS
PITCH = 72
TT = LANES
N_LANE_ARRAYS = 7


def _cparams(*sem):
    return pltpu.CompilerParams(dimension_semantics=sem, vmem_limit_bytes=VMEM_LIMIT)


def _silu(x):
    return x / (1.0 + jnp.exp(-x))


def _dot(a, b):
    return jnp.dot(a, b, preferred_element_type=F32)


def _dot_nt(a, b):
    return lax.dot_general(a, b, (((1,), (1,)), ((), ())), preferred_element_type=F32)


def _rope(x, cos, sin):
    lane = lax.broadcasted_iota(jnp.int32, x.shape, 1)
    low = (lane % 32) < 16
    rot = jnp.where(low, pltpu.roll(x, LANES - 16, 1), pltpu.roll(x, 16, 1))
    return x * cos + rot * sin


def _adaln_body(c_ref, w_ref, b_ref, o_ref):
    s = _silu(c_ref[...]).astype(BF16)
    o_ref[0] = _dot(s, w_ref[0].astype(BF16)) + b_ref[0]


def _adaln(cond, mod_w, mod_b):
    depth, d, n = mod_w.shape
    tn = 1024
    return pl.pallas_call(
        _adaln_body,
        grid=(depth, n // tn),
        in_specs=[
            pl.BlockSpec((MOD_ROWS, d), lambda l, j: (0, 0)),
            pl.BlockSpec((1, d, tn), lambda l, j: (l, 0, j)),
            pl.BlockSpec((1, 1, tn), lambda l, j: (l, 0, j)),
        ],
        out_specs=pl.BlockSpec((1, MOD_ROWS, tn), lambda l, j: (l, 0, j)),
        out_shape=jax.ShapeDtypeStruct((depth, MOD_ROWS, n), F32),
        compiler_params=_cparams("parallel", "parallel"),
        name="adaln",
    )(cond, mod_w, mod_b.reshape(depth, 1, n))


def _inproj_body(x_ref, sh_ref, sc_ref, g_ref, w_ref, *rest, seq_len):
    if seq_len:
        mu_ref, o_ref, h_ref = rest
    else:
        o_ref, h_ref = rest

    @pl.when(pl.program_id(1) == 0)
    def _():
        x = x_ref[...]
        y = x * lax.rsqrt(jnp.mean(x * x, axis=-1, keepdims=True) + NORM_EPS) * g_ref[...]
        h_ref[...] = (y * (1.0 + sc_ref[0]) + sh_ref[0]).astype(BF16)

    p = _dot(h_ref[...], w_ref[...])
    if seq_len:
        tm = p.shape[0]
        row = lax.broadcasted_iota(jnp.int32, (tm, 1), 0) % seq_len
        prev = jnp.where(row == 0, 0.0, pltpu.roll(p, 1, 0))
        nxt = jnp.where(row == seq_len - 1, 0.0, pltpu.roll(p, tm - 1, 0))
        p = p + mu_ref[0:1, :] * (prev - p) + mu_ref[1:2, :] * (nxt - p)
    o_ref[...] = p


def _inproj(x, shift, scale, g, w, mu, *, rows_per_mod, seq_len, tm, tn):
    m, d = x.shape
    n = w.shape[1]
    mod_map = lambda i, j: ((i * tm) // rows_per_mod, 0, 0)
    in_specs = [
        pl.BlockSpec((tm, d), lambda i, j: (i, 0)),
        pl.BlockSpec((1, 1, d), mod_map),
        pl.BlockSpec((1, 1, d), mod_map),
        pl.BlockSpec((1, d), lambda i, j: (0, 0)),
        pl.BlockSpec((d, tn), lambda i, j: (0, j)),
    ]
    args = [x, shift, scale, g, w]
    if mu is not None:
        in_specs.append(pl.BlockSpec((2, tn), lambda i, j: (0, j)))
        args.append(mu)
    return pl.pallas_call(
        functools.partial(_inproj_body, seq_len=seq_len if mu is not None else 0),
        grid=(m // tm, n // tn),
        in_specs=in_specs,
        out_specs=pl.BlockSpec((tm, tn), lambda i, j: (i, j)),
        out_shape=jax.ShapeDtypeStruct((m, n), F32),
        scratch_shapes=[pltpu.VMEM((tm, d), BF16)],
        compiler_params=_cparams("parallel", "arbitrary"),
        name="inproj",
    )(*args)


def _rms(x, g):
    return x * lax.rsqrt(jnp.mean(x * x, axis=-1, keepdims=True) + NORM_EPS) * g


def _mla_prep_body(cq_ref, ckv_ref, qg_ref, kg_ref, wuq_ref, wukv_ref, qb_ref, ckvn_ref, kvb_ref):
    qb_ref[...] = _dot(_rms(cq_ref[...], qg_ref[...]).astype(BF16), wuq_ref[...])
    ckv = _rms(ckv_ref[...], kg_ref[...])
    ckvn_ref[...] = ckv
    kvb_ref[...] = _dot(ckv.astype(BF16), wukv_ref[...])


def _mla_prep(proj, q_norm_g, kv_norm_g, w_uq, w_ukv, *, tm):
    m = proj.shape[0]
    nq, nkv = w_uq.shape[1], w_ukv.shape[1]
    const = lambda i: (0, 0)
    return pl.pallas_call(
        _mla_prep_body,
        grid=(m // tm,),
        in_specs=[
            pl.BlockSpec((tm, B_Q_LORA), lambda i: (i, L0_CQ)),
            pl.BlockSpec((tm, B_KV_LORA), lambda i: (i, L0_CKV)),
            pl.BlockSpec((1, B_Q_LORA), const),
            pl.BlockSpec((1, B_KV_LORA), const),
            pl.BlockSpec((B_Q_LORA, nq), const),
            pl.BlockSpec((B_KV_LORA, nkv), const),
        ],
        out_specs=[
            pl.BlockSpec((tm, nq), lambda i: (i, 0)),
            pl.BlockSpec((tm, B_KV_LORA), lambda i: (i, 0)),
            pl.BlockSpec((tm, nkv), lambda i: (i, 0)),
        ],
        out_shape=[
            jax.ShapeDtypeStruct((m, nq), F32),
            jax.ShapeDtypeStruct((m, B_KV_LORA), F32),
            jax.ShapeDtypeStruct((m, nkv), F32),
        ],
        compiler_params=_cparams("parallel"),
        name="mla_prep",
    )(proj, proj, q_norm_g, kv_norm_g, w_uq, w_ukv)


def _matmul_body(x_ref, w_ref, o_ref):
    o_ref[...] = _dot(x_ref[...].astype(BF16), w_ref[...])


def _matmul(x, w, *, tm):
    m, k = x.shape
    n = w.shape[1]
    return pl.pallas_call(
        _matmul_body,
        grid=(m // tm,),
        in_specs=[pl.BlockSpec((tm, k), lambda i: (i, 0)), pl.BlockSpec((k, n), lambda i: (0, 0))],
        out_specs=pl.BlockSpec((tm, n), lambda i: (i, 0)),
        out_shape=jax.ShapeDtypeStruct((m, n), F32),
        compiler_params=_cparams("parallel"),
        name="matmul",
    )(x, w)


def _softmax_parts(scores):
    m = functools.reduce(jnp.maximum, [jnp.max(s, axis=-1, keepdims=True) for s in scores])
    ps = [jnp.exp(s - m) for s in scores]
    denom = functools.reduce(jnp.add, [jnp.sum(p, axis=-1, keepdims=True) for p in ps])
    return ps, denom


def _diff_attn_body(*refs, dec, tq):
    if dec:
        (q_ref, k_ref, v_ref, kc_ref, vc_ref, gate_ref, cos_ref, sin_ref, lam_ref, sg_ref,
         o_ref, kb, vb, kcb, vcb) = refs
    else:
        q_ref, k_ref, v_ref, gate_ref, lam_ref, sg_ref, o_ref, kb, vb = refs
    qi = pl.program_id(1)

    @pl.when(qi == 0)
    def _():
        for h in range(A_HEADS):
            hs = slice(h * LANES, (h + 1) * LANES)
            kt = k_ref[0, :, hs]
            if dec:
                kt = _rope(kt, cos_ref[...], sin_ref[...])
                kcb[:, hs] = kc_ref[0, :, hs].astype(BF16)
                vcb[:, hs] = vc_ref[0, :, hs].astype(BF16)
            kb[:, hs] = kt.astype(BF16)
            vb[:, hs] = v_ref[0, :, hs].astype(BF16)

    lp = lam_ref[...]
    lam = (jnp.exp(jnp.sum(lp[0:1] * lp[1:2], keepdims=True))
           - jnp.exp(jnp.sum(lp[2:3] * lp[3:4], keepdims=True)) + LAM_INIT_L0)
    if dec:
        row0 = pl.multiple_of(qi * tq, tq)
        cq, sq = cos_ref[pl.ds(row0, tq), :], sin_ref[pl.ds(row0, tq), :]
    first = lax.broadcasted_iota(jnp.int32, (1, LANES), 1) < A_QK_DIM
    for h in range(A_HEADS):
        hs = slice(h * LANES, (h + 1) * LANES)
        qh = q_ref[0, :, hs]
        if dec:
            qh = _rope(qh, cq, sq)
        qh = qh * (A_QK_DIM ** -0.5)
        q1 = jnp.where(first, qh, 0.0).astype(BF16)
        q2 = jnp.where(first, 0.0, qh).astype(BF16)
        keys = [kb[:, hs]] + ([kcb[:, hs]] if dec else [])
        vals = [vb[:, hs]] + ([vcb[:, hs]] if dec else [])
        p1, l1 = _softmax_parts([_dot_nt(q1, kp) for kp in keys])
        p2, l2 = _softmax_parts([_dot_nt(q2, kp) for kp in keys])
        a1 = 1.0 / l1
        a2 = lam / l2
        o = functools.reduce(jnp.add, [_dot((x1 * a1 - x2 * a2).astype(BF16), vp)
                                       for x1, x2, vp in zip(p1, p2, vals)])
        o = _rms(o, sg_ref[...]) * (1.0 - LAM_INIT_L0)
        o_ref[0, :, hs] = (o * _silu(gate_ref[0, :, hs])).astype(BF16)


def _diff_attn(proj, ctx_k, ctx_v, cos, sin, diff_lambda, subln_g, *, tq):
    bsz, t_len, _ = proj.shape
    dec = ctx_k is not None
    w = A_HEADS * LANES
    full = lambda col: pl.BlockSpec((1, t_len, w), lambda b, i: (b, 0, col))
    in_specs = [pl.BlockSpec((1, tq, w), lambda b, i: (b, i, L0_AQ)), full(L0_AK), full(L0_AV)]
    args = [proj, proj, proj]
    scratch = [pltpu.VMEM((t_len, w), BF16), pltpu.VMEM((t_len, w), BF16)]
    if dec:
        p_len = ctx_k.shape[1]
        in_specs += [pl.BlockSpec((1, p_len, w), lambda b, i: (b, 0, 0))] * 2
        args += [ctx_k, ctx_v]
        scratch += [pltpu.VMEM((p_len, w), BF16), pltpu.VMEM((p_len, w), BF16)]
    in_specs.append(pl.BlockSpec((1, tq, w), lambda b, i: (b, i, L0_GATE_A)))
    args.append(proj)
    if dec:
        in_specs += [pl.BlockSpec((t_len, LANES), lambda b, i: (0, 0))] * 2
        args += [cos, sin]
    in_specs += [pl.BlockSpec((4, A_QK_DIM), lambda b, i: (0, 0)),
                 pl.BlockSpec((1, A_V_DIM), lambda b, i: (0, 0))]
    args += [diff_lambda, subln_g]
    return pl.pallas_call(
        functools.partial(_diff_attn_body, dec=dec, tq=tq),
        grid=(bsz, t_len // tq),
        in_specs=in_specs,
        out_specs=pl.BlockSpec((1, tq, w), lambda b, i: (b, i, 0)),
        out_shape=jax.ShapeDtypeStruct((bsz, t_len, w), BF16),
        scratch_shapes=scratch,
        compiler_params=_cparams("parallel", "arbitrary"),
        name="diff_attn",
    )(*args)


def _mla_attn_body(*refs, dec, tq):
    if dec:
        (qn_ref, qp_ref, kv_ref, kpe_ref, kvc_ref, kpec_ref, gate_ref, cos_ref, sin_ref,
         o_ref, kvb, kpeb, kvcb, kpecb) = refs
    else:
        qn_ref, qp_ref, kv_ref, kpe_ref, gate_ref, o_ref, kvb, kpeb = refs
    qi = pl.program_id(1)

    @pl.when(qi == 0)
    def _():
        kvb[...] = kv_ref[0].astype(BF16)
        kp = kpe_ref[0]
        if dec:
            kp = _rope(kp, cos_ref[...], sin_ref[...])
            kvcb[...] = kvc_ref[0].astype(BF16)
            kpecb[...] = kpec_ref[0].astype(BF16)
        kpeb[...] = kp.astype(BF16)

    if dec:
        row0 = pl.multiple_of(qi * tq, tq)
        cq, sq = cos_ref[pl.ds(row0, tq), :], sin_ref[pl.ds(row0, tq), :]
    scale = (B_NOPE + B_ROPE) ** -0.5
    first = lax.broadcasted_iota(jnp.int32, (1, LANES), 1) < B_ROPE
    for h in range(B_HEADS):
        hs = slice(h * LANES, (h + 1) * LANES)
        if h % 2 == 0:
            pair = qp_ref[0, :, (h // 2) * LANES:(h // 2 + 1) * LANES]
            if dec:
                pair = _rope(pair, cq, sq)
            pair = pair * scale
        qp = (jnp.where(first, pair, 0.0) if h % 2 == 0 else jnp.where(first, 0.0, pair)).astype(BF16)
        qn = (qn_ref[0, :, hs] * scale).astype(BF16)
        ks = slice(2 * h * LANES, (2 * h + 1) * LANES)
        vs = slice((2 * h + 1) * LANES, (2 * h + 2) * LANES)
        scores = [_dot_nt(qn, kvb[:, ks]) + _dot_nt(qp, kpeb[...])]
        vals = [kvb[:, vs]]
        if dec:
            scores.append(_dot_nt(qn, kvcb[:, ks]) + _dot_nt(qp, kpecb[...]))
            vals.append(kvcb[:, vs])
        ps, denom = _softmax_parts(scores)
        inv = 1.0 / denom
        o = functools.reduce(jnp.add, [_dot((p * inv).astype(BF16), vp) for p, vp in zip(ps, vals)])
        o_ref[0, :, hs] = (o * _silu(gate_ref[0, :, hs])).astype(BF16)


def _mla_attn(proj, q_b, kv_b, ctx_kv, ctx_kpe, cos, sin, *, tq):
    bsz, t_len, _ = proj.shape
    dec = ctx_kv is not None
    w = B_HEADS * LANES
    nkv = kv_b.shape[2]
    in_specs = [
        pl.BlockSpec((1, tq, w), lambda b, i: (b, i, 0)),
        pl.BlockSpec((1, tq, B_HEADS * B_ROPE), lambda b, i: (b, i, 2)),
        pl.BlockSpec((1, t_len, nkv), lambda b, i: (b, 0, 0)),
        pl.BlockSpec((1, t_len, LANES), lambda b, i: (b, 0, L0_KPE)),
    ]
    args = [q_b, q_b, kv_b, proj]
    scratch = [pltpu.VMEM((t_len, nkv), BF16), pltpu.VMEM((t_len, LANES), BF16)]
    if dec:
        p_len = ctx_kv.shape[1]
        in_specs += [pl.BlockSpec((1, p_len, nkv), lambda b, i: (b, 0, 0)),
                     pl.BlockSpec((1, p_len, LANES), lambda b, i: (b, 0, 0))]
        args += [ctx_kv, ctx_kpe]
        scratch += [pltpu.VMEM((p_len, nkv), BF16), pltpu.VMEM((p_len, LANES), BF16)]
    in_specs.append(pl.BlockSpec((1, tq, w), lambda b, i: (b, i, L0_GATE_B)))
    args.append(proj)
    if dec:
        in_specs += [pl.BlockSpec((t_len, LANES), lambda b, i: (0, 0))] * 2
        args += [cos, sin]
    return pl.pallas_call(
        functools.partial(_mla_attn_body, dec=dec, tq=tq),
        grid=(bsz, t_len // tq),
        in_specs=in_specs,
        out_specs=pl.BlockSpec((1, tq, w), lambda b, i: (b, i, 0)),
        out_shape=jax.ShapeDtypeStruct((bsz, t_len, w), BF16),
        scratch_shapes=scratch,
        compiler_params=_cparams("parallel", "arbitrary"),
        name="mla_attn",
    )(*args)


def _outproj_body(x_ref, ya_ref, yb_ref, w_ref, gate_ref, o_ref):
    half = ya_ref.shape[1]
    acc = _dot(ya_ref[...], w_ref[0:half, :]) + _dot(yb_ref[...], w_ref[half:, :])
    o_ref[...] = x_ref[...] + gate_ref[0] * acc


def _outproj(x, ya, yb, w, gate, *, rows_per_mod, tm, tn):
    m, d = x.shape
    k = w.shape[0]
    return pl.pallas_call(
        _outproj_body,
        grid=(m // tm, d // tn),
        in_specs=[
            pl.BlockSpec((tm, tn), lambda i, j: (i, j)),
            pl.BlockSpec((tm, k // 2), lambda i, j: (i, 0)),
            pl.BlockSpec((tm, k // 2), lambda i, j: (i, 0)),
            pl.BlockSpec((k, tn), lambda i, j: (0, j)),
            pl.BlockSpec((1, 1, tn), lambda i, j: ((i * tm) // rows_per_mod, 0, j)),
        ],
        out_specs=pl.BlockSpec((tm, tn), lambda i, j: (i, j)),
        out_shape=jax.ShapeDtypeStruct((m, d), F32),
        compiler_params=_cparams("parallel", "parallel"),
        name="outproj",
    )(x, ya, yb, w, gate)


def _relayout_body(main_ref, lora_ref, w_ref, bias_ref, o_ref, r_scr, q_scr):
    q = pl.program_id(2)

    def park(tile_of, prepare=None):
        for b in range(PER):
            if prepare is not None:
                prepare(b)
            for c in range(C_WIDTH // LANES):
                at = tile_of(b, c).T
                p0 = b * C_HEADS + 2 * c
                r_scr[pl.ds(p0 * PITCH, C_HEAD), :] = at[:C_HEAD]
                r_scr[pl.ds((p0 + 1) * PITCH, C_HEAD), :] = at[C_HEAD:]

    @pl.when(q < 3)
    def _():
        park(lambda b, c: main_ref[0, b, :, c * LANES:(c + 1) * LANES])

    expanded = lambda b, c: q_scr[:, c * LANES:(c + 1) * LANES]

    @pl.when((q == 3) | (q == 4))
    def _():
        def decay(b):
            u = -(bias_ref[0] + _dot(jnp.tanh(lora_ref[0, b]).astype(BF16), w_ref[0]))
            softplus = jnp.maximum(u, 0.0) + jnp.log(1.0 + jnp.exp(-jnp.abs(u)))
            q_scr[...] = jnp.exp(-jnp.exp(-softplus - 0.5))
        park(expanded, decay)

    @pl.when(q >= 5)
    def _():
        def iclr(b):
            q_scr[...] = 1.0 / (1.0 + jnp.exp(-(bias_ref[0] + _dot(lora_ref[0, b].astype(BF16), w_ref[0]))))
        park(expanded, iclr)

    for j in range(C_HEAD):
        m = r_scr[pl.ds(j, LANES, stride=PITCH), :]
        o_ref[0, 0, pl.ds(j, TT, stride=PITCH), :] = m.T
    zero = jnp.zeros((TT, LANES), F32)
    for j in range(C_HEAD, PITCH):
        o_ref[0, 0, pl.ds(j, TT, stride=PITCH), :] = zero


def _relayout(proj, lora_w, lora_b, bsz, t_len):
    g = bsz // PER
    proj4 = proj.reshape(g, PER, t_len, L1_COLS)
    lora0 = 4 * C_WIDTH // LANES
    lora_ix = lambda gi, ti, q: (jnp.clip(q - 3, 0, 3), 0, 0)
    out = pl.pallas_call(
        _relayout_body,
        grid=(g, t_len // TT, N_LANE_ARRAYS),
        in_specs=[
            pl.BlockSpec((1, PER, TT, C_WIDTH), lambda gi, ti, q: (gi, 0, ti, jnp.minimum(q, 2))),
            pl.BlockSpec((1, PER, TT, LANES), lambda gi, ti, q: (gi, 0, ti, lora0 + jnp.clip(q - 3, 0, 3))),
            pl.BlockSpec((1, LORA_PAD, C_WIDTH), lora_ix),
            pl.BlockSpec((1, 1, C_WIDTH), lora_ix),
        ],
        out_specs=pl.BlockSpec((1, 1, TT * PITCH, LANES), lambda gi, ti, q: (q, gi, ti, 0)),
        out_shape=jax.ShapeDtypeStruct((N_LANE_ARRAYS, g, t_len * PITCH, LANES), F32),
        scratch_shapes=[pltpu.VMEM((LANES * PITCH, LANES), F32), pltpu.VMEM((TT, C_WIDTH), F32)],
        compiler_params=_cparams("parallel", "parallel", "arbitrary"),
        name="lane_relayout",
    )(proj4, proj4, lora_w, lora_b)
    return out.reshape(N_LANE_ARRAYS, g, t_len, PITCH, LANES)


def _lane_tile(p):
    return jnp.tile(p.reshape(C_HEADS, C_HEAD).T, (1, PER))


def _scan_body(*refs, tc, reverse, has_prev):
    r_ref, k_ref, v_ref, w_ref, a_ref, kkp_ref, kap_ref, rkp_ref, lnw_ref, lnb_ref, s0_ref = refs[:11]
    rest = refs[11:]
    if has_prev:
        prev_ref, rest = rest[0], rest[1:]
    o_ref, sfin_ref, s_scr, kka_scr, kz_scr, nkk_scr, y_scr = rest
    c = pl.program_id(1)

    @pl.when(c == 0)
    def _():
        s_scr[...] = s0_ref[0]

    used = lambda ref: ref[0, 0, :, :C_HEAD, :]
    k = used(k_ref)
    a = used(a_ref)
    kk = k * kkp_ref[...]
    kk = kk * lax.rsqrt(jnp.maximum(jnp.sum(kk * kk, axis=1, keepdims=True), 1e-12))
    kz = k * (1.0 + (a - 1.0) * kap_ref[...])
    kz_scr[...] = kz
    kka_scr[...] = kk * a
    nkk_scr[...] = -kk
    bonus = jnp.sum(used(r_ref) * kz * rkp_ref[...], axis=1, keepdims=True) * used(v_ref)

    first = tc - 1 if reverse else 0
    sa0 = jnp.zeros((C_HEAD, LANES), F32)
    for j in range(C_HEAD):
        sa0 = sa0 + s_scr[j] * nkk_scr[first, j:j + 1, :]

    def step(i, sa):
        t = (tc - 1 - i) if reverse else i
        t_next = jnp.maximum(t - 1, 0) if reverse else jnp.minimum(t + 1, tc - 1)
        v = v_ref[0, 0, t, :C_HEAD, :]
        y = jnp.zeros((C_HEAD, LANES), F32)
        sa_next = jnp.zeros((C_HEAD, LANES), F32)
        for j in range(C_HEAD):
            s = s_scr[j] * w_ref[0, 0, t, j:j + 1, :] + sa * kka_scr[t, j:j + 1, :] + v * kz_scr[t, j:j + 1, :]
            s_scr[j] = s
            y = y + s * r_ref[0, 0, t, j:j + 1, :]
            sa_next = sa_next + s * nkk_scr[t_next, j:j + 1, :]
        y_scr[t] = y
        return sa_next

    lax.fori_loop(0, tc, step, sa0)

    y = y_scr[...]
    dev = y - jnp.mean(y, axis=1, keepdims=True)
    var = jnp.mean(dev * dev, axis=1, keepdims=True)
    out = dev * lax.rsqrt(var + GN_EPS) * lnw_ref[...] + lnb_ref[...] + bonus
    if has_prev:
        out = out + prev_ref[0, :, :C_HEAD, :]
    o_ref[0, :, :C_HEAD, :] = out
    o_ref[0, :, C_HEAD:, :] = jnp.zeros((tc, PITCH - C_HEAD, LANES), F32)

    @pl.when(c == pl.num_programs(1) - 1)
    def _():
        sfin_ref[0] = s_scr[...]


def _scan(z7, z, params, s0, prev, *, reverse, tc):
    _, g, t_len = z7.shape[:3]
    n_c = t_len // tc
    tix = (lambda ci: n_c - 1 - ci) if reverse else (lambda ci: ci)
    qblk = lambda q: pl.BlockSpec((1, 1, tc, PITCH, LANES), lambda gi, ci: (q, gi, tix(ci), 0, 0))
    tile = pl.BlockSpec((C_HEAD, LANES), lambda gi, ci: (0, 0))
    st = pl.BlockSpec((1, C_HEAD, C_HEAD, LANES), lambda gi, ci: (gi, 0, 0, 0))
    oblk = pl.BlockSpec((1, tc, PITCH, LANES), lambda gi, ci: (gi, tix(ci), 0, 0))
    in_specs = [qblk(0), qblk(1), qblk(2), qblk(3 + z), qblk(5 + z)] + [tile] * len(params) + [st]
    args = [z7] * 5 + list(params) + [s0]
    if prev is not None:
        in_specs.append(oblk)
        args.append(prev)
    chunk = pltpu.VMEM((tc, C_HEAD, LANES), F32)
    return pl.pallas_call(
        functools.partial(_scan_body, tc=tc, reverse=reverse, has_prev=prev is not None),
        grid=(g, n_c),
        in_specs=in_specs,
        out_specs=[oblk, st],
        out_shape=[jax.ShapeDtypeStruct((g, t_len, PITCH, LANES), F32), jax.ShapeDtypeStruct(s0.shape, F32)],
        scratch_shapes=[pltpu.VMEM((C_HEAD, C_HEAD, LANES), F32)] + [chunk] * 4,
        compiler_params=_cparams("parallel", "arbitrary"),
        name="wkv_scan",
    )(*args)


def _state_to_lanes(s):
    per = LANES // C_HEADS
    bsz = s.shape[0]
    s = s.reshape(bsz // per, per, C_HEADS, C_HEAD, C_HEAD)
    return s.transpose(0, 4, 3, 1, 2).reshape(bsz // per, C_HEAD, C_HEAD, LANES)


def _state_from_lanes(s):
    per = LANES // C_HEADS
    g = s.shape[0]
    s = s.reshape(g, C_HEAD, C_HEAD, per, C_HEADS)
    return s.transpose(0, 3, 4, 2, 1).reshape(g * per, C_HEADS, C_HEAD, C_HEAD)


def _rwkv_out_body(y_ref, x_ref, gate_ref, w_ref, mg_ref, fg_ref, o_ref, r_scr, y_scr):
    b = pl.program_id(2)

    @pl.when(b == 0)
    def _():
        for i in range(C_HEAD):
            m = y_ref[0, pl.ds(i, TT, stride=PITCH), :]
            r_scr[pl.ds(i, LANES, stride=PITCH), :] = m.T
        for bb in range(PER):
            for c in range(C_WIDTH // LANES):
                p0 = bb * C_HEADS + 2 * c
                pair = jnp.concatenate([r_scr[pl.ds(p0 * PITCH, C_HEAD), :],
                                        r_scr[pl.ds((p0 + 1) * PITCH, C_HEAD), :]], axis=0)
                y_scr[bb, :, c * LANES:(c + 1) * LANES] = pair.T

    yg = (y_scr[b] * _silu(gate_ref[0, 0])).astype(BF16)
    xn = x_ref[0, 0] + mg_ref[0] * _dot(yg, w_ref[...])
    o_ref[0, 0] = _rms(xn, fg_ref[...])


def _rwkv_out(x, y, proj, w, gate, final_g, *, bsz, t_len):
    d = x.shape[1]
    g = bsz // PER
    n_mod = gate.shape[0]
    tok = lambda gi, ti, b: (gi, b, ti, 0)
    const = lambda gi, ti, b: (0, 0)
    out = pl.pallas_call(
        _rwkv_out_body,
        grid=(g, t_len // TT, PER),
        in_specs=[
            pl.BlockSpec((1, TT * PITCH, LANES), lambda gi, ti, b: (gi, ti, 0)),
            pl.BlockSpec((1, 1, TT, d), tok),
            pl.BlockSpec((1, 1, TT, C_WIDTH), lambda gi, ti, b: (gi, b, ti, 3)),
            pl.BlockSpec((C_WIDTH, d), const, pipeline_mode=pl.Buffered(1)),
            pl.BlockSpec((1, 1, d), lambda gi, ti, b: ((gi * PER + b) % n_mod, 0, 0)),
            pl.BlockSpec((1, d), const),
        ],
        out_specs=pl.BlockSpec((1, 1, TT, d), tok),
        out_shape=jax.ShapeDtypeStruct((g, PER, t_len, d), F32),
        scratch_shapes=[pltpu.VMEM((LANES * PITCH, LANES), F32), pltpu.VMEM((PER, TT, C_WIDTH), F32)],
        compiler_params=_cparams("parallel", "parallel", "arbitrary"),
        name="rwkv_out",
    )(y.reshape(g, t_len * PITCH, LANES), x.reshape(g, PER, t_len, d), proj.reshape(g, PER, t_len, L1_COLS),
      w, gate, final_g)
    return out.reshape(bsz * t_len, d)


def _rope_tables(t_len):
    pos = jnp.arange(t_len)
    half = 16
    freqs = ROPE_BASE ** (-jnp.arange(half, dtype=F32) / half)
    ang_r = (pos // GRID_W).astype(F32)[:, None] * freqs
    ang_c = (pos % GRID_W).astype(F32)[:, None] * freqs
    cos = jnp.concatenate([jnp.cos(ang_r)] * 2 + [jnp.cos(ang_c)] * 2, axis=-1)
    sin = jnp.concatenate([-jnp.sin(ang_r), jnp.sin(ang_r), -jnp.sin(ang_c), jnp.sin(ang_c)], axis=-1)
    return jnp.tile(cos, (1, 2)), jnp.tile(sin, (1, 2))


def _l0_w_in_layout(w):
    d = w.shape[0]
    aq, ak, av, cq, ckv, kpe, gate = jnp.split(w, [1024, 2048, 3072, 3584, 3840, 3904], axis=1)
    pad = jnp.zeros((d, L0_COLS - 6016), w.dtype)
    return jnp.concatenate([gate, aq, ak, av, cq, ckv, kpe, kpe, pad], axis=1).astype(BF16)


def _l1_cols_layout(w):
    main, wd, ad = jnp.split(w, [4 * C_WIDTH, 4 * C_WIDTH + 2 * C_LORA], axis=1)
    pad = jnp.zeros((w.shape[0], LORA_PAD - C_LORA), w.dtype)
    parts = [main]
    for seg in (wd, ad):
        for z in range(2):
            parts += [seg[:, z * C_LORA:(z + 1) * C_LORA], pad]
    return jnp.concatenate(parts, axis=1)


def _layer0(x, mods, rows_per_mod, seq_len, weights, ctx, tables):
    w_in, w_out, diff_lambda, subln_g, q_norm_g, w_uq, kv_norm_g, w_ukv = weights
    shift, scale, gate, norm_g = mods
    m = x.shape[0]
    bsz = m // seq_len
    proj = _inproj(x, shift, scale, norm_g, w_in, None, rows_per_mod=rows_per_mod, seq_len=seq_len, tm=1024, tn=1024)
    q_b, ckv_n, kv_b = _mla_prep(proj, q_norm_g, kv_norm_g, w_uq, w_ukv, tm=512)
    proj3 = proj.reshape(bsz, seq_len, L0_COLS)
    q_b3 = q_b.reshape(bsz, seq_len, -1)
    kv_b3 = kv_b.reshape(bsz, seq_len, -1)
    if ctx is None:
        ya = _diff_attn(proj3, None, None, None, None, diff_lambda, subln_g, tq=256)
        yb = _mla_attn(proj3, q_b3, kv_b3, None, None, None, None, tq=256)
    else:
        k_ctx, v_ctx, ckv_ctx, kpe_ctx = ctx
        p_len = k_ctx.shape[1]
        cos, sin = tables
        kv_ctx = _matmul(ckv_ctx.reshape(bsz * p_len, B_KV_LORA), w_ukv, tm=512).reshape(bsz, p_len, -1)
        ya = _diff_attn(proj3, k_ctx.reshape(bsz, p_len, -1), v_ctx.reshape(bsz, p_len, -1), cos, sin,
                        diff_lambda, subln_g, tq=256)
        yb = _mla_attn(proj3, q_b3, kv_b3, kv_ctx, jnp.concatenate([kpe_ctx, kpe_ctx], axis=-1), cos, sin, tq=256)
    x_new = _outproj(x, ya.reshape(m, -1), yb.reshape(m, -1), w_out, gate, rows_per_mod=rows_per_mod, tm=512, tn=1024)
    return x_new, proj, ckv_n


def _layer1(x, mods, rows_per_mod, seq_len, weights, states, final_g):
    w_in, w_out, mu, lora_w, lora_b, lane_params = weights
    shift, scale, gate, norm_g = mods
    m = x.shape[0]
    bsz = m // seq_len
    proj = _inproj(x, shift, scale, norm_g, w_in, mu, rows_per_mod=rows_per_mod, seq_len=seq_len, tm=1024, tn=512)
    z7 = _relayout(proj, lora_w, lora_b, bsz, seq_len)
    y, finals = None, []
    for z in range(2):
        y, s_fin = _scan(z7, z, lane_params, _state_to_lanes(states[z]), y, reverse=(z == 1), tc=32)
        finals.append(_state_from_lanes(s_fin))
    out = _rwkv_out(x, y, proj, w_out, gate, final_g, bsz=bsz, t_len=seq_len)
    return out, finals


def kernel(x_prompt, x_sample, cache_l0_a_k, cache_l0_a_v, cache_l0_mla_ckv, cache_l0_mla_kpe, state_l1_fwd, state_l1_bwd, c, c_ctx, mod_w, mod_b, norm_g, final_norm_g, l0_w_in, l0_w_out, l0_diff_lambda, l0_subln_g, l0_q_norm_g, l0_w_uq, l0_kv_norm_g, l0_w_ukv, l1_w_in, l1_w_out, l1_mu, l1_w0, l1_w2, l1_a0, l1_a2, l1_k_k, l1_k_a, l1_r_k, l1_ln_w, l1_ln_b):
    d = D_MODEL
    bp, tp, _ = x_prompt.shape
    bs, ts, _ = x_sample.shape

    cond = jnp.concatenate([c_ctx[None, :], c, jnp.zeros((MOD_ROWS - 1 - bs, d), F32)], axis=0)
    mods = _adaln(cond, mod_w, mod_b)

    def mod_rows(layer, lo, hi):
        rows = mods[layer, lo:hi]
        shift, scale, gate = (rows[:, i * d:(i + 1) * d].reshape(hi - lo, 1, d) for i in range(3))
        return shift, scale, gate, norm_g[layer].reshape(1, d)

    w_uq = l0_w_uq.reshape(B_Q_LORA, B_HEADS, B_NOPE + B_ROPE)
    w_uq = jnp.concatenate([w_uq[:, :, :B_NOPE].reshape(B_Q_LORA, -1), w_uq[:, :, B_NOPE:].reshape(B_Q_LORA, -1)],
                           axis=1).astype(BF16)
    l0_weights = (_l0_w_in_layout(l0_w_in), l0_w_out.astype(BF16), l0_diff_lambda, l0_subln_g.reshape(1, -1),
                  l0_q_norm_g.reshape(1, -1), w_uq, l0_kv_norm_g.reshape(1, -1), l0_w_ukv.astype(BF16))
    lora_pad = jnp.zeros((2, LORA_PAD - C_LORA, C_WIDTH), F32)
    lora_w = jnp.concatenate([jnp.concatenate([l1_w2, lora_pad], axis=1),
                              jnp.concatenate([l1_a2, lora_pad], axis=1)], axis=0).astype(BF16)
    lora_b = jnp.concatenate([l1_w0, l1_a0], axis=0).reshape(4, 1, C_WIDTH)
    lane_params = [_lane_tile(p) for p in (l1_k_k, l1_k_a, l1_r_k, l1_ln_w, l1_ln_b)]
    l1_weights = (_l1_cols_layout(l1_w_in).astype(BF16), l1_w_out.astype(BF16), _l1_cols_layout(l1_mu),
                  lora_w, lora_b, lane_params)
    tables = _rope_tables(ts)
    final_g = final_norm_g.reshape(1, d)

    xp = x_prompt.reshape(bp * tp, d)
    xs = x_sample.reshape(bs * ts, d)
    ctx0 = (cache_l0_a_k, cache_l0_a_v, cache_l0_mla_ckv, cache_l0_mla_kpe)

    xp1, proj_p, ckv_p = _layer0(xp, mod_rows(0, 0, 1), bp * tp, tp, l0_weights, None, None)
    xs1, _, _ = _layer0(xs, mod_rows(0, 1, 1 + bs), ts, ts, l0_weights, ctx0, tables)

    zero_state = jnp.zeros((bp, C_HEADS, C_HEAD, C_HEAD), F32)
    y_prompt, finals = _layer1(xp1, mod_rows(1, 0, 1), bp * tp, tp, l1_weights, (zero_state, zero_state), final_g)
    y_sample, _ = _layer1(xs1, mod_rows(1, 1, 1 + bs), ts, ts, l1_weights, (state_l1_fwd, state_l1_bwd), final_g)

    new_a_k = proj_p[:, 3072:4096].reshape(bp, tp, A_HEADS, 2, A_QK_DIM)
    new_a_v = proj_p[:, 4096:5120].reshape(bp, tp, A_HEADS, A_V_DIM)
    new_ckv = ckv_p.reshape(bp, tp, B_KV_LORA)
    new_kpe = proj_p[:, 5888:5888 + B_ROPE].reshape(bp, tp, B_ROPE)
    return (y_prompt.reshape(bp, tp, d), y_sample.reshape(bs, ts, d), new_a_k, new_a_v, new_ckv, new_kpe,
            finals[0], finals[1])
```

```python
import functools
import math

import jax
import jax.numpy as jnp
from jax import lax
from jax.experimental import pallas as pl
from jax.experimental.pallas import tpu as pltpu

F32 = jnp.float32
BF16 = jnp.bfloat16

D_MODEL = 2048
GRID_W = 64
ROPE_BASE = 10000.0
NORM_EPS = 1e-6
GN_EPS = 64e-5

A_HEADS = 8
A_QK_DIM = 64
A_V_DIM = 128
B_HEADS = 8
B_NOPE = 128
B_ROPE = 64
B_V = 128
B_Q_LORA = 512
B_KV_LORA = 256
AB_WIDTH = A_HEADS * A_V_DIM + B_HEADS * B_V
C_HEAD = 64
C_HEADS = D_MODEL // C_HEAD
C_WIDTH = C_HEADS * C_HEAD
C_LORA = 96
LAM_INIT_L0 = 0.8 - 0.6 * math.exp(-0.3 * 0)

LANES = 128
SUBLANES = 8
LORA_PAD = 128
MOD_ROWS = 8
VMEM_LIMIT = 56 * 1024 * 1024

L0_COLS = 6144
L0_GATE_A, L0_GATE_B, L0_AQ, L0_AK, L0_AV = 0, 1, 2, 3, 4
L0_CQ = 10
L0_CKV = 22
L0_KPE = 46
L1_COLS = 4 * C_WIDTH + 4 * LORA_PAD
PER = LANES // C_HEADS
PITCH = 72
TT = LANES
N_LANE_ARRAYS = 7


def _cparams(*sem):
    return pltpu.CompilerParams(dimension_semantics=sem, vmem_limit_bytes=VMEM_LIMIT)


def _silu(x):
    return x / (1.0 + jnp.exp(-x))


def _dot(a, b):
    return jnp.dot(a, b, preferred_element_type=F32)


def _dot_nt(a, b):
    return lax.dot_general(a, b, (((1,), (1,)), ((), ())), preferred_element_type=F32)


def _rope(x, cos, sin):
    lane = lax.broadcasted_iota(jnp.int32, x.shape, 1)
    low = (lane % 32) < 16
    rot = jnp.where(low, pltpu.roll(x, LANES - 16, 1), pltpu.roll(x, 16, 1))
    return x * cos + rot * sin


def _adaln_body(c_ref, w_ref, b_ref, o_ref):
    s = _silu(c_ref[...]).astype(BF16)
    o_ref[0] = _dot(s, w_ref[0].astype(BF16)) + b_ref[0]


def _adaln(cond, mod_w, mod_b):
    depth, d, n = mod_w.shape
    tn = 1024
    return pl.pallas_call(
        _adaln_body,
        grid=(depth, n // tn),
        in_specs=[
            pl.BlockSpec((MOD_ROWS, d), lambda l, j: (0, 0)),
            pl.BlockSpec((1, d, tn), lambda l, j: (l, 0, j)),
            pl.BlockSpec((1, 1, tn), lambda l, j: (l, 0, j)),
        ],
        out_specs=pl.BlockSpec((1, MOD_ROWS, tn), lambda l, j: (l, 0, j)),
        out_shape=jax.ShapeDtypeStruct((depth, MOD_ROWS, n), F32),
        compiler_params=_cparams("parallel", "parallel"),
        name="adaln",
    )(cond, mod_w, mod_b.reshape(depth, 1, n))


def _inproj_body(x_ref, sh_ref, sc_ref, g_ref, w_ref, *rest, seq_len):
    if seq_len:
        mu_ref, o_ref, h_ref = rest
    else:
        o_ref, h_ref = rest

    @pl.when(pl.program_id(1) == 0)
    def _():
        x = x_ref[...]
        y = x * lax.rsqrt(jnp.mean(x * x, axis=-1, keepdims=True) + NORM_EPS) * g_ref[...]
        h_ref[...] = (y * (1.0 + sc_ref[0]) + sh_ref[0]).astype(BF16)

    p = _dot(h_ref[...], w_ref[...])
    if seq_len:
        tm = p.shape[0]
        row = lax.broadcasted_iota(jnp.int32, (tm, 1), 0) % seq_len
        prev = jnp.where(row == 0, 0.0, pltpu.roll(p, 1, 0))
        nxt = jnp.where(row == seq_len - 1, 0.0, pltpu.roll(p, tm - 1, 0))
        p = p + mu_ref[0:1, :] * (prev - p) + mu_ref[1:2, :] * (nxt - p)
    o_ref[...] = p


def _inproj(x, shift, scale, g, w, mu, *, rows_per_mod, seq_len, tm, tn):
    m, d = x.shape
    n = w.shape[1]
    mod_map = lambda i, j: ((i * tm) // rows_per_mod, 0, 0)
    in_specs = [
        pl.BlockSpec((tm, d), lambda i, j: (i, 0)),
        pl.BlockSpec((1, 1, d), mod_map),
        pl.BlockSpec((1, 1, d), mod_map),
        pl.BlockSpec((1, d), lambda i, j: (0, 0)),
        pl.BlockSpec((d, tn), lambda i, j: (0, j)),
    ]
    args = [x, shift, scale, g, w]
    if mu is not None:
        in_specs.append(pl.BlockSpec((2, tn), lambda i, j: (0, j)))
        args.append(mu)
    return pl.pallas_call(
        functools.partial(_inproj_body, seq_len=seq_len if mu is not None else 0),
        grid=(m // tm, n // tn),
        in_specs=in_specs,
        out_specs=pl.BlockSpec((tm, tn), lambda i, j: (i, j)),
        out_shape=jax.ShapeDtypeStruct((m, n), F32),
        scratch_shapes=[pltpu.VMEM((tm, d), BF16)],
        compiler_params=_cparams("parallel", "arbitrary"),
        name="inproj",
    )(*args)


def _rms(x, g):
    return x * lax.rsqrt(jnp.mean(x * x, axis=-1, keepdims=True) + NORM_EPS) * g


def _mla_prep_body(cq_ref, ckv_ref, qg_ref, kg_ref, wuq_ref, wukv_ref, qb_ref, ckvn_ref, kvb_ref):
    qb_ref[...] = _dot(_rms(cq_ref[...], qg_ref[...]).astype(BF16), wuq_ref[...])
    ckv = _rms(ckv_ref[...], kg_ref[...])
    ckvn_ref[...] = ckv
    kvb_ref[...] = _dot(ckv.astype(BF16), wukv_ref[...])


def _mla_prep(proj, q_norm_g, kv_norm_g, w_uq, w_ukv, *, tm):
    m = proj.shape[0]
    nq, nkv = w_uq.shape[1], w_ukv.shape[1]
    const = lambda i: (0, 0)
    return pl.pallas_call(
        _mla_prep_body,
        grid=(m // tm,),
        in_specs=[
            pl.BlockSpec((tm, B_Q_LORA), lambda i: (i, L0_CQ)),
            pl.BlockSpec((tm, B_KV_LORA), lambda i: (i, L0_CKV)),
            pl.BlockSpec((1, B_Q_LORA), const),
            pl.BlockSpec((1, B_KV_LORA), const),
            pl.BlockSpec((B_Q_LORA, nq), const),
            pl.BlockSpec((B_KV_LORA, nkv), const),
        ],
        out_specs=[
            pl.BlockSpec((tm, nq), lambda i: (i, 0)),
            pl.BlockSpec((tm, B_KV_LORA), lambda i: (i, 0)),
            pl.BlockSpec((tm, nkv), lambda i: (i, 0)),
        ],
        out_shape=[
            jax.ShapeDtypeStruct((m, nq), F32),
            jax.ShapeDtypeStruct((m, B_KV_LORA), F32),
            jax.ShapeDtypeStruct((m, nkv), F32),
        ],
        compiler_params=_cparams("parallel"),
        name="mla_prep",
    )(proj, proj, q_norm_g, kv_norm_g, w_uq, w_ukv)


def _matmul_body(x_ref, w_ref, o_ref):
    o_ref[...] = _dot(x_ref[...].astype(BF16), w_ref[...])


def _matmul(x, w, *, tm):
    m, k = x.shape
    n = w.shape[1]
    return pl.pallas_call(
        _matmul_body,
        grid=(m // tm,),
        in_specs=[pl.BlockSpec((tm, k), lambda i: (i, 0)), pl.BlockSpec((k, n), lambda i: (0, 0))],
        out_specs=pl.BlockSpec((tm, n), lambda i: (i, 0)),
        out_shape=jax.ShapeDtypeStruct((m, n), F32),
        compiler_params=_cparams("parallel"),
        name="matmul",
    )(x, w)


def _softmax_parts(scores):
    m = functools.reduce(jnp.maximum, [jnp.max(s, axis=-1, keepdims=True) for s in scores])
    ps = [jnp.exp(s - m) for s in scores]
    denom = functools.reduce(jnp.add, [jnp.sum(p, axis=-1, keepdims=True) for p in ps])
    return ps, denom


def _diff_attn_body(*refs, dec, tq):
    if dec:
        (q_ref, k_ref, v_ref, kc_ref, vc_ref, gate_ref, cos_ref, sin_ref, lam_ref, sg_ref,
         o_ref, kb, vb, kcb, vcb) = refs
    else:
        q_ref, k_ref, v_ref, gate_ref, lam_ref, sg_ref, o_ref, kb, vb = refs
    qi = pl.program_id(1)

    @pl.when(qi == 0)
    def _():
        for h in range(A_HEADS):
            hs = slice(h * LANES, (h + 1) * LANES)
            kt = k_ref[0, :, hs]
            if dec:
                kt = _rope(kt, cos_ref[...], sin_ref[...])
                kcb[:, hs] = kc_ref[0, :, hs].astype(BF16)
                vcb[:, hs] = vc_ref[0, :, hs].astype(BF16)
            kb[:, hs] = kt.astype(BF16)
            vb[:, hs] = v_ref[0, :, hs].astype(BF16)

    lp = lam_ref[...]
    lam = (jnp.exp(jnp.sum(lp[0:1] * lp[1:2], keepdims=True))
           - jnp.exp(jnp.sum(lp[2:3] * lp[3:4], keepdims=True)) + LAM_INIT_L0)
    if dec:
        row0 = pl.multiple_of(qi * tq, tq)
        cq, sq = cos_ref[pl.ds(row0, tq), :], sin_ref[pl.ds(row0, tq), :]
    first = lax.broadcasted_iota(jnp.int32, (1, LANES), 1) < A_QK_DIM
    for h in range(A_HEADS):
        hs = slice(h * LANES, (h + 1) * LANES)
        qh = q_ref[0, :, hs]
        if dec:
            qh = _rope(qh, cq, sq)
        qh = qh * (A_QK_DIM ** -0.5)
        q1 = jnp.where(first, qh, 0.0).astype(BF16)
        q2 = jnp.where(first, 0.0, qh).astype(BF16)
        keys = [kb[:, hs]] + ([kcb[:, hs]] if dec else [])
        vals = [vb[:, hs]] + ([vcb[:, hs]] if dec else [])
        p1, l1 = _softmax_parts([_dot_nt(q1, kp) for kp in keys])
        p2, l2 = _softmax_parts([_dot_nt(q2, kp) for kp in keys])
        a1 = 1.0 / l1
        a2 = lam / l2
        o = functools.reduce(jnp.add, [_dot((x1 * a1 - x2 * a2).astype(BF16), vp)
                                       for x1, x2, vp in zip(p1, p2, vals)])
        o = _rms(o, sg_ref[...]) * (1.0 - LAM_INIT_L0)
        o_ref[0, :, hs] = (o * _silu(gate_ref[0, :, hs])).astype(BF16)


def _diff_attn(proj, ctx_k, ctx_v, cos, sin, diff_lambda, subln_g, *, tq):
    bsz, t_len, _ = proj.shape
    dec = ctx_k is not None
    w = A_HEADS * LANES
    full = lambda col: pl.BlockSpec((1, t_len, w), lambda b, i: (b, 0, col))
    in_specs = [pl.BlockSpec((1, tq, w), lambda b, i: (b, i, L0_AQ)), full(L0_AK), full(L0_AV)]
    args = [proj, proj, proj]
    scratch = [pltpu.VMEM((t_len, w), BF16), pltpu.VMEM((t_len, w), BF16)]
    if dec:
        p_len = ctx_k.shape[1]
        in_specs += [pl.BlockSpec((1, p_len, w), lambda b, i: (b, 0, 0))] * 2
        args += [ctx_k, ctx_v]
        scratch += [pltpu.VMEM((p_len, w), BF16), pltpu.VMEM((p_len, w), BF16)]
    in_specs.append(pl.BlockSpec((1, tq, w), lambda b, i: (b, i, L0_GATE_A)))
    args.append(proj)
    if dec:
        in_specs += [pl.BlockSpec((t_len, LANES), lambda b, i: (0, 0))] * 2
        args += [cos, sin]
    in_specs += [pl.BlockSpec((4, A_QK_DIM), lambda b, i: (0, 0)),
                 pl.BlockSpec((1, A_V_DIM), lambda b, i: (0, 0))]
    args += [diff_lambda, subln_g]
    return pl.pallas_call(
        functools.partial(_diff_attn_body, dec=dec, tq=tq),
        grid=(bsz, t_len // tq),
        in_specs=in_specs,
        out_specs=pl.BlockSpec((1, tq, w), lambda b, i: (b, i, 0)),
        out_shape=jax.ShapeDtypeStruct((bsz, t_len, w), BF16),
        scratch_shapes=scratch,
        compiler_params=_cparams("parallel", "arbitrary"),
        name="diff_attn",
    )(*args)


def _mla_attn_body(*refs, dec, tq):
    if dec:
        (qn_ref, qp_ref, kv_ref, kpe_ref, kvc_ref, kpec_ref, gate_ref, cos_ref, sin_ref,
         o_ref, kvb, kpeb, kvcb, kpecb) = refs
    else:
        qn_ref, qp_ref, kv_ref, kpe_ref, gate_ref, o_ref, kvb, kpeb = refs
    qi = pl.program_id(1)

    @pl.when(qi == 0)
    def _():
        kvb[...] = kv_ref[0].astype(BF16)
        kp = kpe_ref[0]
        if dec:
            kp = _rope(kp, cos_ref[...], sin_ref[...])
            kvcb[...] = kvc_ref[0].astype(BF16)
            kpecb[...] = kpec_ref[0].astype(BF16)
        kpeb[...] = kp.astype(BF16)

    if dec:
        row0 = pl.multiple_of(qi * tq, tq)
        cq, sq = cos_ref[pl.ds(row0, tq), :], sin_ref[pl.ds(row0, tq), :]
    scale = (B_NOPE + B_ROPE) ** -0.5
    first = lax.broadcasted_iota(jnp.int32, (1, LANES), 1) < B_ROPE
    for h in range(B_HEADS):
        hs = slice(h * LANES, (h + 1) * LANES)
        if h % 2 == 0:
            pair = qp_ref[0, :, (h // 2) * LANES:(h // 2 + 1) * LANES]
            if dec:
                pair = _rope(pair, cq, sq)
            pair = pair * scale
        qp = (jnp.where(first, pair, 0.0) if h % 2 == 0 else jnp.where(first, 0.0, pair)).astype(BF16)
        qn = (qn_ref[0, :, hs] * scale).astype(BF16)
        ks = slice(2 * h * LANES, (2 * h + 1) * LANES)
        vs = slice((2 * h + 1) * LANES, (2 * h + 2) * LANES)
        scores = [_dot_nt(qn, kvb[:, ks]) + _dot_nt(qp, kpeb[...])]
        vals = [kvb[:, vs]]
        if dec:
            scores.append(_dot_nt(qn, kvcb[:, ks]) + _dot_nt(qp, kpecb[...]))
            vals.append(kvcb[:, vs])
        ps, denom = _softmax_parts(scores)
        inv = 1.0 / denom
        o = functools.reduce(jnp.add, [_dot((p * inv).astype(BF16), vp) for p, vp in zip(ps, vals)])
        o_ref[0, :, hs] = (o * _silu(gate_ref[0, :, hs])).astype(BF16)


def _mla_attn(proj, q_b, kv_b, ctx_kv, ctx_kpe, cos, sin, *, tq):
    bsz, t_len, _ = proj.shape
    dec = ctx_kv is not None
    w = B_HEADS * LANES
    nkv = kv_b.shape[2]
    in_specs = [
        pl.BlockSpec((1, tq, w), lambda b, i: (b, i, 0)),
        pl.BlockSpec((1, tq, B_HEADS * B_ROPE), lambda b, i: (b, i, 2)),
        pl.BlockSpec((1, t_len, nkv), lambda b, i: (b, 0, 0)),
        pl.BlockSpec((1, t_len, LANES), lambda b, i: (b, 0, L0_KPE)),
    ]
    args = [q_b, q_b, kv_b, proj]
    scratch = [pltpu.VMEM((t_len, nkv), BF16), pltpu.VMEM((t_len, LANES), BF16)]
    if dec:
        p_len = ctx_kv.shape[1]
        in_specs += [pl.BlockSpec((1, p_len, nkv), lambda b, i: (b, 0, 0)),
                     pl.BlockSpec((1, p_len, LANES), lambda b, i: (b, 0, 0))]
        args += [ctx_kv, ctx_kpe]
        scratch += [pltpu.VMEM((p_len, nkv), BF16), pltpu.VMEM((p_len, LANES), BF16)]
    in_specs.append(pl.BlockSpec((1, tq, w), lambda b, i: (b, i, L0_GATE_B)))
    args.append(proj)
    if dec:
        in_specs += [pl.BlockSpec((t_len, LANES), lambda b, i: (0, 0))] * 2
        args += [cos, sin]
    return pl.pallas_call(
        functools.partial(_mla_attn_body, dec=dec, tq=tq),
        grid=(bsz, t_len // tq),
        in_specs=in_specs,
        out_specs=pl.BlockSpec((1, tq, w), lambda b, i: (b, i, 0)),
        out_shape=jax.ShapeDtypeStruct((bsz, t_len, w), BF16),
        scratch_shapes=scratch,
        compiler_params=_cparams("parallel", "arbitrary"),
        name="mla_attn",
    )(*args)


def _outproj_body(x_ref, ya_ref, yb_ref, w_ref, gate_ref, o_ref):
    half = ya_ref.shape[1]
    acc = _dot(ya_ref[...], w_ref[0:half, :]) + _dot(yb_ref[...], w_ref[half:, :])
    o_ref[...] = x_ref[...] + gate_ref[0] * acc


def _outproj(x, ya, yb, w, gate, *, rows_per_mod, tm, tn):
    m, d = x.shape
    k = w.shape[0]
    return pl.pallas_call(
        _outproj_body,
        grid=(m // tm, d // tn),
        in_specs=[
            pl.BlockSpec((tm, tn), lambda i, j: (i, j)),
            pl.BlockSpec((tm, k // 2), lambda i, j: (i, 0)),
            pl.BlockSpec((tm, k // 2), lambda i, j: (i, 0)),
            pl.BlockSpec((k, tn), lambda i, j: (0, j)),
            pl.BlockSpec((1, 1, tn), lambda i, j: ((i * tm) // rows_per_mod, 0, j)),
        ],
        out_specs=pl.BlockSpec((tm, tn), lambda i, j: (i, j)),
        out_shape=jax.ShapeDtypeStruct((m, d), F32),
        compiler_params=_cparams("parallel", "parallel"),
        name="outproj",
    )(x, ya, yb, w, gate)


def _relayout_body(main_ref, lora_ref, w_ref, bias_ref, o_ref, r_scr, q_scr):
    q = pl.program_id(2)

    def park(tile_of, prepare=None):
        for b in range(PER):
            if prepare is not None:
                prepare(b)
            for c in range(C_WIDTH // LANES):
                at = tile_of(b, c).T
                p0 = b * C_HEADS + 2 * c
                r_scr[pl.ds(p0 * PITCH, C_HEAD), :] = at[:C_HEAD]
                r_scr[pl.ds((p0 + 1) * PITCH, C_HEAD), :] = at[C_HEAD:]

    @pl.when(q < 3)
    def _():
        park(lambda b, c: main_ref[0, b, :, c * LANES:(c + 1) * LANES])

    expanded = lambda b, c: q_scr[:, c * LANES:(c + 1) * LANES]

    @pl.when((q == 3) | (q == 4))
    def _():
        def decay(b):
            u = -(bias_ref[0] + _dot(jnp.tanh(lora_ref[0, b]).astype(BF16), w_ref[0]))
            softplus = jnp.maximum(u, 0.0) + jnp.log(1.0 + jnp.exp(-jnp.abs(u)))
            q_scr[...] = jnp.exp(-jnp.exp(-softplus - 0.5))
        park(expanded, decay)

    @pl.when(q >= 5)
    def _():
        def iclr(b):
            q_scr[...] = 1.0 / (1.0 + jnp.exp(-(bias_ref[0] + _dot(lora_ref[0, b].astype(BF16), w_ref[0]))))
        park(expanded, iclr)

    for j in range(C_HEAD):
        m = r_scr[pl.ds(j, LANES, stride=PITCH), :]
        o_ref[0, 0, pl.ds(j, TT, stride=PITCH), :] = m.T
    zero = jnp.zeros((TT, LANES), F32)
    for j in range(C_HEAD, PITCH):
        o_ref[0, 0, pl.ds(j, TT, stride=PITCH), :] = zero


def _relayout(proj, lora_w, lora_b, bsz, t_len):
    g = bsz // PER
    proj4 = proj.reshape(g, PER, t_len, L1_COLS)
    lora0 = 4 * C_WIDTH // LANES
    lora_ix = lambda gi, ti, q: (jnp.clip(q - 3, 0, 3), 0, 0)
    out = pl.pallas_call(
        _relayout_body,
        grid=(g, t_len // TT, N_LANE_ARRAYS),
        in_specs=[
            pl.BlockSpec((1, PER, TT, C_WIDTH), lambda gi, ti, q: (gi, 0, ti, jnp.minimum(q, 2))),
            pl.BlockSpec((1, PER, TT, LANES), lambda gi, ti, q: (gi, 0, ti, lora0 + jnp.clip(q - 3, 0, 3))),
            pl.BlockSpec((1, LORA_PAD, C_WIDTH), lora_ix),
            pl.BlockSpec((1, 1, C_WIDTH), lora_ix),
        ],
        out_specs=pl.BlockSpec((1, 1, TT * PITCH, LANES), lambda gi, ti, q: (q, gi, ti, 0)),
        out_shape=jax.ShapeDtypeStruct((N_LANE_ARRAYS, g, t_len * PITCH, LANES), F32),
        scratch_shapes=[pltpu.VMEM((LANES * PITCH, LANES), F32), pltpu.VMEM((TT, C_WIDTH), F32)],
        compiler_params=_cparams("parallel", "parallel", "arbitrary"),
        name="lane_relayout",
    )(proj4, proj4, lora_w, lora_b)
    return out.reshape(N_LANE_ARRAYS, g, t_len, PITCH, LANES)


def _lane_tile(p):
    return jnp.tile(p.reshape(C_HEADS, C_HEAD).T, (1, PER))


def _scan_body(*refs, tc, reverse, has_prev):
    r_ref, k_ref, v_ref, w_ref, a_ref, kkp_ref, kap_ref, rkp_ref, lnw_ref, lnb_ref, s0_ref = refs[:11]
    rest = refs[11:]
    if has_prev:
        prev_ref, rest = rest[0], rest[1:]
    o_ref, sfin_ref, s_scr, kka_scr, kz_scr, nkk_scr, y_scr = rest
    c = pl.program_id(1)

    @pl.when(c == 0)
    def _():
        s_scr[...] = s0_ref[0]

    used = lambda ref: ref[0, 0, :, :C_HEAD, :]
    k = used(k_ref)
    a = used(a_ref)
    kk = k * kkp_ref[...]
    kk = kk * lax.rsqrt(jnp.maximum(jnp.sum(kk * kk, axis=1, keepdims=True), 1e-12))
    kz = k * (1.0 + (a - 1.0) * kap_ref[...])
    kz_scr[...] = kz
    kka_scr[...] = kk * a
    nkk_scr[...] = -kk
    bonus = jnp.sum(used(r_ref) * kz * rkp_ref[...], axis=1, keepdims=True) * used(v_ref)

    first = tc - 1 if reverse else 0
    sa0 = jnp.zeros((C_HEAD, LANES), F32)
    for j in range(C_HEAD):
        sa0 = sa0 + s_scr[j] * nkk_scr[first, j:j + 1, :]

    def step(i, sa):
        t = (tc - 1 - i) if reverse else i
        t_next = jnp.maximum(t - 1, 0) if reverse else jnp.minimum(t + 1, tc - 1)
        v = v_ref[0, 0, t, :C_HEAD, :]
        y = jnp.zeros((C_HEAD, LANES), F32)
        sa_next = jnp.zeros((C_HEAD, LANES), F32)
        for j in range(C_HEAD):
            s = s_scr[j] * w_ref[0, 0, t, j:j + 1, :] + sa * kka_scr[t, j:j + 1, :] + v * kz_scr[t, j:j + 1, :]
            s_scr[j] = s
            y = y + s * r_ref[0, 0, t, j:j + 1, :]
            sa_next = sa_next + s * nkk_scr[t_next, j:j + 1, :]
        y_scr[t] = y
        return sa_next

    lax.fori_loop(0, tc, step, sa0)

    y = y_scr[...]
    dev = y - jnp.mean(y, axis=1, keepdims=True)
    var = jnp.mean(dev * dev, axis=1, keepdims=True)
    out = dev * lax.rsqrt(var + GN_EPS) * lnw_ref[...] + lnb_ref[...] + bonus
    if has_prev:
        out = out + prev_ref[0, :, :C_HEAD, :]
    o_ref[0, :, :C_HEAD, :] = out
    o_ref[0, :, C_HEAD:, :] = jnp.zeros((tc, PITCH - C_HEAD, LANES), F32)

    @pl.when(c == pl.num_programs(1) - 1)
    def _():
        sfin_ref[0] = s_scr[...]


def _scan(z7, z, params, s0, prev, *, reverse, tc):
    _, g, t_len = z7.shape[:3]
    n_c = t_len // tc
    tix = (lambda ci: n_c - 1 - ci) if reverse else (lambda ci: ci)
    qblk = lambda q: pl.BlockSpec((1, 1, tc, PITCH, LANES), lambda gi, ci: (q, gi, tix(ci), 0, 0))
    tile = pl.BlockSpec((C_HEAD, LANES), lambda gi, ci: (0, 0))
    st = pl.BlockSpec((1, C_HEAD, C_HEAD, LANES), lambda gi, ci: (gi, 0, 0, 0))
    oblk = pl.BlockSpec((1, tc, PITCH, LANES), lambda gi, ci: (gi, tix(ci), 0, 0))
    in_specs = [qblk(0), qblk(1), qblk(2), qblk(3 + z), qblk(5 + z)] + [tile] * len(params) + [st]
    args = [z7] * 5 + list(params) + [s0]
    if prev is not None:
        in_specs.append(oblk)
        args.append(prev)
    chunk = pltpu.VMEM((tc, C_HEAD, LANES), F32)
    return pl.pallas_call(
        functools.partial(_scan_body, tc=tc, reverse=reverse, has_prev=prev is not None),
        grid=(g, n_c),
        in_specs=in_specs,
        out_specs=[oblk, st],
        out_shape=[jax.ShapeDtypeStruct((g, t_len, PITCH, LANES), F32), jax.ShapeDtypeStruct(s0.shape, F32)],
        scratch_shapes=[pltpu.VMEM((C_HEAD, C_HEAD, LANES), F32)] + [chunk] * 4,
        compiler_params=_cparams("parallel", "arbitrary"),
        name="wkv_scan",
    )(*args)


def _state_to_lanes(s):
    per = LANES // C_HEADS
    bsz = s.shape[0]
    s = s.reshape(bsz // per, per, C_HEADS, C_HEAD, C_HEAD)
    return s.transpose(0, 4, 3, 1, 2).reshape(bsz // per, C_HEAD, C_HEAD, LANES)


def _state_from_lanes(s):
    per = LANES // C_HEADS
    g = s.shape[0]
    s = s.reshape(g, C_HEAD, C_HEAD, per, C_HEADS)
    return s.transpose(0, 3, 4, 2, 1).reshape(g * per, C_HEADS, C_HEAD, C_HEAD)


def _rwkv_out_body(y_ref, x_ref, gate_ref, w_ref, mg_ref, fg_ref, o_ref, r_scr, y_scr):
    b = pl.program_id(2)

    @pl.when(b == 0)
    def _():
        for i in range(C_HEAD):
            m = y_ref[0, pl.ds(i, TT, stride=PITCH), :]
            r_scr[pl.ds(i, LANES, stride=PITCH), :] = m.T
        for bb in range(PER):
            for c in range(C_WIDTH // LANES):
                p0 = bb * C_HEADS + 2 * c
                pair = jnp.concatenate([r_scr[pl.ds(p0 * PITCH, C_HEAD), :],
                                        r_scr[pl.ds((p0 + 1) * PITCH, C_HEAD), :]], axis=0)
                y_scr[bb, :, c * LANES:(c + 1) * LANES] = pair.T

    yg = (y_scr[b] * _silu(gate_ref[0, 0])).astype(BF16)
    xn = x_ref[0, 0] + mg_ref[0] * _dot(yg, w_ref[...])
    o_ref[0, 0] = _rms(xn, fg_ref[...])


def _rwkv_out(x, y, proj, w, gate, final_g, *, bsz, t_len):
    d = x.shape[1]
    g = bsz // PER
    n_mod = gate.shape[0]
    tok = lambda gi, ti, b: (gi, b, ti, 0)
    const = lambda gi, ti, b: (0, 0)
    out = pl.pallas_call(
        _rwkv_out_body,
        grid=(g, t_len // TT, PER),
        in_specs=[
            pl.BlockSpec((1, TT * PITCH, LANES), lambda gi, ti, b: (gi, ti, 0)),
            pl.BlockSpec((1, 1, TT, d), tok),
            pl.BlockSpec((1, 1, TT, C_WIDTH), lambda gi, ti, b: (gi, b, ti, 3)),
            pl.BlockSpec((C_WIDTH, d), const, pipeline_mode=pl.Buffered(1)),
            pl.BlockSpec((1, 1, d), lambda gi, ti, b: ((gi * PER + b) % n_mod, 0, 0)),
            pl.BlockSpec((1, d), const),
        ],
        out_specs=pl.BlockSpec((1, 1, TT, d), tok),
        out_shape=jax.ShapeDtypeStruct((g, PER, t_len, d), F32),
        scratch_shapes=[pltpu.VMEM((LANES * PITCH, LANES), F32), pltpu.VMEM((PER, TT, C_WIDTH), F32)],
        compiler_params=_cparams("parallel", "parallel", "arbitrary"),
        name="rwkv_out",
    )(y.reshape(g, t_len * PITCH, LANES), x.reshape(g, PER, t_len, d), proj.reshape(g, PER, t_len, L1_COLS),
      w, gate, final_g)
    return out.reshape(bsz * t_len, d)


def _rope_tables(t_len):
    pos = jnp.arange(t_len)
    half = 16
    freqs = ROPE_BASE ** (-jnp.arange(half, dtype=F32) / half)
    ang_r = (pos // GRID_W).astype(F32)[:, None] * freqs
    ang_c = (pos % GRID_W).astype(F32)[:, None] * freqs
    cos = jnp.concatenate([jnp.cos(ang_r)] * 2 + [jnp.cos(ang_c)] * 2, axis=-1)
    sin = jnp.concatenate([-jnp.sin(ang_r), jnp.sin(ang_r), -jnp.sin(ang_c), jnp.sin(ang_c)], axis=-1)
    return jnp.tile(cos, (1, 2)), jnp.tile(sin, (1, 2))


def _l0_w_in_layout(w):
    d = w.shape[0]
    aq, ak, av, cq, ckv, kpe, gate = jnp.split(w.astype(BF16), [1024, 2048, 3072, 3584, 3840, 3904], axis=1)
    pad = jnp.zeros((d, L0_COLS - 6016), BF16)
    return jnp.concatenate([gate, aq, ak, av, cq, ckv, kpe, kpe, pad], axis=1)


def _l1_cols_layout(w):
    main, wd, ad = jnp.split(w, [4 * C_WIDTH, 4 * C_WIDTH + 2 * C_LORA], axis=1)
    pad = jnp.zeros((w.shape[0], LORA_PAD - C_LORA), w.dtype)
    parts = [main]
    for seg in (wd, ad):
        for z in range(2):
            parts += [seg[:, z * C_LORA:(z + 1) * C_LORA], pad]
    return jnp.concatenate(parts, axis=1)


def _layer0(x, mods, rows_per_mod, seq_len, weights, ctx, tables):
    w_in, w_out, diff_lambda, subln_g, q_norm_g, w_uq, kv_norm_g, w_ukv = weights
    shift, scale, gate, norm_g = mods
    m = x.shape[0]
    bsz = m // seq_len
    proj = _inproj(x, shift, scale, norm_g, w_in, None, rows_per_mod=rows_per_mod, seq_len=seq_len, tm=1024, tn=1024)
    q_b, ckv_n, kv_b = _mla_prep(proj, q_norm_g, kv_norm_g, w_uq, w_ukv, tm=512)
    proj3 = proj.reshape(bsz, seq_len, L0_COLS)
    q_b3 = q_b.reshape(bsz, seq_len, -1)
    kv_b3 = kv_b.reshape(bsz, seq_len, -1)
    if ctx is None:
        ya = _diff_attn(proj3, None, None, None, None, diff_lambda, subln_g, tq=256)
        yb = _mla_attn(proj3, q_b3, kv_b3, None, None, None, None, tq=256)
    else:
        k_ctx, v_ctx, ckv_ctx, kpe_ctx = ctx
        p_len = k_ctx.shape[1]
        cos, sin = tables
        kv_ctx = _matmul(ckv_ctx.reshape(bsz * p_len, B_KV_LORA), w_ukv, tm=512).reshape(bsz, p_len, -1)
        ya = _diff_attn(proj3, k_ctx.reshape(bsz, p_len, -1), v_ctx.reshape(bsz, p_len, -1), cos, sin,
                        diff_lambda, subln_g, tq=256)
        yb = _mla_attn(proj3, q_b3, kv_b3, kv_ctx, jnp.concatenate([kpe_ctx, kpe_ctx], axis=-1), cos, sin, tq=256)
    x_new = _outproj(x, ya.reshape(m, -1), yb.reshape(m, -1), w_out, gate, rows_per_mod=rows_per_mod, tm=512, tn=1024)
    return x_new, proj, ckv_n


def _layer1(x, mods, rows_per_mod, seq_len, weights, states, final_g):
    w_in, w_out, mu, lora_w, lora_b, lane_params = weights
    shift, scale, gate, norm_g = mods
    m = x.shape[0]
    bsz = m // seq_len
    proj = _inproj(x, shift, scale, norm_g, w_in, mu, rows_per_mod=rows_per_mod, seq_len=seq_len, tm=1024, tn=512)
    z7 = _relayout(proj, lora_w, lora_b, bsz, seq_len)
    y, finals = None, []
    for z in range(2):
        y, s_fin = _scan(z7, z, lane_params, states[z], y, reverse=(z == 1), tc=32)
        finals.append(_state_from_lanes(s_fin))
    out = _rwkv_out(x, y, proj, w_out, gate, final_g, bsz=bsz, t_len=seq_len)
    return out, finals


def kernel(x_prompt, x_sample, cache_l0_a_k, cache_l0_a_v, cache_l0_mla_ckv, cache_l0_mla_kpe, state_l1_fwd, state_l1_bwd, c, c_ctx, mod_w, mod_b, norm_g, final_norm_g, l0_w_in, l0_w_out, l0_diff_lambda, l0_subln_g, l0_q_norm_g, l0_w_uq, l0_kv_norm_g, l0_w_ukv, l1_w_in, l1_w_out, l1_mu, l1_w0, l1_w2, l1_a0, l1_a2, l1_k_k, l1_k_a, l1_r_k, l1_ln_w, l1_ln_b):
    d = D_MODEL
    bp, tp, _ = x_prompt.shape
    bs, ts, _ = x_sample.shape

    cond = jnp.concatenate([c_ctx[None, :], c, jnp.zeros((MOD_ROWS - 1 - bs, d), F32)], axis=0)
    mods = _adaln(cond, mod_w, mod_b)

    def mod_rows(layer, lo, hi):
        rows = mods[layer, lo:hi]
        shift, scale, gate = (rows[:, i * d:(i + 1) * d].reshape(hi - lo, 1, d) for i in range(3))
        return shift, scale, gate, norm_g[layer].reshape(1, d)

    w_uq = l0_w_uq.reshape(B_Q_LORA, B_HEADS, B_NOPE + B_ROPE)
    w_uq = jnp.concatenate([w_uq[:, :, :B_NOPE].reshape(B_Q_LORA, -1), w_uq[:, :, B_NOPE:].reshape(B_Q_LORA, -1)],
                           axis=1).astype(BF16)
    l0_weights = (_l0_w_in_layout(l0_w_in), l0_w_out.astype(BF16), l0_diff_lambda, l0_subln_g.reshape(1, -1),
                  l0_q_norm_g.reshape(1, -1), w_uq, l0_kv_norm_g.reshape(1, -1), l0_w_ukv.astype(BF16))
    lora_pad = jnp.zeros((2, LORA_PAD - C_LORA, C_WIDTH), F32)
    lora_w = jnp.concatenate([jnp.concatenate([l1_w2, lora_pad], axis=1),
                              jnp.concatenate([l1_a2, lora_pad], axis=1)], axis=0).astype(BF16)
    lora_b = jnp.concatenate([l1_w0, l1_a0], axis=0).reshape(4, 1, C_WIDTH)
    lane_params = [_lane_tile(p) for p in (l1_k_k, l1_k_a, l1_r_k, l1_ln_w, l1_ln_b)]
    l1_weights = (_l1_cols_layout(l1_w_in.astype(BF16)), l1_w_out.astype(BF16), _l1_cols_layout(l1_mu),
                  lora_w, lora_b, lane_params)
    tables = _rope_tables(ts)
    final_g = final_norm_g.reshape(1, d)

    xp = x_prompt.reshape(bp * tp, d)
    xs = x_sample.reshape(bs * ts, d)
    ctx0 = (cache_l0_a_k, cache_l0_a_v, cache_l0_mla_ckv, cache_l0_mla_kpe)

    xp1, proj_p, ckv_p = _layer0(xp, mod_rows(0, 0, 1), bp * tp, tp, l0_weights, None, None)
    xs1, _, _ = _layer0(xs, mod_rows(0, 1, 1 + bs), ts, ts, l0_weights, ctx0, tables)

    zero_state = jnp.zeros((bp // PER, C_HEAD, C_HEAD, LANES), F32)
    y_prompt, finals = _layer1(xp1, mod_rows(1, 0, 1), bp * tp, tp, l1_weights, (zero_state, zero_state), final_g)
    y_sample, _ = _layer1(xs1, mod_rows(1, 1, 1 + bs), ts, ts, l1_weights,
                          (_state_to_lanes(state_l1_fwd), _state_to_lanes(state_l1_bwd)), final_g)

    new_a_k = proj_p[:, 3072:4096].reshape(bp, tp, A_HEADS, 2, A_QK_DIM)
    new_a_v = proj_p[:, 4096:5120].reshape(bp, tp, A_HEADS, A_V_DIM)
    new_ckv = ckv_p.reshape(bp, tp, B_KV_LORA)
    new_kpe = proj_p[:, 5888:5888 + B_ROPE].reshape(bp, tp, B_ROPE)
    return (y_prompt.reshape(bp, tp, d), y_sample.reshape(bs, ts, d), new_a_k, new_a_v, new_ckv, new_kpe,
            finals[0], finals[1])
```

```python
import functools
import math

import jax
import jax.numpy as jnp
from jax import lax
from jax.experimental import pallas as pl
from jax.experimental.pallas import tpu as pltpu

F32 = jnp.float32
BF16 = jnp.bfloat16

D_MODEL = 2048
GRID_W = 64
ROPE_BASE = 10000.0
NORM_EPS = 1e-6
GN_EPS = 64e-5

A_HEADS = 8
A_QK_DIM = 64
A_V_DIM = 128
B_HEADS = 8
B_NOPE = 128
B_ROPE = 64
B_V = 128
B_Q_LORA = 512
B_KV_LORA = 256
AB_WIDTH = A_HEADS * A_V_DIM + B_HEADS * B_V
C_HEAD = 64
C_HEADS = D_MODEL // C_HEAD
C_WIDTH = C_HEADS * C_HEAD
C_LORA = 96
LAM_INIT_L0 = 0.8 - 0.6 * math.exp(-0.3 * 0)

LANES = 128
SUBLANES = 8
LORA_PAD = 128
MOD_ROWS = 8
VMEM_LIMIT = 56 * 1024 * 1024

L0_COLS = 6144
L0_GATE_A, L0_GATE_B, L0_AQ, L0_AK, L0_AV = 0, 1, 2, 3, 4
L0_CQ = 10
L0_CKV = 22
L0_KPE = 46
L1_COLS = 4 * C_WIDTH + 4 * LORA_PAD
PER = LANES // C_HEADS
PITCH = 72
TT = LANES
N_LANE_ARRAYS = 7


def _cparams(*sem):
    return pltpu.CompilerParams(dimension_semantics=sem, vmem_limit_bytes=VMEM_LIMIT)


def _silu(x):
    return x / (1.0 + jnp.exp(-x))


def _dot(a, b):
    return jnp.dot(a, b, preferred_element_type=F32)


def _dot_nt(a, b):
    return lax.dot_general(a, b, (((1,), (1,)), ((), ())), preferred_element_type=F32)


def _rope(x, cos, sin):
    lane = lax.broadcasted_iota(jnp.int32, x.shape, 1)
    low = (lane % 32) < 16
    rot = jnp.where(low, pltpu.roll(x, LANES - 16, 1), pltpu.roll(x, 16, 1))
    return x * cos + rot * sin


def _adaln_body(c_ref, w_ref, b_ref, o_ref):
    s = _silu(c_ref[...]).astype(BF16)
    o_ref[0] = _dot(s, w_ref[0].astype(BF16)) + b_ref[0]


def _adaln(cond, mod_w, mod_b):
    depth, d, n = mod_w.shape
    tn = 1024
    return pl.pallas_call(
        _adaln_body,
        grid=(depth, n // tn),
        in_specs=[
            pl.BlockSpec((MOD_ROWS, d), lambda l, j: (0, 0)),
            pl.BlockSpec((1, d, tn), lambda l, j: (l, 0, j)),
            pl.BlockSpec((1, 1, tn), lambda l, j: (l, 0, j)),
        ],
        out_specs=pl.BlockSpec((1, MOD_ROWS, tn), lambda l, j: (l, 0, j)),
        out_shape=jax.ShapeDtypeStruct((depth, MOD_ROWS, n), F32),
        compiler_params=_cparams("parallel", "parallel"),
        name="adaln",
    )(cond, mod_w, mod_b.reshape(depth, 1, n))


def _inproj_body(x_ref, sh_ref, sc_ref, g_ref, w_ref, *rest, seq_len):
    if seq_len:
        mu_ref, o_ref, h_ref = rest
    else:
        o_ref, h_ref = rest

    @pl.when(pl.program_id(1) == 0)
    def _():
        x = x_ref[...]
        y = x * lax.rsqrt(jnp.mean(x * x, axis=-1, keepdims=True) + NORM_EPS) * g_ref[...]
        h_ref[...] = (y * (1.0 + sc_ref[0]) + sh_ref[0]).astype(BF16)

    p = _dot(h_ref[...], w_ref[...])
    if seq_len:
        tm = p.shape[0]
        row = lax.broadcasted_iota(jnp.int32, (tm, 1), 0) % seq_len
        prev = jnp.where(row == 0, 0.0, pltpu.roll(p, 1, 0))
        nxt = jnp.where(row == seq_len - 1, 0.0, pltpu.roll(p, tm - 1, 0))
        p = p + mu_ref[0:1, :] * (prev - p) + mu_ref[1:2, :] * (nxt - p)
    o_ref[...] = p


def _inproj(x, shift, scale, g, w, mu, *, rows_per_mod, seq_len, tm, tn):
    m, d = x.shape
    n = w.shape[1]
    mod_map = lambda i, j: ((i * tm) // rows_per_mod, 0, 0)
    in_specs = [
        pl.BlockSpec((tm, d), lambda i, j: (i, 0)),
        pl.BlockSpec((1, 1, d), mod_map),
        pl.BlockSpec((1, 1, d), mod_map),
        pl.BlockSpec((1, d), lambda i, j: (0, 0)),
        pl.BlockSpec((d, tn), lambda i, j: (0, j)),
    ]
    args = [x, shift, scale, g, w]
    if mu is not None:
        in_specs.append(pl.BlockSpec((2, tn), lambda i, j: (0, j)))
        args.append(mu)
    return pl.pallas_call(
        functools.partial(_inproj_body, seq_len=seq_len if mu is not None else 0),
        grid=(m // tm, n // tn),
        in_specs=in_specs,
        out_specs=pl.BlockSpec((tm, tn), lambda i, j: (i, j)),
        out_shape=jax.ShapeDtypeStruct((m, n), F32),
        scratch_shapes=[pltpu.VMEM((tm, d), BF16)],
        compiler_params=_cparams("parallel", "arbitrary"),
        name="inproj",
    )(*args)


def _rms(x, g):
    return x * lax.rsqrt(jnp.mean(x * x, axis=-1, keepdims=True) + NORM_EPS) * g


def _mla_prep_body(cq_ref, ckv_ref, qg_ref, kg_ref, wuq_ref, wukv_ref, qb_ref, ckvn_ref, kvb_ref):
    qb_ref[...] = _dot(_rms(cq_ref[...], qg_ref[...]).astype(BF16), wuq_ref[...])
    ckv = _rms(ckv_ref[...], kg_ref[...])
    ckvn_ref[...] = ckv
    kvb_ref[...] = _dot(ckv.astype(BF16), wukv_ref[...])


def _mla_prep(proj, q_norm_g, kv_norm_g, w_uq, w_ukv, *, tm):
    m = proj.shape[0]
    nq, nkv = w_uq.shape[1], w_ukv.shape[1]
    const = lambda i: (0, 0)
    return pl.pallas_call(
        _mla_prep_body,
        grid=(m // tm,),
        in_specs=[
            pl.BlockSpec((tm, B_Q_LORA), lambda i: (i, L0_CQ)),
            pl.BlockSpec((tm, B_KV_LORA), lambda i: (i, L0_CKV)),
            pl.BlockSpec((1, B_Q_LORA), const),
            pl.BlockSpec((1, B_KV_LORA), const),
            pl.BlockSpec((B_Q_LORA, nq), const),
            pl.BlockSpec((B_KV_LORA, nkv), const),
        ],
        out_specs=[
            pl.BlockSpec((tm, nq), lambda i: (i, 0)),
            pl.BlockSpec((tm, B_KV_LORA), lambda i: (i, 0)),
            pl.BlockSpec((tm, nkv), lambda i: (i, 0)),
        ],
        out_shape=[
            jax.ShapeDtypeStruct((m, nq), F32),
            jax.ShapeDtypeStruct((m, B_KV_LORA), F32),
            jax.ShapeDtypeStruct((m, nkv), F32),
        ],
        compiler_params=_cparams("parallel"),
        name="mla_prep",
    )(proj, proj, q_norm_g, kv_norm_g, w_uq, w_ukv)


def _matmul_body(x_ref, w_ref, o_ref):
    o_ref[...] = _dot(x_ref[...].astype(BF16), w_ref[...])


def _matmul(x, w, *, tm):
    m, k = x.shape
    n = w.shape[1]
    return pl.pallas_call(
        _matmul_body,
        grid=(m // tm,),
        in_specs=[pl.BlockSpec((tm, k), lambda i: (i, 0)), pl.BlockSpec((k, n), lambda i: (0, 0))],
        out_specs=pl.BlockSpec((tm, n), lambda i: (i, 0)),
        out_shape=jax.ShapeDtypeStruct((m, n), F32),
        compiler_params=_cparams("parallel"),
        name="matmul",
    )(x, w)


def _softmax_parts(scores):
    m = functools.reduce(jnp.maximum, [jnp.max(s, axis=-1, keepdims=True) for s in scores])
    ps = [jnp.exp(s - m) for s in scores]
    denom = functools.reduce(jnp.add, [jnp.sum(p, axis=-1, keepdims=True) for p in ps])
    return ps, denom


def _diff_attn_body(*refs, dec, tq):
    if dec:
        (q_ref, k_ref, v_ref, kc_ref, vc_ref, gate_ref, cos_ref, sin_ref, lam_ref, sg_ref,
         o_ref, kb, vb, kcb, vcb) = refs
    else:
        q_ref, k_ref, v_ref, gate_ref, lam_ref, sg_ref, o_ref, kb, vb = refs
    qi = pl.program_id(1)

    @pl.when(qi == 0)
    def _():
        for h in range(A_HEADS):
            hs = slice(h * LANES, (h + 1) * LANES)
            kt = k_ref[0, :, hs]
            if dec:
                kt = _rope(kt, cos_ref[...], sin_ref[...])
                kcb[:, hs] = kc_ref[0, :, hs].astype(BF16)
                vcb[:, hs] = vc_ref[0, :, hs].astype(BF16)
            kb[:, hs] = kt.astype(BF16)
            vb[:, hs] = v_ref[0, :, hs].astype(BF16)

    lp = lam_ref[...]
    lam = (jnp.exp(jnp.sum(lp[0:1] * lp[1:2], keepdims=True))
           - jnp.exp(jnp.sum(lp[2:3] * lp[3:4], keepdims=True)) + LAM_INIT_L0)
    if dec:
        row0 = pl.multiple_of(qi * tq, tq)
        cq, sq = cos_ref[pl.ds(row0, tq), :], sin_ref[pl.ds(row0, tq), :]
    first = lax.broadcasted_iota(jnp.int32, (1, LANES), 1) < A_QK_DIM
    for h in range(A_HEADS):
        hs = slice(h * LANES, (h + 1) * LANES)
        qh = q_ref[0, :, hs]
        if dec:
            qh = _rope(qh, cq, sq)
        qh = qh * (A_QK_DIM ** -0.5)
        q1 = jnp.where(first, qh, 0.0).astype(BF16)
        q2 = jnp.where(first, 0.0, qh).astype(BF16)
        keys = [kb[:, hs]] + ([kcb[:, hs]] if dec else [])
        vals = [vb[:, hs]] + ([vcb[:, hs]] if dec else [])
        p1, l1 = _softmax_parts([_dot_nt(q1, kp) for kp in keys])
        p2, l2 = _softmax_parts([_dot_nt(q2, kp) for kp in keys])
        o1 = functools.reduce(jnp.add, [_dot(x.astype(BF16), vp) for x, vp in zip(p1, vals)])
        o2 = functools.reduce(jnp.add, [_dot(x.astype(BF16), vp) for x, vp in zip(p2, vals)])
        o = o1 * (1.0 / l1) - o2 * (lam / l2)
        o = _rms(o, sg_ref[...]) * (1.0 - LAM_INIT_L0)
        o_ref[0, :, hs] = (o * _silu(gate_ref[0, :, hs])).astype(BF16)


def _diff_attn(proj, ctx_k, ctx_v, cos, sin, diff_lambda, subln_g, *, tq):
    bsz, t_len, _ = proj.shape
    dec = ctx_k is not None
    w = A_HEADS * LANES
    full = lambda col: pl.BlockSpec((1, t_len, w), lambda b, i: (b, 0, col))
    in_specs = [pl.BlockSpec((1, tq, w), lambda b, i: (b, i, L0_AQ)), full(L0_AK), full(L0_AV)]
    args = [proj, proj, proj]
    scratch = [pltpu.VMEM((t_len, w), BF16), pltpu.VMEM((t_len, w), BF16)]
    if dec:
        p_len = ctx_k.shape[1]
        in_specs += [pl.BlockSpec((1, p_len, w), lambda b, i: (b, 0, 0))] * 2
        args += [ctx_k, ctx_v]
        scratch += [pltpu.VMEM((p_len, w), BF16), pltpu.VMEM((p_len, w), BF16)]
    in_specs.append(pl.BlockSpec((1, tq, w), lambda b, i: (b, i, L0_GATE_A)))
    args.append(proj)
    if dec:
        in_specs += [pl.BlockSpec((t_len, LANES), lambda b, i: (0, 0))] * 2
        args += [cos, sin]
    in_specs += [pl.BlockSpec((4, A_QK_DIM), lambda b, i: (0, 0)),
                 pl.BlockSpec((1, A_V_DIM), lambda b, i: (0, 0))]
    args += [diff_lambda, subln_g]
    return pl.pallas_call(
        functools.partial(_diff_attn_body, dec=dec, tq=tq),
        grid=(bsz, t_len // tq),
        in_specs=in_specs,
        out_specs=pl.BlockSpec((1, tq, w), lambda b, i: (b, i, 0)),
        out_shape=jax.ShapeDtypeStruct((bsz, t_len, w), BF16),
        scratch_shapes=scratch,
        compiler_params=_cparams("parallel", "arbitrary"),
        name="diff_attn",
    )(*args)


def _mla_attn_body(*refs, dec, tq):
    if dec:
        (qn_ref, qp_ref, kv_ref, kpe_ref, kvc_ref, kpec_ref, gate_ref, cos_ref, sin_ref,
         o_ref, kb, vb, kcb, vcb) = refs
    else:
        qn_ref, qp_ref, kv_ref, kpe_ref, gate_ref, o_ref, kb, vb = refs
    qi = pl.program_id(1)

    def stage(src_ref, kpe, k_dst, v_dst):
        for h in range(B_HEADS):
            k_dst[:, 2 * h * LANES:(2 * h + 1) * LANES] = src_ref[0, :, 2 * h * LANES:(2 * h + 1) * LANES].astype(BF16)
            k_dst[:, (2 * h + 1) * LANES:(2 * h + 2) * LANES] = kpe
            v_dst[:, h * LANES:(h + 1) * LANES] = src_ref[0, :, (2 * h + 1) * LANES:(2 * h + 2) * LANES].astype(BF16)

    @pl.when(qi == 0)
    def _():
        kp = kpe_ref[0]
        if dec:
            kp = _rope(kp, cos_ref[...], sin_ref[...])
            stage(kvc_ref, kpec_ref[0].astype(BF16), kcb, vcb)
        stage(kv_ref, kp.astype(BF16), kb, vb)

    if dec:
        row0 = pl.multiple_of(qi * tq, tq)
        cq, sq = cos_ref[pl.ds(row0, tq), :], sin_ref[pl.ds(row0, tq), :]
    scale = (B_NOPE + B_ROPE) ** -0.5
    first = lax.broadcasted_iota(jnp.int32, (1, LANES), 1) < B_ROPE
    for h in range(B_HEADS):
        hs = slice(h * LANES, (h + 1) * LANES)
        if h % 2 == 0:
            pair = qp_ref[0, :, (h // 2) * LANES:(h // 2 + 1) * LANES]
            if dec:
                pair = _rope(pair, cq, sq)
            pair = pair * scale
        qp = (jnp.where(first, pair, 0.0) if h % 2 == 0 else jnp.where(first, 0.0, pair)).astype(BF16)
        qn = (qn_ref[0, :, hs] * scale).astype(BF16)
        q = jnp.concatenate([qn, qp], axis=1)
        ks = slice(2 * h * LANES, (2 * h + 2) * LANES)
        scores = [_dot_nt(q, kb[:, ks])]
        vals = [vb[:, hs]]
        if dec:
            scores.append(_dot_nt(q, kcb[:, ks]))
            vals.append(vcb[:, hs])
        ps, denom = _softmax_parts(scores)
        o = functools.reduce(jnp.add, [_dot(p.astype(BF16), vp) for p, vp in zip(ps, vals)]) * (1.0 / denom)
        o_ref[0, :, hs] = (o * _silu(gate_ref[0, :, hs])).astype(BF16)


def _mla_attn(proj, q_b, kv_b, ctx_kv, ctx_kpe, cos, sin, *, tq):
    bsz, t_len, _ = proj.shape
    dec = ctx_kv is not None
    w = B_HEADS * LANES
    nkv = kv_b.shape[2]
    in_specs = [
        pl.BlockSpec((1, tq, w), lambda b, i: (b, i, 0)),
        pl.BlockSpec((1, tq, B_HEADS * B_ROPE), lambda b, i: (b, i, 2)),
        pl.BlockSpec((1, t_len, nkv), lambda b, i: (b, 0, 0)),
        pl.BlockSpec((1, t_len, LANES), lambda b, i: (b, 0, L0_KPE)),
    ]
    args = [q_b, q_b, kv_b, proj]
    scratch = [pltpu.VMEM((t_len, 2 * w), BF16), pltpu.VMEM((t_len, w), BF16)]
    if dec:
        p_len = ctx_kv.shape[1]
        in_specs += [pl.BlockSpec((1, p_len, nkv), lambda b, i: (b, 0, 0)),
                     pl.BlockSpec((1, p_len, LANES), lambda b, i: (b, 0, 0))]
        args += [ctx_kv, ctx_kpe]
        scratch += [pltpu.VMEM((p_len, 2 * w), BF16), pltpu.VMEM((p_len, w), BF16)]
    in_specs.append(pl.BlockSpec((1, tq, w), lambda b, i: (b, i, L0_GATE_B)))
    args.append(proj)
    if dec:
        in_specs += [pl.BlockSpec((t_len, LANES), lambda b, i: (0, 0))] * 2
        args += [cos, sin]
    return pl.pallas_call(
        functools.partial(_mla_attn_body, dec=dec, tq=tq),
        grid=(bsz, t_len // tq),
        in_specs=in_specs,
        out_specs=pl.BlockSpec((1, tq, w), lambda b, i: (b, i, 0)),
        out_shape=jax.ShapeDtypeStruct((bsz, t_len, w), BF16),
        scratch_shapes=scratch,
        compiler_params=_cparams("parallel", "arbitrary"),
        name="mla_attn",
    )(*args)


def _outproj_body(x_ref, ya_ref, yb_ref, w_ref, gate_ref, o_ref):
    half = ya_ref.shape[1]
    acc = _dot(ya_ref[...], w_ref[0:half, :]) + _dot(yb_ref[...], w_ref[half:, :])
    o_ref[...] = x_ref[...] + gate_ref[0] * acc


def _outproj(x, ya, yb, w, gate, *, rows_per_mod, tm):
    m, d = x.shape
    k = w.shape[0]
    return pl.pallas_call(
        _outproj_body,
        grid=(m // tm,),
        in_specs=[
            pl.BlockSpec((tm, d), lambda i: (i, 0)),
            pl.BlockSpec((tm, k // 2), lambda i: (i, 0)),
            pl.BlockSpec((tm, k // 2), lambda i: (i, 0)),
            pl.BlockSpec((k, d), lambda i: (0, 0), pipeline_mode=pl.Buffered(1)),
            pl.BlockSpec((1, 1, d), lambda i: ((i * tm) // rows_per_mod, 0, 0)),
        ],
        out_specs=pl.BlockSpec((tm, d), lambda i: (i, 0)),
        out_shape=jax.ShapeDtypeStruct((m, d), F32),
        compiler_params=_cparams("parallel"),
        name="outproj",
    )(x, ya, yb, w, gate)


def _relayout_body(main_ref, lora_ref, w_ref, bias_ref, o_ref, r_scr, q_scr):
    q = pl.program_id(2)

    def park(tile_of, prepare=None):
        for b in range(PER):
            if prepare is not None:
                prepare(b)
            for c in range(C_WIDTH // LANES):
                at = tile_of(b, c).T
                p0 = b * C_HEADS + 2 * c
                r_scr[pl.ds(p0 * PITCH, C_HEAD), :] = at[:C_HEAD]
                r_scr[pl.ds((p0 + 1) * PITCH, C_HEAD), :] = at[C_HEAD:]

    @pl.when(q < 3)
    def _():
        park(lambda b, c: main_ref[0, b, :, c * LANES:(c + 1) * LANES])

    expanded = lambda b, c: q_scr[:, c * LANES:(c + 1) * LANES]

    @pl.when((q == 3) | (q == 4))
    def _():
        def decay(b):
            u = -(bias_ref[0] + _dot(jnp.tanh(lora_ref[0, b]).astype(BF16), w_ref[0]))
            softplus = jnp.maximum(u, 0.0) + jnp.log(1.0 + jnp.exp(-jnp.abs(u)))
            q_scr[...] = jnp.exp(-jnp.exp(-softplus - 0.5))
        park(expanded, decay)

    @pl.when(q >= 5)
    def _():
        def iclr(b):
            q_scr[...] = 1.0 / (1.0 + jnp.exp(-(bias_ref[0] + _dot(lora_ref[0, b].astype(BF16), w_ref[0]))))
        park(expanded, iclr)

    for j in range(C_HEAD):
        m = r_scr[pl.ds(j, LANES, stride=PITCH), :]
        o_ref[0, 0, pl.ds(j, TT, stride=PITCH), :] = m.T
    zero = jnp.zeros((TT, LANES), F32)
    for j in range(C_HEAD, PITCH):
        o_ref[0, 0, pl.ds(j, TT, stride=PITCH), :] = zero


def _relayout(proj, lora_w, lora_b, bsz, t_len):
    g = bsz // PER
    proj4 = proj.reshape(g, PER, t_len, L1_COLS)
    lora0 = 4 * C_WIDTH // LANES
    lora_ix = lambda gi, ti, q: (jnp.clip(q - 3, 0, 3), 0, 0)
    out = pl.pallas_call(
        _relayout_body,
        grid=(g, t_len // TT, N_LANE_ARRAYS),
        in_specs=[
            pl.BlockSpec((1, PER, TT, C_WIDTH), lambda gi, ti, q: (gi, 0, ti, jnp.minimum(q, 2))),
            pl.BlockSpec((1, PER, TT, LANES), lambda gi, ti, q: (gi, 0, ti, lora0 + jnp.clip(q - 3, 0, 3))),
            pl.BlockSpec((1, LORA_PAD, C_WIDTH), lora_ix),
            pl.BlockSpec((1, 1, C_WIDTH), lora_ix),
        ],
        out_specs=pl.BlockSpec((1, 1, TT * PITCH, LANES), lambda gi, ti, q: (q, gi, ti, 0)),
        out_shape=jax.ShapeDtypeStruct((N_LANE_ARRAYS, g, t_len * PITCH, LANES), F32),
        scratch_shapes=[pltpu.VMEM((LANES * PITCH, LANES), F32), pltpu.VMEM((TT, C_WIDTH), F32)],
        compiler_params=_cparams("parallel", "parallel", "arbitrary"),
        name="lane_relayout",
    )(proj4, proj4, lora_w, lora_b)
    return out.reshape(N_LANE_ARRAYS, g, t_len, PITCH, LANES)


def _lane_tile(p):
    return jnp.tile(p.reshape(C_HEADS, C_HEAD).T, (1, PER))


def _scan_body(*refs, tc, reverse, has_prev):
    r_ref, k_ref, v_ref, w_ref, a_ref, kkp_ref, kap_ref, rkp_ref, lnw_ref, lnb_ref, s0_ref = refs[:11]
    rest = refs[11:]
    if has_prev:
        prev_ref, rest = rest[0], rest[1:]
    o_ref, sfin_ref, s_scr, kka_scr, kz_scr, nkk_scr, y_scr = rest
    c = pl.program_id(1)

    @pl.when(c == 0)
    def _():
        s_scr[...] = s0_ref[0]

    used = lambda ref: ref[0, 0, :, :C_HEAD, :]
    k = used(k_ref)
    a = used(a_ref)
    kk = k * kkp_ref[...]
    kk = kk * lax.rsqrt(jnp.maximum(jnp.sum(kk * kk, axis=1, keepdims=True), 1e-12))
    kz = k * (1.0 + (a - 1.0) * kap_ref[...])
    kz_scr[...] = kz
    kka_scr[...] = kk * a
    nkk_scr[...] = -kk
    bonus = jnp.sum(used(r_ref) * kz * rkp_ref[...], axis=1, keepdims=True) * used(v_ref)

    first = tc - 1 if reverse else 0
    sa0 = jnp.zeros((C_HEAD, LANES), F32)
    for j in range(C_HEAD):
        sa0 = sa0 + s_scr[j] * nkk_scr[first, j:j + 1, :]

    def step(i, sa):
        t = (tc - 1 - i) if reverse else i
        t_next = jnp.maximum(t - 1, 0) if reverse else jnp.minimum(t + 1, tc - 1)
        v = v_ref[0, 0, t, :C_HEAD, :]
        y = jnp.zeros((C_HEAD, LANES), F32)
        sa_next = jnp.zeros((C_HEAD, LANES), F32)
        for j in range(C_HEAD):
            s = s_scr[j] * w_ref[0, 0, t, j:j + 1, :] + sa * kka_scr[t, j:j + 1, :] + v * kz_scr[t, j:j + 1, :]
            s_scr[j] = s
            y = y + s * r_ref[0, 0, t, j:j + 1, :]
            sa_next = sa_next + s * nkk_scr[t_next, j:j + 1, :]
        y_scr[t] = y
        return sa_next

    lax.fori_loop(0, tc, step, sa0)

    y = y_scr[...]
    dev = y - jnp.mean(y, axis=1, keepdims=True)
    var = jnp.mean(dev * dev, axis=1, keepdims=True)
    out = dev * lax.rsqrt(var + GN_EPS) * lnw_ref[...] + lnb_ref[...] + bonus
    if has_prev:
        out = out + prev_ref[0, :, :C_HEAD, :]
    o_ref[0, :, :C_HEAD, :] = out
    o_ref[0, :, C_HEAD:, :] = jnp.zeros((tc, PITCH - C_HEAD, LANES), F32)

    @pl.when(c == pl.num_programs(1) - 1)
    def _():
        sfin_ref[0] = s_scr[...]


def _scan(z7, z, params, s0, prev, *, reverse, tc):
    _, g, t_len = z7.shape[:3]
    n_c = t_len // tc
    tix = (lambda ci: n_c - 1 - ci) if reverse else (lambda ci: ci)
    qblk = lambda q: pl.BlockSpec((1, 1, tc, PITCH, LANES), lambda gi, ci: (q, gi, tix(ci), 0, 0))
    tile = pl.BlockSpec((C_HEAD, LANES), lambda gi, ci: (0, 0))
    st = pl.BlockSpec((1, C_HEAD, C_HEAD, LANES), lambda gi, ci: (gi, 0, 0, 0))
    oblk = pl.BlockSpec((1, tc, PITCH, LANES), lambda gi, ci: (gi, tix(ci), 0, 0))
    in_specs = [qblk(0), qblk(1), qblk(2), qblk(3 + z), qblk(5 + z)] + [tile] * len(params) + [st]
    args = [z7] * 5 + list(params) + [s0]
    if prev is not None:
        in_specs.append(oblk)
        args.append(prev)
    chunk = pltpu.VMEM((tc, C_HEAD, LANES), F32)
    return pl.pallas_call(
        functools.partial(_scan_body, tc=tc, reverse=reverse, has_prev=prev is not None),
        grid=(g, n_c),
        in_specs=in_specs,
        out_specs=[oblk, st],
        out_shape=[jax.ShapeDtypeStruct((g, t_len, PITCH, LANES), F32), jax.ShapeDtypeStruct(s0.shape, F32)],
        scratch_shapes=[pltpu.VMEM((C_HEAD, C_HEAD, LANES), F32)] + [chunk] * 4,
        compiler_params=_cparams("parallel", "arbitrary"),
        name="wkv_scan",
    )(*args)


def _state_to_lanes(s):
    per = LANES // C_HEADS
    bsz = s.shape[0]
    s = s.reshape(bsz // per, per, C_HEADS, C_HEAD, C_HEAD)
    return s.transpose(0, 4, 3, 1, 2).reshape(bsz // per, C_HEAD, C_HEAD, LANES)


def _state_from_lanes(s):
    per = LANES // C_HEADS
    g = s.shape[0]
    s = s.reshape(g, C_HEAD, C_HEAD, per, C_HEADS)
    return s.transpose(0, 3, 4, 2, 1).reshape(g * per, C_HEADS, C_HEAD, C_HEAD)


def _rwkv_out_body(y_ref, x_ref, gate_ref, w_ref, mg_ref, fg_ref, o_ref, r_scr, y_scr):
    b = pl.program_id(2)

    @pl.when(b == 0)
    def _():
        for i in range(C_HEAD):
            m = y_ref[0, pl.ds(i, TT, stride=PITCH), :]
            r_scr[pl.ds(i, LANES, stride=PITCH), :] = m.T
        for bb in range(PER):
            for c in range(C_WIDTH // LANES):
                p0 = bb * C_HEADS + 2 * c
                pair = jnp.concatenate([r_scr[pl.ds(p0 * PITCH, C_HEAD), :],
                                        r_scr[pl.ds((p0 + 1) * PITCH, C_HEAD), :]], axis=0)
                y_scr[bb, :, c * LANES:(c + 1) * LANES] = pair.T

    yg = (y_scr[b] * _silu(gate_ref[0, 0])).astype(BF16)
    xn = x_ref[0, 0] + mg_ref[0] * _dot(yg, w_ref[...])
    o_ref[0, 0] = _rms(xn, fg_ref[...])


def _rwkv_out(x, y, proj, w, gate, final_g, *, bsz, t_len):
    d = x.shape[1]
    g = bsz // PER
    n_mod = gate.shape[0]
    tok = lambda gi, ti, b: (gi, b, ti, 0)
    const = lambda gi, ti, b: (0, 0)
    out = pl.pallas_call(
        _rwkv_out_body,
        grid=(g, t_len // TT, PER),
        in_specs=[
            pl.BlockSpec((1, TT * PITCH, LANES), lambda gi, ti, b: (gi, ti, 0)),
            pl.BlockSpec((1, 1, TT, d), tok),
            pl.BlockSpec((1, 1, TT, C_WIDTH), lambda gi, ti, b: (gi, b, ti, 3)),
            pl.BlockSpec((C_WIDTH, d), const, pipeline_mode=pl.Buffered(1)),
            pl.BlockSpec((1, 1, d), lambda gi, ti, b: ((gi * PER + b) % n_mod, 0, 0)),
            pl.BlockSpec((1, d), const),
        ],
        out_specs=pl.BlockSpec((1, 1, TT, d), tok),
        out_shape=jax.ShapeDtypeStruct((g, PER, t_len, d), F32),
        scratch_shapes=[pltpu.VMEM((LANES * PITCH, LANES), F32), pltpu.VMEM((PER, TT, C_WIDTH), F32)],
        compiler_params=_cparams("parallel", "parallel", "arbitrary"),
        name="rwkv_out",
    )(y.reshape(g, t_len * PITCH, LANES), x.reshape(g, PER, t_len, d), proj.reshape(g, PER, t_len, L1_COLS),
      w, gate, final_g)
    return out.reshape(bsz * t_len, d)


def _rope_tables(t_len):
    pos = jnp.arange(t_len)
    half = 16
    freqs = ROPE_BASE ** (-jnp.arange(half, dtype=F32) / half)
    ang_r = (pos // GRID_W).astype(F32)[:, None] * freqs
    ang_c = (pos % GRID_W).astype(F32)[:, None] * freqs
    cos = jnp.concatenate([jnp.cos(ang_r)] * 2 + [jnp.cos(ang_c)] * 2, axis=-1)
    sin = jnp.concatenate([-jnp.sin(ang_r), jnp.sin(ang_r), -jnp.sin(ang_c), jnp.sin(ang_c)], axis=-1)
    return jnp.tile(cos, (1, 2)), jnp.tile(sin, (1, 2))


def _l0_w_in_layout(w):
    d = w.shape[0]
    aq, ak, av, cq, ckv, kpe, gate = jnp.split(w.astype(BF16), [1024, 2048, 3072, 3584, 3840, 3904], axis=1)
    pad = jnp.zeros((d, L0_COLS - 6016), BF16)
    return jnp.concatenate([gate, aq, ak, av, cq, ckv, kpe, kpe, pad], axis=1)


def _l1_cols_layout(w):
    main, wd, ad = jnp.split(w, [4 * C_WIDTH, 4 * C_WIDTH + 2 * C_LORA], axis=1)
    pad = jnp.zeros((w.shape[0], LORA_PAD - C_LORA), w.dtype)
    parts = [main]
    for seg in (wd, ad):
        for z in range(2):
            parts += [seg[:, z * C_LORA:(z + 1) * C_LORA], pad]
    return jnp.concatenate(parts, axis=1)


def _layer0(x, mods, rows_per_mod, seq_len, weights, ctx, tables):
    w_in, w_out, diff_lambda, subln_g, q_norm_g, w_uq, kv_norm_g, w_ukv = weights
    shift, scale, gate, norm_g = mods
    m = x.shape[0]
    bsz = m // seq_len
    proj = _inproj(x, shift, scale, norm_g, w_in, None, rows_per_mod=rows_per_mod, seq_len=seq_len, tm=1024, tn=1024)
    q_b, ckv_n, kv_b = _mla_prep(proj, q_norm_g, kv_norm_g, w_uq, w_ukv, tm=512)
    proj3 = proj.reshape(bsz, seq_len, L0_COLS)
    q_b3 = q_b.reshape(bsz, seq_len, -1)
    kv_b3 = kv_b.reshape(bsz, seq_len, -1)
    if ctx is None:
        ya = _diff_attn(proj3, None, None, None, None, diff_lambda, subln_g, tq=256)
        yb = _mla_attn(proj3, q_b3, kv_b3, None, None, None, None, tq=256)
    else:
        k_ctx, v_ctx, ckv_ctx, kpe_ctx = ctx
        p_len = k_ctx.shape[1]
        cos, sin = tables
        kv_ctx = _matmul(ckv_ctx.reshape(bsz * p_len, B_KV_LORA), w_ukv, tm=512).reshape(bsz, p_len, -1)
        ya = _diff_attn(proj3, k_ctx.reshape(bsz, p_len, -1), v_ctx.reshape(bsz, p_len, -1), cos, sin,
                        diff_lambda, subln_g, tq=256)
        yb = _mla_attn(proj3, q_b3, kv_b3, kv_ctx, jnp.concatenate([kpe_ctx, kpe_ctx], axis=-1), cos, sin, tq=256)
    x_new = _outproj(x, ya.reshape(m, -1), yb.reshape(m, -1), w_out, gate, rows_per_mod=rows_per_mod, tm=512)
    return x_new, proj, ckv_n


def _layer1(x, mods, rows_per_mod, seq_len, weights, states, final_g):
    w_in, w_out, mu, lora_w, lora_b, lane_params = weights
    shift, scale, gate, norm_g = mods
    m = x.shape[0]
    bsz = m // seq_len
    proj = _inproj(x, shift, scale, norm_g, w_in, mu, rows_per_mod=rows_per_mod, seq_len=seq_len, tm=1024, tn=512)
    z7 = _relayout(proj, lora_w, lora_b, bsz, seq_len)
    y, finals = None, []
    for z in range(2):
        y, s_fin = _scan(z7, z, lane_params, states[z], y, reverse=(z == 1), tc=32)
        finals.append(_state_from_lanes(s_fin))
    out = _rwkv_out(x, y, proj, w_out, gate, final_g, bsz=bsz, t_len=seq_len)
    return out, finals


def kernel(x_prompt, x_sample, cache_l0_a_k, cache_l0_a_v, cache_l0_mla_ckv, cache_l0_mla_kpe, state_l1_fwd, state_l1_bwd, c, c_ctx, mod_w, mod_b, norm_g, final_norm_g, l0_w_in, l0_w_out, l0_diff_lambda, l0_subln_g, l0_q_norm_g, l0_w_uq, l0_kv_norm_g, l0_w_ukv, l1_w_in, l1_w_out, l1_mu, l1_w0, l1_w2, l1_a0, l1_a2, l1_k_k, l1_k_a, l1_r_k, l1_ln_w, l1_ln_b):
    d = D_MODEL
    bp, tp, _ = x_prompt.shape
    bs, ts, _ = x_sample.shape

    cond = jnp.concatenate([c_ctx[None, :], c, jnp.zeros((MOD_ROWS - 1 - bs, d), F32)], axis=0)
    mods = _adaln(cond, mod_w, mod_b)

    def mod_rows(layer, lo, hi):
        rows = mods[layer, lo:hi]
        shift, scale, gate = (rows[:, i * d:(i + 1) * d].reshape(hi - lo, 1, d) for i in range(3))
        return shift, scale, gate, norm_g[layer].reshape(1, d)

    w_uq = l0_w_uq.reshape(B_Q_LORA, B_HEADS, B_NOPE + B_ROPE)
    w_uq = jnp.concatenate([w_uq[:, :, :B_NOPE].reshape(B_Q_LORA, -1), w_uq[:, :, B_NOPE:].reshape(B_Q_LORA, -1)],
                           axis=1).astype(BF16)
    l0_weights = (_l0_w_in_layout(l0_w_in), l0_w_out.astype(BF16), l0_diff_lambda, l0_subln_g.reshape(1, -1),
                  l0_q_norm_g.reshape(1, -1), w_uq, l0_kv_norm_g.reshape(1, -1), l0_w_ukv.astype(BF16))
    lora_pad = jnp.zeros((2, LORA_PAD - C_LORA, C_WIDTH), F32)
    lora_w = jnp.concatenate([jnp.concatenate([l1_w2, lora_pad], axis=1),
                              jnp.concatenate([l1_a2, lora_pad], axis=1)], axis=0).astype(BF16)
    lora_b = jnp.concatenate([l1_w0, l1_a0], axis=0).reshape(4, 1, C_WIDTH)
    lane_params = [_lane_tile(p) for p in (l1_k_k, l1_k_a, l1_r_k, l1_ln_w, l1_ln_b)]
    l1_weights = (_l1_cols_layout(l1_w_in.astype(BF16)), l1_w_out.astype(BF16), _l1_cols_layout(l1_mu),
                  lora_w, lora_b, lane_params)
    tables = _rope_tables(ts)
    final_g = final_norm_g.reshape(1, d)

    xp = x_prompt.reshape(bp * tp, d)
    xs = x_sample.reshape(bs * ts, d)
    ctx0 = (cache_l0_a_k, cache_l0_a_v, cache_l0_mla_ckv, cache_l0_mla_kpe)

    xp1, proj_p, ckv_p = _layer0(xp, mod_rows(0, 0, 1), bp * tp, tp, l0_weights, None, None)
    xs1, _, _ = _layer0(xs, mod_rows(0, 1, 1 + bs), ts, ts, l0_weights, ctx0, tables)

    zero_state = jnp.zeros((bp // PER, C_HEAD, C_HEAD, LANES), F32)
    y_prompt, finals = _layer1(xp1, mod_rows(1, 0, 1), bp * tp, tp, l1_weights, (zero_state, zero_state), final_g)
    y_sample, _ = _layer1(xs1, mod_rows(1, 1, 1 + bs), ts, ts, l1_weights,
                          (_state_to_lanes(state_l1_fwd), _state_to_lanes(state_l1_bwd)), final_g)

    new_a_k = proj_p[:, 3072:4096].reshape(bp, tp, A_HEADS, 2, A_QK_DIM)
    new_a_v = proj_p[:, 4096:5120].reshape(bp, tp, A_HEADS, A_V_DIM)
    new_ckv = ckv_p.reshape(bp, tp, B_KV_LORA)
    new_kpe = proj_p[:, 5888:5888 + B_ROPE].reshape(bp, tp, B_ROPE)
    return (y_prompt.reshape(bp, tp, d), y_sample.reshape(bs, ts, d), new_a_k, new_a_v, new_ckv, new_kpe,
            finals[0], finals[1])
```

```python
import functools
import math

import jax
import jax.numpy as jnp
from jax import lax
from jax.experimental import pallas as pl
from jax.experimental.pallas import tpu as pltpu

F32 = jnp.float32
BF16 = jnp.bfloat16

D_MODEL = 2048
GRID_W = 64
ROPE_BASE = 10000.0
NORM_EPS = 1e-6
GN_EPS = 64e-5

A_HEADS = 8
A_QK_DIM = 64
A_V_DIM = 128
B_HEADS = 8
B_NOPE = 128
B_ROPE = 64
B_V = 128
B_Q_LORA = 512
B_KV_LORA = 256
AB_WIDTH = A_HEADS * A_V_DIM + B_HEADS * B_V
C_HEAD = 64
C_HEADS = D_MODEL // C_HEAD
C_WIDTH = C_HEADS * C_HEAD
C_LORA = 96
LAM_INIT_L0 = 0.8 - 0.6 * math.exp(-0.3 * 0)

LANES = 128
SUBLANES = 8
LORA_PAD = 128
MOD_ROWS = 8
VMEM_LIMIT = 56 * 1024 * 1024

L0_COLS = 6144
L0_GATE_A, L0_GATE_B, L0_AQ, L0_AK, L0_AV = 0, 1, 2, 3, 4
L0_CQ = 10
L0_CKV = 22
L0_KPE = 46
L1_COLS = 4 * C_WIDTH + 4 * LORA_PAD
PER = LANES // C_HEADS
PITCH = 72
TT = LANES
N_LANE_ARRAYS = 7


def _cparams(*sem):
    return pltpu.CompilerParams(dimension_semantics=sem, vmem_limit_bytes=VMEM_LIMIT)


def _silu(x):
    return x / (1.0 + jnp.exp(-x))


def _dot(a, b):
    return jnp.dot(a, b, preferred_element_type=F32)


def _dot_nt(a, b):
    return lax.dot_general(a, b, (((1,), (1,)), ((), ())), preferred_element_type=F32)


def _rope(x, cos, sin):
    lane = lax.broadcasted_iota(jnp.int32, x.shape, 1)
    low = (lane % 32) < 16
    rot = jnp.where(low, pltpu.roll(x, LANES - 16, 1), pltpu.roll(x, 16, 1))
    return x * cos + rot * sin


def _adaln_body(c_ref, w_ref, b_ref, o_ref):
    s = _silu(c_ref[...]).astype(BF16)
    o_ref[0] = _dot(s, w_ref[0].astype(BF16)) + b_ref[0]


def _adaln(cond, mod_w, mod_b):
    depth, d, n = mod_w.shape
    tn = 1024
    return pl.pallas_call(
        _adaln_body,
        grid=(depth, n // tn),
        in_specs=[
            pl.BlockSpec((MOD_ROWS, d), lambda l, j: (0, 0)),
            pl.BlockSpec((1, d, tn), lambda l, j: (l, 0, j)),
            pl.BlockSpec((1, 1, tn), lambda l, j: (l, 0, j)),
        ],
        out_specs=pl.BlockSpec((1, MOD_ROWS, tn), lambda l, j: (l, 0, j)),
        out_shape=jax.ShapeDtypeStruct((depth, MOD_ROWS, n), F32),
        compiler_params=_cparams("parallel", "parallel"),
        name="adaln",
    )(cond, mod_w, mod_b.reshape(depth, 1, n))


def _inproj_body(x_ref, sh_ref, sc_ref, g_ref, w_ref, *rest, seq_len):
    if seq_len:
        mu_ref, o_ref, h_ref, p_scr = rest
    else:
        o_ref, h_ref = rest

    @pl.when(pl.program_id(1) == 0)
    def _():
        x = x_ref[...]
        y = x * lax.rsqrt(jnp.mean(x * x, axis=-1, keepdims=True) + NORM_EPS) * g_ref[...]
        h_ref[...] = (y * (1.0 + sc_ref[0]) + sh_ref[0]).astype(BF16)

    if not seq_len:
        o_ref[...] = _dot(h_ref[...], w_ref[...])
        return
    tm, tn = o_ref.shape
    zeros = jnp.zeros((SUBLANES, tn), F32)
    mu0, mu1 = mu_ref[0:1, :], mu_ref[1:2, :]
    keep = 1.0 - mu0 - mu1
    p_scr[pl.ds(0, SUBLANES), :] = zeros
    p_scr[pl.ds(SUBLANES, tm), :] = _dot(h_ref[...], w_ref[...])
    p_scr[pl.ds(SUBLANES + tm, SUBLANES), :] = zeros
    at = lambda r0, n, off: p_scr[pl.ds(SUBLANES + r0 + off, n), :]
    o_ref[...] = at(0, tm, 0) * keep + at(0, tm, -1) * mu0 + at(0, tm, 1) * mu1
    edge = lax.broadcasted_iota(jnp.int32, (2 * SUBLANES, 1), 0)
    for s in range(1, tm // seq_len):
        r0 = s * seq_len - SUBLANES
        prev = jnp.where(edge == SUBLANES, 0.0, at(r0, 2 * SUBLANES, -1))
        nxt = jnp.where(edge == SUBLANES - 1, 0.0, at(r0, 2 * SUBLANES, 1))
        o_ref[pl.ds(r0, 2 * SUBLANES), :] = at(r0, 2 * SUBLANES, 0) * keep + prev * mu0 + nxt * mu1


def _inproj(x, shift, scale, g, w, mu, *, rows_per_mod, seq_len, tm, tn):
    m, d = x.shape
    n = w.shape[1]
    mod_map = lambda i, j: ((i * tm) // rows_per_mod, 0, 0)
    in_specs = [
        pl.BlockSpec((tm, d), lambda i, j: (i, 0)),
        pl.BlockSpec((1, 1, d), mod_map),
        pl.BlockSpec((1, 1, d), mod_map),
        pl.BlockSpec((1, d), lambda i, j: (0, 0)),
        pl.BlockSpec((d, tn), lambda i, j: (0, j)),
    ]
    args = [x, shift, scale, g, w]
    if mu is not None:
        in_specs.append(pl.BlockSpec((2, tn), lambda i, j: (0, j)))
        args.append(mu)
    scratch = [pltpu.VMEM((tm, d), BF16)]
    if mu is not None:
        scratch.append(pltpu.VMEM((tm + 2 * SUBLANES, tn), F32))
    return pl.pallas_call(
        functools.partial(_inproj_body, seq_len=seq_len if mu is not None else 0),
        grid=(m // tm, n // tn),
        in_specs=in_specs,
        out_specs=pl.BlockSpec((tm, tn), lambda i, j: (i, j)),
        out_shape=jax.ShapeDtypeStruct((m, n), F32),
        scratch_shapes=scratch,
        compiler_params=_cparams("parallel", "arbitrary"),
        name="inproj",
    )(*args)


def _rms(x, g):
    return x * lax.rsqrt(jnp.mean(x * x, axis=-1, keepdims=True) + NORM_EPS) * g


def _mla_prep_body(cq_ref, ckv_ref, qg_ref, kg_ref, wuq_ref, wukv_ref, qb_ref, ckvn_ref, kvb_ref):
    qb_ref[...] = _dot(_rms(cq_ref[...], qg_ref[...]).astype(BF16), wuq_ref[...])
    ckv = _rms(ckv_ref[...], kg_ref[...])
    ckvn_ref[...] = ckv
    kvb_ref[...] = _dot(ckv.astype(BF16), wukv_ref[...])


def _mla_prep(proj, q_norm_g, kv_norm_g, w_uq, w_ukv, *, tm):
    m = proj.shape[0]
    nq, nkv = w_uq.shape[1], w_ukv.shape[1]
    const = lambda i: (0, 0)
    return pl.pallas_call(
        _mla_prep_body,
        grid=(m // tm,),
        in_specs=[
            pl.BlockSpec((tm, B_Q_LORA), lambda i: (i, L0_CQ)),
            pl.BlockSpec((tm, B_KV_LORA), lambda i: (i, L0_CKV)),
            pl.BlockSpec((1, B_Q_LORA), const),
            pl.BlockSpec((1, B_KV_LORA), const),
            pl.BlockSpec((B_Q_LORA, nq), const),
            pl.BlockSpec((B_KV_LORA, nkv), const),
        ],
        out_specs=[
            pl.BlockSpec((tm, nq), lambda i: (i, 0)),
            pl.BlockSpec((tm, B_KV_LORA), lambda i: (i, 0)),
            pl.BlockSpec((tm, nkv), lambda i: (i, 0)),
        ],
        out_shape=[
            jax.ShapeDtypeStruct((m, nq), F32),
            jax.ShapeDtypeStruct((m, B_KV_LORA), F32),
            jax.ShapeDtypeStruct((m, nkv), F32),
        ],
        compiler_params=_cparams("parallel"),
        name="mla_prep",
    )(proj, proj, q_norm_g, kv_norm_g, w_uq, w_ukv)


def _matmul_body(x_ref, w_ref, o_ref):
    o_ref[...] = _dot(x_ref[...].astype(BF16), w_ref[...])


def _matmul(x, w, *, tm):
    m, k = x.shape
    n = w.shape[1]
    return pl.pallas_call(
        _matmul_body,
        grid=(m // tm,),
        in_specs=[pl.BlockSpec((tm, k), lambda i: (i, 0)), pl.BlockSpec((k, n), lambda i: (0, 0))],
        out_specs=pl.BlockSpec((tm, n), lambda i: (i, 0)),
        out_shape=jax.ShapeDtypeStruct((m, n), F32),
        compiler_params=_cparams("parallel"),
        name="matmul",
    )(x, w)


def _softmax_parts(scores):
    m = functools.reduce(jnp.maximum, [jnp.max(s, axis=-1, keepdims=True) for s in scores])
    ps = [jnp.exp(s - m) for s in scores]
    denom = functools.reduce(jnp.add, [jnp.sum(p, axis=-1, keepdims=True) for p in ps])
    return ps, denom


def _diff_attn_body(*refs, dec, tq):
    if dec:
        (q_ref, k_ref, v_ref, kc_ref, vc_ref, gate_ref, cos_ref, sin_ref, lam_ref, sg_ref,
         o_ref, kb, vb, kcb, vcb) = refs
    else:
        q_ref, k_ref, v_ref, gate_ref, lam_ref, sg_ref, o_ref, kb, vb = refs
    qi = pl.program_id(1)

    @pl.when(qi == 0)
    def _():
        for h in range(A_HEADS):
            hs = slice(h * LANES, (h + 1) * LANES)
            kt = k_ref[0, :, hs]
            if dec:
                kt = _rope(kt, cos_ref[...], sin_ref[...])
                kcb[:, hs] = kc_ref[0, :, hs].astype(BF16)
                vcb[:, hs] = vc_ref[0, :, hs].astype(BF16)
            kb[:, hs] = kt.astype(BF16)
            vb[:, hs] = v_ref[0, :, hs].astype(BF16)

    lp = lam_ref[...]
    lam = (jnp.exp(jnp.sum(lp[0:1] * lp[1:2], keepdims=True))
           - jnp.exp(jnp.sum(lp[2:3] * lp[3:4], keepdims=True)) + LAM_INIT_L0)
    if dec:
        row0 = pl.multiple_of(qi * tq, tq)
        cq, sq = cos_ref[pl.ds(row0, tq), :], sin_ref[pl.ds(row0, tq), :]
    first = lax.broadcasted_iota(jnp.int32, (1, LANES), 1) < A_QK_DIM
    for h in range(A_HEADS):
        hs = slice(h * LANES, (h + 1) * LANES)
        qh = q_ref[0, :, hs]
        if dec:
            qh = _rope(qh, cq, sq)
        qh = qh * (A_QK_DIM ** -0.5)
        q1 = jnp.where(first, qh, 0.0).astype(BF16)
        q2 = jnp.where(first, 0.0, qh).astype(BF16)
        keys = [kb[:, hs]] + ([kcb[:, hs]] if dec else [])
        vals = [vb[:, hs]] + ([vcb[:, hs]] if dec else [])
        p1, l1 = _softmax_parts([_dot_nt(q1, kp) for kp in keys])
        p2, l2 = _softmax_parts([_dot_nt(q2, kp) for kp in keys])
        o1 = functools.reduce(jnp.add, [_dot(x.astype(BF16), vp) for x, vp in zip(p1, vals)])
        o2 = functools.reduce(jnp.add, [_dot(x.astype(BF16), vp) for x, vp in zip(p2, vals)])
        o = o1 * (1.0 / l1) - o2 * (lam / l2)
        o = _rms(o, sg_ref[...]) * (1.0 - LAM_INIT_L0)
        o_ref[0, :, hs] = (o * _silu(gate_ref[0, :, hs])).astype(BF16)


def _diff_attn(proj, ctx_k, ctx_v, cos, sin, diff_lambda, subln_g, *, tq):
    bsz, t_len, _ = proj.shape
    dec = ctx_k is not None
    w = A_HEADS * LANES
    full = lambda col: pl.BlockSpec((1, t_len, w), lambda b, i: (b, 0, col))
    in_specs = [pl.BlockSpec((1, tq, w), lambda b, i: (b, i, L0_AQ)), full(L0_AK), full(L0_AV)]
    args = [proj, proj, proj]
    scratch = [pltpu.VMEM((t_len, w), BF16), pltpu.VMEM((t_len, w), BF16)]
    if dec:
        p_len = ctx_k.shape[1]
        in_specs += [pl.BlockSpec((1, p_len, w), lambda b, i: (b, 0, 0))] * 2
        args += [ctx_k, ctx_v]
        scratch += [pltpu.VMEM((p_len, w), BF16), pltpu.VMEM((p_len, w), BF16)]
    in_specs.append(pl.BlockSpec((1, tq, w), lambda b, i: (b, i, L0_GATE_A)))
    args.append(proj)
    if dec:
        in_specs += [pl.BlockSpec((t_len, LANES), lambda b, i: (0, 0))] * 2
        args += [cos, sin]
    in_specs += [pl.BlockSpec((4, A_QK_DIM), lambda b, i: (0, 0)),
                 pl.BlockSpec((1, A_V_DIM), lambda b, i: (0, 0))]
    args += [diff_lambda, subln_g]
    return pl.pallas_call(
        functools.partial(_diff_attn_body, dec=dec, tq=tq),
        grid=(bsz, t_len // tq),
        in_specs=in_specs,
        out_specs=pl.BlockSpec((1, tq, w), lambda b, i: (b, i, 0)),
        out_shape=jax.ShapeDtypeStruct((bsz, t_len, w), BF16),
        scratch_shapes=scratch,
        compiler_params=_cparams("parallel", "arbitrary"),
        name="diff_attn",
    )(*args)


def _mla_attn_body(*refs, dec, tq):
    if dec:
        (qn_ref, qp_ref, kv_ref, kpe_ref, kvc_ref, kpec_ref, gate_ref, cos_ref, sin_ref,
         o_ref, kb, vb, kcb, vcb) = refs
    else:
        qn_ref, qp_ref, kv_ref, kpe_ref, gate_ref, o_ref, kb, vb = refs
    qi = pl.program_id(1)

    def stage(src_ref, kpe, k_dst, v_dst):
        for h in range(B_HEADS):
            k_dst[:, 2 * h * LANES:(2 * h + 1) * LANES] = src_ref[0, :, 2 * h * LANES:(2 * h + 1) * LANES].astype(BF16)
            k_dst[:, (2 * h + 1) * LANES:(2 * h + 2) * LANES] = kpe
            v_dst[:, h * LANES:(h + 1) * LANES] = src_ref[0, :, (2 * h + 1) * LANES:(2 * h + 2) * LANES].astype(BF16)

    @pl.when(qi == 0)
    def _():
        kp = kpe_ref[0]
        if dec:
            kp = _rope(kp, cos_ref[...], sin_ref[...])
            stage(kvc_ref, kpec_ref[0].astype(BF16), kcb, vcb)
        stage(kv_ref, kp.astype(BF16), kb, vb)

    if dec:
        row0 = pl.multiple_of(qi * tq, tq)
        cq, sq = cos_ref[pl.ds(row0, tq), :], sin_ref[pl.ds(row0, tq), :]
    scale = (B_NOPE + B_ROPE) ** -0.5
    first = lax.broadcasted_iota(jnp.int32, (1, LANES), 1) < B_ROPE
    for h in range(B_HEADS):
        hs = slice(h * LANES, (h + 1) * LANES)
        if h % 2 == 0:
            pair = qp_ref[0, :, (h // 2) * LANES:(h // 2 + 1) * LANES]
            if dec:
                pair = _rope(pair, cq, sq)
            pair = pair * scale
        qp = (jnp.where(first, pair, 0.0) if h % 2 == 0 else jnp.where(first, 0.0, pair)).astype(BF16)
        qn = (qn_ref[0, :, hs] * scale).astype(BF16)
        q = jnp.concatenate([qn, qp], axis=1)
        ks = slice(2 * h * LANES, (2 * h + 2) * LANES)
        scores = [_dot_nt(q, kb[:, ks])]
        vals = [vb[:, hs]]
        if dec:
            scores.append(_dot_nt(q, kcb[:, ks]))
            vals.append(vcb[:, hs])
        ps, denom = _softmax_parts(scores)
        o = functools.reduce(jnp.add, [_dot(p.astype(BF16), vp) for p, vp in zip(ps, vals)]) * (1.0 / denom)
        o_ref[0, :, hs] = (o * _silu(gate_ref[0, :, hs])).astype(BF16)


def _mla_attn(proj, q_b, kv_b, ctx_kv, ctx_kpe, cos, sin, *, tq):
    bsz, t_len, _ = proj.shape
    dec = ctx_kv is not None
    w = B_HEADS * LANES
    nkv = kv_b.shape[2]
    in_specs = [
        pl.BlockSpec((1, tq, w), lambda b, i: (b, i, 0)),
        pl.BlockSpec((1, tq, B_HEADS * B_ROPE), lambda b, i: (b, i, 2)),
        pl.BlockSpec((1, t_len, nkv), lambda b, i: (b, 0, 0)),
        pl.BlockSpec((1, t_len, LANES), lambda b, i: (b, 0, L0_KPE)),
    ]
    args = [q_b, q_b, kv_b, proj]
    scratch = [pltpu.VMEM((t_len, 2 * w), BF16), pltpu.VMEM((t_len, w), BF16)]
    if dec:
        p_len = ctx_kv.shape[1]
        in_specs += [pl.BlockSpec((1, p_len, nkv), lambda b, i: (b, 0, 0)),
                     pl.BlockSpec((1, p_len, LANES), lambda b, i: (b, 0, 0))]
        args += [ctx_kv, ctx_kpe]
        scratch += [pltpu.VMEM((p_len, 2 * w), BF16), pltpu.VMEM((p_len, w), BF16)]
    in_specs.append(pl.BlockSpec((1, tq, w), lambda b, i: (b, i, L0_GATE_B)))
    args.append(proj)
    if dec:
        in_specs += [pl.BlockSpec((t_len, LANES), lambda b, i: (0, 0))] * 2
        args += [cos, sin]
    return pl.pallas_call(
        functools.partial(_mla_attn_body, dec=dec, tq=tq),
        grid=(bsz, t_len // tq),
        in_specs=in_specs,
        out_specs=pl.BlockSpec((1, tq, w), lambda b, i: (b, i, 0)),
        out_shape=jax.ShapeDtypeStruct((bsz, t_len, w), BF16),
        scratch_shapes=scratch,
        compiler_params=_cparams("parallel", "arbitrary"),
        name="mla_attn",
    )(*args)


def _outproj_body(x_ref, ya_ref, yb_ref, w_ref, gate_ref, o_ref):
    half = ya_ref.shape[1]
    acc = _dot(ya_ref[...], w_ref[0:half, :]) + _dot(yb_ref[...], w_ref[half:, :])
    o_ref[...] = x_ref[...] + gate_ref[0] * acc


def _outproj(x, ya, yb, w, gate, *, rows_per_mod, tm):
    m, d = x.shape
    k = w.shape[0]
    return pl.pallas_call(
        _outproj_body,
        grid=(m // tm,),
        in_specs=[
            pl.BlockSpec((tm, d), lambda i: (i, 0)),
            pl.BlockSpec((tm, k // 2), lambda i: (i, 0)),
            pl.BlockSpec((tm, k // 2), lambda i: (i, 0)),
            pl.BlockSpec((k, d), lambda i: (0, 0), pipeline_mode=pl.Buffered(1)),
            pl.BlockSpec((1, 1, d), lambda i: ((i * tm) // rows_per_mod, 0, 0)),
        ],
        out_specs=pl.BlockSpec((tm, d), lambda i: (i, 0)),
        out_shape=jax.ShapeDtypeStruct((m, d), F32),
        compiler_params=_cparams("parallel"),
        name="outproj",
    )(x, ya, yb, w, gate)


def _relayout_body(main_ref, lora_ref, w_ref, bias_ref, o_ref, r_scr, q_scr):
    q = pl.program_id(2)

    def park(tile_of, prepare=None):
        for b in range(PER):
            if prepare is not None:
                prepare(b)
            for c in range(C_WIDTH // LANES):
                at = tile_of(b, c).T
                p0 = b * C_HEADS + 2 * c
                r_scr[pl.ds(p0 * PITCH, C_HEAD), :] = at[:C_HEAD]
                r_scr[pl.ds((p0 + 1) * PITCH, C_HEAD), :] = at[C_HEAD:]

    @pl.when(q < 3)
    def _():
        park(lambda b, c: main_ref[0, b, :, c * LANES:(c + 1) * LANES])

    expanded = lambda b, c: q_scr[:, c * LANES:(c + 1) * LANES]

    @pl.when((q == 3) | (q == 4))
    def _():
        def decay(b):
            u = -(bias_ref[0] + _dot(jnp.tanh(lora_ref[0, b]).astype(BF16), w_ref[0]))
            softplus = jnp.maximum(u, 0.0) + jnp.log(1.0 + jnp.exp(-jnp.abs(u)))
            q_scr[...] = jnp.exp(-jnp.exp(-softplus - 0.5))
        park(expanded, decay)

    @pl.when(q >= 5)
    def _():
        def iclr(b):
            q_scr[...] = 1.0 / (1.0 + jnp.exp(-(bias_ref[0] + _dot(lora_ref[0, b].astype(BF16), w_ref[0]))))
        park(expanded, iclr)

    for j in range(C_HEAD):
        m = r_scr[pl.ds(j, LANES, stride=PITCH), :]
        o_ref[0, 0, pl.ds(j, TT, stride=PITCH), :] = m.T
    zero = jnp.zeros((TT, LANES), F32)
    for j in range(C_HEAD, PITCH):
        o_ref[0, 0, pl.ds(j, TT, stride=PITCH), :] = zero


def _relayout(proj, lora_w, lora_b, bsz, t_len):
    g = bsz // PER
    proj4 = proj.reshape(g, PER, t_len, L1_COLS)
    lora0 = 4 * C_WIDTH // LANES
    lora_ix = lambda gi, ti, q: (jnp.clip(q - 3, 0, 3), 0, 0)
    out = pl.pallas_call(
        _relayout_body,
        grid=(g, t_len // TT, N_LANE_ARRAYS),
        in_specs=[
            pl.BlockSpec((1, PER, TT, C_WIDTH), lambda gi, ti, q: (gi, 0, ti, jnp.minimum(q, 2))),
            pl.BlockSpec((1, PER, TT, LANES), lambda gi, ti, q: (gi, 0, ti, lora0 + jnp.clip(q - 3, 0, 3))),
            pl.BlockSpec((1, LORA_PAD, C_WIDTH), lora_ix),
            pl.BlockSpec((1, 1, C_WIDTH), lora_ix),
        ],
        out_specs=pl.BlockSpec((1, 1, TT * PITCH, LANES), lambda gi, ti, q: (q, gi, ti, 0)),
        out_shape=jax.ShapeDtypeStruct((N_LANE_ARRAYS, g, t_len * PITCH, LANES), F32),
        scratch_shapes=[pltpu.VMEM((LANES * PITCH, LANES), F32), pltpu.VMEM((TT, C_WIDTH), F32)],
        compiler_params=_cparams("parallel", "parallel", "arbitrary"),
        name="lane_relayout",
    )(proj4, proj4, lora_w, lora_b)
    return out.reshape(N_LANE_ARRAYS, g, t_len, PITCH, LANES)


def _lane_tile(p):
    return jnp.tile(p.reshape(C_HEADS, C_HEAD).T, (1, PER))


def _scan_body(*refs, tc, reverse, has_prev):
    r_ref, k_ref, v_ref, w_ref, a_ref, kkp_ref, kap_ref, rkp_ref, lnw_ref, lnb_ref, s0_ref = refs[:11]
    rest = refs[11:]
    if has_prev:
        prev_ref, rest = rest[0], rest[1:]
    o_ref, sfin_ref, s_scr, kka_scr, kz_scr, nkk_scr, y_scr = rest
    c = pl.program_id(1)

    @pl.when(c == 0)
    def _():
        s_scr[...] = s0_ref[0]

    used = lambda ref: ref[0, 0, :, :C_HEAD, :]
    k = used(k_ref)
    a = used(a_ref)
    kk = k * kkp_ref[...]
    kk = kk * lax.rsqrt(jnp.maximum(jnp.sum(kk * kk, axis=1, keepdims=True), 1e-12))
    kz = k * (1.0 + (a - 1.0) * kap_ref[...])
    kz_scr[...] = kz
    kka_scr[...] = kk * a
    nkk_scr[...] = -kk
    bonus = jnp.sum(used(r_ref) * kz * rkp_ref[...], axis=1, keepdims=True) * used(v_ref)

    first = tc - 1 if reverse else 0
    sa0 = jnp.zeros((C_HEAD, LANES), F32)
    for j in range(C_HEAD):
        sa0 = sa0 + s_scr[pl.ds(j * C_HEAD, C_HEAD), :] * nkk_scr[first, j:j + 1, :]

    def step(i, sa):
        t = (tc - 1 - i) if reverse else i
        t_next = jnp.maximum(t - 1, 0) if reverse else jnp.minimum(t + 1, tc - 1)
        v = v_ref[0, 0, t, :C_HEAD, :]
        y = jnp.zeros((C_HEAD, LANES), F32)
        sa_next = jnp.zeros((C_HEAD, LANES), F32)
        for j in range(C_HEAD):
            rows = pl.ds(j * C_HEAD, C_HEAD)
            s = s_scr[rows, :] * w_ref[0, 0, t, j:j + 1, :] + sa * kka_scr[t, j:j + 1, :] + v * kz_scr[t, j:j + 1, :]
            s_scr[rows, :] = s
            y = y + s * r_ref[0, 0, t, j:j + 1, :]
            sa_next = sa_next + s * nkk_scr[t_next, j:j + 1, :]
        y_scr[t] = y
        return sa_next

    lax.fori_loop(0, tc, step, sa0)

    y = y_scr[...]
    dev = y - jnp.mean(y, axis=1, keepdims=True)
    var = jnp.mean(dev * dev, axis=1, keepdims=True)
    out = dev * lax.rsqrt(var + GN_EPS) * lnw_ref[...] + lnb_ref[...] + bonus
    if has_prev:
        out = out + prev_ref[0, :, :C_HEAD, :]
    o_ref[0, :, :C_HEAD, :] = out
    o_ref[0, :, C_HEAD:, :] = jnp.zeros((tc, PITCH - C_HEAD, LANES), F32)

    @pl.when(c == pl.num_programs(1) - 1)
    def _():
        for ip in range(C_HEAD // 2):
            pair = jnp.concatenate([s_scr[pl.ds(2 * ip, C_HEAD, stride=C_HEAD), :],
                                    s_scr[pl.ds(2 * ip + 1, C_HEAD, stride=C_HEAD), :]], axis=0)
            sfin_ref[0, pl.ds(ip, LANES, stride=C_HEAD // 2), :] = pair.T


def _scan(z7, z, params, s0, prev, *, reverse, tc):
    _, g, t_len = z7.shape[:3]
    n_c = t_len // tc
    tix = (lambda ci: n_c - 1 - ci) if reverse else (lambda ci: ci)
    qblk = lambda q: pl.BlockSpec((1, 1, tc, PITCH, LANES), lambda gi, ci: (q, gi, tix(ci), 0, 0))
    tile = pl.BlockSpec((C_HEAD, LANES), lambda gi, ci: (0, 0))
    st = pl.BlockSpec((1, C_HEAD * C_HEAD, LANES), lambda gi, ci: (gi, 0, 0))
    oblk = pl.BlockSpec((1, tc, PITCH, LANES), lambda gi, ci: (gi, tix(ci), 0, 0))
    in_specs = [qblk(0), qblk(1), qblk(2), qblk(3 + z), qblk(5 + z)] + [tile] * len(params) + [st]
    args = [z7] * 5 + list(params) + [s0]
    if prev is not None:
        in_specs.append(oblk)
        args.append(prev)
    chunk = pltpu.VMEM((tc, C_HEAD, LANES), F32)
    return pl.pallas_call(
        functools.partial(_scan_body, tc=tc, reverse=reverse, has_prev=prev is not None),
        grid=(g, n_c),
        in_specs=in_specs,
        out_specs=[oblk, st],
        out_shape=[jax.ShapeDtypeStruct((g, t_len, PITCH, LANES), F32), jax.ShapeDtypeStruct(s0.shape, F32)],
        scratch_shapes=[pltpu.VMEM((C_HEAD * C_HEAD, LANES), F32)] + [chunk] * 4,
        compiler_params=_cparams("parallel", "arbitrary"),
        name="wkv_scan",
    )(*args)


def _state_to_lanes(s):
    per = LANES // C_HEADS
    bsz = s.shape[0]
    s = s.reshape(bsz // per, per, C_HEADS, C_HEAD, C_HEAD)
    return s.transpose(0, 4, 3, 1, 2).reshape(bsz // per, C_HEAD * C_HEAD, LANES)


def _rwkv_out_body(y_ref, x_ref, gate_ref, w_ref, mg_ref, fg_ref, o_ref, r_scr, y_scr):
    b = pl.program_id(2)

    @pl.when(b == 0)
    def _():
        for i in range(C_HEAD):
            m = y_ref[0, pl.ds(i, TT, stride=PITCH), :]
            r_scr[pl.ds(i, LANES, stride=PITCH), :] = m.T
        for bb in range(PER):
            for c in range(C_WIDTH // LANES):
                p0 = bb * C_HEADS + 2 * c
                pair = jnp.concatenate([r_scr[pl.ds(p0 * PITCH, C_HEAD), :],
                                        r_scr[pl.ds((p0 + 1) * PITCH, C_HEAD), :]], axis=0)
                y_scr[bb, :, c * LANES:(c + 1) * LANES] = pair.T

    yg = (y_scr[b] * _silu(gate_ref[0, 0])).astype(BF16)
    xn = x_ref[0, 0] + mg_ref[0] * _dot(yg, w_ref[...])
    o_ref[0, 0] = _rms(xn, fg_ref[...])


def _rwkv_out(x, y, proj, w, gate, final_g, *, bsz, t_len):
    d = x.shape[1]
    g = bsz // PER
    n_mod = gate.shape[0]
    tok = lambda gi, ti, b: (gi, b, ti, 0)
    const = lambda gi, ti, b: (0, 0)
    out = pl.pallas_call(
        _rwkv_out_body,
        grid=(g, t_len // TT, PER),
        in_specs=[
            pl.BlockSpec((1, TT * PITCH, LANES), lambda gi, ti, b: (gi, ti, 0)),
            pl.BlockSpec((1, 1, TT, d), tok),
            pl.BlockSpec((1, 1, TT, C_WIDTH), lambda gi, ti, b: (gi, b, ti, 3)),
            pl.BlockSpec((C_WIDTH, d), const, pipeline_mode=pl.Buffered(1)),
            pl.BlockSpec((1, 1, d), lambda gi, ti, b: ((gi * PER + b) % n_mod, 0, 0)),
            pl.BlockSpec((1, d), const),
        ],
        out_specs=pl.BlockSpec((1, 1, TT, d), tok),
        out_shape=jax.ShapeDtypeStruct((g, PER, t_len, d), F32),
        scratch_shapes=[pltpu.VMEM((LANES * PITCH, LANES), F32), pltpu.VMEM((PER, TT, C_WIDTH), F32)],
        compiler_params=_cparams("parallel", "parallel", "arbitrary"),
        name="rwkv_out",
    )(y.reshape(g, t_len * PITCH, LANES), x.reshape(g, PER, t_len, d), proj.reshape(g, PER, t_len, L1_COLS),
      w, gate, final_g)
    return out.reshape(bsz * t_len, d)


def _rope_tables(t_len):
    pos = jnp.arange(t_len)
    half = 16
    freqs = ROPE_BASE ** (-jnp.arange(half, dtype=F32) / half)
    ang_r = (pos // GRID_W).astype(F32)[:, None] * freqs
    ang_c = (pos % GRID_W).astype(F32)[:, None] * freqs
    cos = jnp.concatenate([jnp.cos(ang_r)] * 2 + [jnp.cos(ang_c)] * 2, axis=-1)
    sin = jnp.concatenate([-jnp.sin(ang_r), jnp.sin(ang_r), -jnp.sin(ang_c), jnp.sin(ang_c)], axis=-1)
    return jnp.tile(cos, (1, 2)), jnp.tile(sin, (1, 2))


def _l0_w_in_layout(w):
    d = w.shape[0]
    aq, ak, av, cq, ckv, kpe, gate = jnp.split(w.astype(BF16), [1024, 2048, 3072, 3584, 3840, 3904], axis=1)
    pad = jnp.zeros((d, L0_COLS - 6016), BF16)
    return jnp.concatenate([gate, aq, ak, av, cq, ckv, kpe, kpe, pad], axis=1)


def _l1_cols_layout(w):
    main, wd, ad = jnp.split(w, [4 * C_WIDTH, 4 * C_WIDTH + 2 * C_LORA], axis=1)
    pad = jnp.zeros((w.shape[0], LORA_PAD - C_LORA), w.dtype)
    parts = [main]
    for seg in (wd, ad):
        for z in range(2):
            parts += [seg[:, z * C_LORA:(z + 1) * C_LORA], pad]
    return jnp.concatenate(parts, axis=1)


def _layer0(x, mods, rows_per_mod, seq_len, weights, ctx, tables):
    w_in, w_out, diff_lambda, subln_g, q_norm_g, w_uq, kv_norm_g, w_ukv = weights
    shift, scale, gate, norm_g = mods
    m = x.shape[0]
    bsz = m // seq_len
    proj = _inproj(x, shift, scale, norm_g, w_in, None, rows_per_mod=rows_per_mod, seq_len=seq_len, tm=1024, tn=1024)
    q_b, ckv_n, kv_b = _mla_prep(proj, q_norm_g, kv_norm_g, w_uq, w_ukv, tm=512)
    proj3 = proj.reshape(bsz, seq_len, L0_COLS)
    q_b3 = q_b.reshape(bsz, seq_len, -1)
    kv_b3 = kv_b.reshape(bsz, seq_len, -1)
    if ctx is None:
        ya = _diff_attn(proj3, None, None, None, None, diff_lambda, subln_g, tq=256)
        yb = _mla_attn(proj3, q_b3, kv_b3, None, None, None, None, tq=256)
    else:
        k_ctx, v_ctx, ckv_ctx, kpe_ctx = ctx
        p_len = k_ctx.shape[1]
        cos, sin = tables
        kv_ctx = _matmul(ckv_ctx.reshape(bsz * p_len, B_KV_LORA), w_ukv, tm=512).reshape(bsz, p_len, -1)
        ya = _diff_attn(proj3, k_ctx.reshape(bsz, p_len, -1), v_ctx.reshape(bsz, p_len, -1), cos, sin,
                        diff_lambda, subln_g, tq=256)
        yb = _mla_attn(proj3, q_b3, kv_b3, kv_ctx, jnp.concatenate([kpe_ctx, kpe_ctx], axis=-1), cos, sin, tq=256)
    x_new = _outproj(x, ya.reshape(m, -1), yb.reshape(m, -1), w_out, gate, rows_per_mod=rows_per_mod, tm=512)
    return x_new, proj, ckv_n


def _layer1(x, mods, rows_per_mod, seq_len, weights, states, final_g):
    w_in, w_out, mu, lora_w, lora_b, lane_params = weights
    shift, scale, gate, norm_g = mods
    m = x.shape[0]
    bsz = m // seq_len
    proj = _inproj(x, shift, scale, norm_g, w_in, mu, rows_per_mod=rows_per_mod, seq_len=seq_len, tm=1024, tn=512)
    z7 = _relayout(proj, lora_w, lora_b, bsz, seq_len)
    y, finals = None, []
    for z in range(2):
        y, s_fin = _scan(z7, z, lane_params, states[z], y, reverse=(z == 1), tc=32)
        finals.append(s_fin.reshape(bsz, C_HEADS, C_HEAD, C_HEAD))
    out = _rwkv_out(x, y, proj, w_out, gate, final_g, bsz=bsz, t_len=seq_len)
    return out, finals


def kernel(x_prompt, x_sample, cache_l0_a_k, cache_l0_a_v, cache_l0_mla_ckv, cache_l0_mla_kpe, state_l1_fwd, state_l1_bwd, c, c_ctx, mod_w, mod_b, norm_g, final_norm_g, l0_w_in, l0_w_out, l0_diff_lambda, l0_subln_g, l0_q_norm_g, l0_w_uq, l0_kv_norm_g, l0_w_ukv, l1_w_in, l1_w_out, l1_mu, l1_w0, l1_w2, l1_a0, l1_a2, l1_k_k, l1_k_a, l1_r_k, l1_ln_w, l1_ln_b):
    d = D_MODEL
    bp, tp, _ = x_prompt.shape
    bs, ts, _ = x_sample.shape

    cond = jnp.concatenate([c_ctx[None, :], c, jnp.zeros((MOD_ROWS - 1 - bs, d), F32)], axis=0)
    mods = _adaln(cond, mod_w, mod_b)

    def mod_rows(layer, lo, hi):
        rows = mods[layer, lo:hi]
        shift, scale, gate = (rows[:, i * d:(i + 1) * d].reshape(hi - lo, 1, d) for i in range(3))
        return shift, scale, gate, norm_g[layer].reshape(1, d)

    w_uq = l0_w_uq.reshape(B_Q_LORA, B_HEADS, B_NOPE + B_ROPE)
    w_uq = jnp.concatenate([w_uq[:, :, :B_NOPE].reshape(B_Q_LORA, -1), w_uq[:, :, B_NOPE:].reshape(B_Q_LORA, -1)],
                           axis=1).astype(BF16)
    l0_weights = (_l0_w_in_layout(l0_w_in), l0_w_out.astype(BF16), l0_diff_lambda, l0_subln_g.reshape(1, -1),
                  l0_q_norm_g.reshape(1, -1), w_uq, l0_kv_norm_g.reshape(1, -1), l0_w_ukv.astype(BF16))
    lora_pad = jnp.zeros((2, LORA_PAD - C_LORA, C_WIDTH), F32)
    lora_w = jnp.concatenate([jnp.concatenate([l1_w2, lora_pad], axis=1),
                              jnp.concatenate([l1_a2, lora_pad], axis=1)], axis=0).astype(BF16)
    lora_b = jnp.concatenate([l1_w0, l1_a0], axis=0).reshape(4, 1, C_WIDTH)
    lane_params = [_lane_tile(p) for p in (l1_k_k, l1_k_a, l1_r_k, l1_ln_w, l1_ln_b)]
    l1_weights = (_l1_cols_layout(l1_w_in.astype(BF16)), l1_w_out.astype(BF16), _l1_cols_layout(l1_mu),
                  lora_w, lora_b, lane_params)
    tables = _rope_tables(ts)
    final_g = final_norm_g.reshape(1, d)

    xp = x_prompt.reshape(bp * tp, d)
    xs = x_sample.reshape(bs * ts, d)
    ctx0 = (cache_l0_a_k, cache_l0_a_v, cache_l0_mla_ckv, cache_l0_mla_kpe)

    xp1, proj_p, ckv_p = _layer0(xp, mod_rows(0, 0, 1), bp * tp, tp, l0_weights, None, None)
    xs1, _, _ = _layer0(xs, mod_rows(0, 1, 1 + bs), ts, ts, l0_weights, ctx0, tables)

    zero_state = jnp.zeros((bp // PER, C_HEAD * C_HEAD, LANES), F32)
    y_prompt, finals = _layer1(xp1, mod_rows(1, 0, 1), bp * tp, tp, l1_weights, (zero_state, zero_state), final_g)
    y_sample, _ = _layer1(xs1, mod_rows(1, 1, 1 + bs), ts, ts, l1_weights,
                          (_state_to_lanes(state_l1_fwd), _state_to_lanes(state_l1_bwd)), final_g)

    new_a_k = proj_p[:, 3072:4096].reshape(bp, tp, A_HEADS, 2, A_QK_DIM)
    new_a_v = proj_p[:, 4096:5120].reshape(bp, tp, A_HEADS, A_V_DIM)
    new_ckv = ckv_p.reshape(bp, tp, B_KV_LORA)
    new_kpe = proj_p[:, 5888:5888 + B_ROPE].reshape(bp, tp, B_ROPE)
    return (y_prompt.reshape(bp, tp, d), y_sample.reshape(bs, ts, d), new_a_k, new_a_v, new_ckv, new_kpe,
            finals[0], finals[1])
```

```python
import functools
import math

import jax
import jax.numpy as jnp
from jax import lax
from jax.experimental import pallas as pl
from jax.experimental.pallas import tpu as pltpu

F32 = jnp.float32
BF16 = jnp.bfloat16

D_MODEL = 2048
GRID_W = 64
ROPE_BASE = 10000.0
NORM_EPS = 1e-6
GN_EPS = 64e-5

A_HEADS = 8
A_QK_DIM = 64
A_V_DIM = 128
B_HEADS = 8
B_NOPE = 128
B_ROPE = 64
B_V = 128
B_Q_LORA = 512
B_KV_LORA = 256
AB_WIDTH = A_HEADS * A_V_DIM + B_HEADS * B_V
C_HEAD = 64
C_HEADS = D_MODEL // C_HEAD
C_WIDTH = C_HEADS * C_HEAD
C_LORA = 96
LAM_INIT_L0 = 0.8 - 0.6 * math.exp(-0.3 * 0)

LANES = 128
SUBLANES = 8
LORA_PAD = 128
MOD_ROWS = 8
VMEM_LIMIT = 56 * 1024 * 1024

L0_COLS = 6144
L0_GATE_A, L0_GATE_B, L0_AQ, L0_AK, L0_AV = 0, 1, 2, 3, 4
L0_CQ = 10
L0_CKV = 22
L0_KPE = 46
L1_COLS = 4 * C_WIDTH + 4 * LORA_PAD
PER = LANES // C_HEADS
PITCH = 72
TT = LANES
N_LANE_ARRAYS = 7


def _cparams(*sem):
    return pltpu.CompilerParams(dimension_semantics=sem, vmem_limit_bytes=VMEM_LIMIT)


def _silu(x):
    return x / (1.0 + jnp.exp(-x))


def _dot(a, b):
    return jnp.dot(a, b, preferred_element_type=F32)


def _dot_nt(a, b):
    return lax.dot_general(a, b, (((1,), (1,)), ((), ())), preferred_element_type=F32)


def _rope(x, cos, sin):
    lane = lax.broadcasted_iota(jnp.int32, x.shape, 1)
    low = (lane % 32) < 16
    rot = jnp.where(low, pltpu.roll(x, LANES - 16, 1), pltpu.roll(x, 16, 1))
    return x * cos + rot * sin


def _adaln_body(c_ref, w_ref, b_ref, o_ref):
    s = _silu(c_ref[...]).astype(BF16)
    o_ref[0] = _dot(s, w_ref[0].astype(BF16)) + b_ref[0]


def _adaln(cond, mod_w, mod_b):
    depth, d, n = mod_w.shape
    tn = 1024
    return pl.pallas_call(
        _adaln_body,
        grid=(depth, n // tn),
        in_specs=[
            pl.BlockSpec((MOD_ROWS, d), lambda l, j: (0, 0)),
            pl.BlockSpec((1, d, tn), lambda l, j: (l, 0, j)),
            pl.BlockSpec((1, 1, tn), lambda l, j: (l, 0, j)),
        ],
        out_specs=pl.BlockSpec((1, MOD_ROWS, tn), lambda l, j: (l, 0, j)),
        out_shape=jax.ShapeDtypeStruct((depth, MOD_ROWS, n), F32),
        compiler_params=_cparams("parallel", "parallel"),
        name="adaln",
    )(cond, mod_w, mod_b.reshape(depth, 1, n))


def _inproj_body(x_ref, sh_ref, sc_ref, g_ref, w_ref, *rest, seq_len):
    if seq_len:
        mu_ref, o_ref, h_ref, p_scr = rest
    else:
        o_ref, h_ref = rest

    @pl.when(pl.program_id(1) == 0)
    def _():
        x = x_ref[...]
        y = x * lax.rsqrt(jnp.mean(x * x, axis=-1, keepdims=True) + NORM_EPS) * g_ref[...]
        h_ref[...] = (y * (1.0 + sc_ref[0]) + sh_ref[0]).astype(BF16)

    if not seq_len:
        o_ref[...] = _dot(h_ref[...], w_ref[...])
        return
    tm, tn = o_ref.shape
    zeros = jnp.zeros((SUBLANES, tn), F32)
    mu0, mu1 = mu_ref[0:1, :], mu_ref[1:2, :]
    keep = 1.0 - mu0 - mu1
    p_scr[pl.ds(0, SUBLANES), :] = zeros
    p_scr[pl.ds(SUBLANES, tm), :] = _dot(h_ref[...], w_ref[...])
    p_scr[pl.ds(SUBLANES + tm, SUBLANES), :] = zeros
    at = lambda r0, n, off: p_scr[pl.ds(SUBLANES + r0 + off, n), :]
    o_ref[...] = at(0, tm, 0) * keep + at(0, tm, -1) * mu0 + at(0, tm, 1) * mu1
    edge = lax.broadcasted_iota(jnp.int32, (2 * SUBLANES, 1), 0)
    for s in range(1, tm // seq_len):
        r0 = s * seq_len - SUBLANES
        prev = jnp.where(edge == SUBLANES, 0.0, at(r0, 2 * SUBLANES, -1))
        nxt = jnp.where(edge == SUBLANES - 1, 0.0, at(r0, 2 * SUBLANES, 1))
        o_ref[pl.ds(r0, 2 * SUBLANES), :] = at(r0, 2 * SUBLANES, 0) * keep + prev * mu0 + nxt * mu1


def _inproj(x, shift, scale, g, w, mu, *, rows_per_mod, seq_len, tm, tn):
    m, d = x.shape
    n = w.shape[1]
    mod_map = lambda i, j: ((i * tm) // rows_per_mod, 0, 0)
    in_specs = [
        pl.BlockSpec((tm, d), lambda i, j: (i, 0)),
        pl.BlockSpec((1, 1, d), mod_map),
        pl.BlockSpec((1, 1, d), mod_map),
        pl.BlockSpec((1, d), lambda i, j: (0, 0)),
        pl.BlockSpec((d, tn), lambda i, j: (0, j)),
    ]
    args = [x, shift, scale, g, w]
    if mu is not None:
        in_specs.append(pl.BlockSpec((2, tn), lambda i, j: (0, j)))
        args.append(mu)
    scratch = [pltpu.VMEM((tm, d), BF16)]
    if mu is not None:
        scratch.append(pltpu.VMEM((tm + 2 * SUBLANES, tn), F32))
    return pl.pallas_call(
        functools.partial(_inproj_body, seq_len=seq_len if mu is not None else 0),
        grid=(m // tm, n // tn),
        in_specs=in_specs,
        out_specs=pl.BlockSpec((tm, tn), lambda i, j: (i, j)),
        out_shape=jax.ShapeDtypeStruct((m, n), F32),
        scratch_shapes=scratch,
        compiler_params=_cparams("parallel", "arbitrary"),
        name="inproj",
    )(*args)


def _rms(x, g):
    return x * lax.rsqrt(jnp.mean(x * x, axis=-1, keepdims=True) + NORM_EPS) * g


def _mla_prep_body(cq_ref, ckv_ref, qg_ref, kg_ref, wuq_ref, wukv_ref, qb_ref, ckvn_ref, kvb_ref):
    qb_ref[...] = _dot(_rms(cq_ref[...], qg_ref[...]).astype(BF16), wuq_ref[...])
    ckv = _rms(ckv_ref[...], kg_ref[...])
    ckvn_ref[...] = ckv
    kvb_ref[...] = _dot(ckv.astype(BF16), wukv_ref[...])


def _mla_prep(proj, q_norm_g, kv_norm_g, w_uq, w_ukv, *, tm):
    m = proj.shape[0]
    nq, nkv = w_uq.shape[1], w_ukv.shape[1]
    const = lambda i: (0, 0)
    return pl.pallas_call(
        _mla_prep_body,
        grid=(m // tm,),
        in_specs=[
            pl.BlockSpec((tm, B_Q_LORA), lambda i: (i, L0_CQ)),
            pl.BlockSpec((tm, B_KV_LORA), lambda i: (i, L0_CKV)),
            pl.BlockSpec((1, B_Q_LORA), const),
            pl.BlockSpec((1, B_KV_LORA), const),
            pl.BlockSpec((B_Q_LORA, nq), const),
            pl.BlockSpec((B_KV_LORA, nkv), const),
        ],
        out_specs=[
            pl.BlockSpec((tm, nq), lambda i: (i, 0)),
            pl.BlockSpec((tm, B_KV_LORA), lambda i: (i, 0)),
            pl.BlockSpec((tm, nkv), lambda i: (i, 0)),
        ],
        out_shape=[
            jax.ShapeDtypeStruct((m, nq), F32),
            jax.ShapeDtypeStruct((m, B_KV_LORA), F32),
            jax.ShapeDtypeStruct((m, nkv), F32),
        ],
        compiler_params=_cparams("parallel"),
        name="mla_prep",
    )(proj, proj, q_norm_g, kv_norm_g, w_uq, w_ukv)


def _matmul_body(x_ref, w_ref, o_ref):
    o_ref[...] = _dot(x_ref[...].astype(BF16), w_ref[...])


def _matmul(x, w, *, tm):
    m, k = x.shape
    n = w.shape[1]
    return pl.pallas_call(
        _matmul_body,
        grid=(m // tm,),
        in_specs=[pl.BlockSpec((tm, k), lambda i: (i, 0)), pl.BlockSpec((k, n), lambda i: (0, 0))],
        out_specs=pl.BlockSpec((tm, n), lambda i: (i, 0)),
        out_shape=jax.ShapeDtypeStruct((m, n), F32),
        compiler_params=_cparams("parallel"),
        name="matmul",
    )(x, w)


def _softmax_parts(scores):
    m = functools.reduce(jnp.maximum, [jnp.max(s, axis=-1, keepdims=True) for s in scores])
    ps = [jnp.exp(s - m) for s in scores]
    denom = functools.reduce(jnp.add, [jnp.sum(p, axis=-1, keepdims=True) for p in ps])
    return ps, denom


def _diff_attn_body(*refs, dec, tq):
    if dec:
        (q_ref, k_ref, v_ref, kc_ref, vc_ref, gate_ref, cos_ref, sin_ref, lam_ref, sg_ref,
         o_ref, kb, vb, kcb, vcb) = refs
    else:
        q_ref, k_ref, v_ref, gate_ref, lam_ref, sg_ref, o_ref, kb, vb = refs
    qi = pl.program_id(1)

    @pl.when(qi == 0)
    def _():
        for h in range(A_HEADS):
            hs = slice(h * LANES, (h + 1) * LANES)
            kt = k_ref[0, :, hs]
            if dec:
                kt = _rope(kt, cos_ref[...], sin_ref[...])
                kcb[:, hs] = kc_ref[0, :, hs].astype(BF16)
                vcb[:, hs] = vc_ref[0, :, hs].astype(BF16)
            kb[:, hs] = kt.astype(BF16)
            vb[:, hs] = v_ref[0, :, hs].astype(BF16)

    lp = lam_ref[...]
    lam = (jnp.exp(jnp.sum(lp[0:1] * lp[1:2], keepdims=True))
           - jnp.exp(jnp.sum(lp[2:3] * lp[3:4], keepdims=True)) + LAM_INIT_L0)
    if dec:
        row0 = pl.multiple_of(qi * tq, tq)
        cq, sq = cos_ref[pl.ds(row0, tq), :], sin_ref[pl.ds(row0, tq), :]
    first = lax.broadcasted_iota(jnp.int32, (1, LANES), 1) < A_QK_DIM
    for h in range(A_HEADS):
        hs = slice(h * LANES, (h + 1) * LANES)
        qh = q_ref[0, :, hs]
        if dec:
            qh = _rope(qh, cq, sq)
        qh = qh * (A_QK_DIM ** -0.5)
        q1 = jnp.where(first, qh, 0.0).astype(BF16)
        q2 = jnp.where(first, 0.0, qh).astype(BF16)
        keys = [kb[:, hs]] + ([kcb[:, hs]] if dec else [])
        vals = [vb[:, hs]] + ([vcb[:, hs]] if dec else [])
        p1, l1 = _softmax_parts([_dot_nt(q1, kp) for kp in keys])
        p2, l2 = _softmax_parts([_dot_nt(q2, kp) for kp in keys])
        o1 = functools.reduce(jnp.add, [_dot(x.astype(BF16), vp) for x, vp in zip(p1, vals)])
        o2 = functools.reduce(jnp.add, [_dot(x.astype(BF16), vp) for x, vp in zip(p2, vals)])
        o = o1 * (1.0 / l1) - o2 * (lam / l2)
        o = _rms(o, sg_ref[...]) * (1.0 - LAM_INIT_L0)
        o_ref[0, :, hs] = (o * _silu(gate_ref[0, :, hs])).astype(BF16)


def _diff_attn(proj, ctx_k, ctx_v, cos, sin, diff_lambda, subln_g, *, tq):
    bsz, t_len, _ = proj.shape
    dec = ctx_k is not None
    w = A_HEADS * LANES
    full = lambda col: pl.BlockSpec((1, t_len, w), lambda b, i: (b, 0, col))
    in_specs = [pl.BlockSpec((1, tq, w), lambda b, i: (b, i, L0_AQ)), full(L0_AK), full(L0_AV)]
    args = [proj, proj, proj]
    scratch = [pltpu.VMEM((t_len, w), BF16), pltpu.VMEM((t_len, w), BF16)]
    if dec:
        p_len = ctx_k.shape[1]
        in_specs += [pl.BlockSpec((1, p_len, w), lambda b, i: (b, 0, 0))] * 2
        args += [ctx_k, ctx_v]
        scratch += [pltpu.VMEM((p_len, w), BF16), pltpu.VMEM((p_len, w), BF16)]
    in_specs.append(pl.BlockSpec((1, tq, w), lambda b, i: (b, i, L0_GATE_A)))
    args.append(proj)
    if dec:
        in_specs += [pl.BlockSpec((t_len, LANES), lambda b, i: (0, 0))] * 2
        args += [cos, sin]
    in_specs += [pl.BlockSpec((4, A_QK_DIM), lambda b, i: (0, 0)),
                 pl.BlockSpec((1, A_V_DIM), lambda b, i: (0, 0))]
    args += [diff_lambda, subln_g]
    return pl.pallas_call(
        functools.partial(_diff_attn_body, dec=dec, tq=tq),
        grid=(bsz, t_len // tq),
        in_specs=in_specs,
        out_specs=pl.BlockSpec((1, tq, w), lambda b, i: (b, i, 0)),
        out_shape=jax.ShapeDtypeStruct((bsz, t_len, w), BF16),
        scratch_shapes=scratch,
        compiler_params=_cparams("parallel", "arbitrary"),
        name="diff_attn",
    )(*args)


def _mla_attn_body(*refs, dec, tq):
    if dec:
        (qn_ref, qp_ref, kv_ref, kpe_ref, kvc_ref, kpec_ref, gate_ref, cos_ref, sin_ref,
         o_ref, kb, vb, kcb, vcb) = refs
    else:
        qn_ref, qp_ref, kv_ref, kpe_ref, gate_ref, o_ref, kb, vb = refs
    qi = pl.program_id(1)

    def stage(src_ref, kpe, k_dst, v_dst):
        for h in range(B_HEADS):
            k_dst[:, 2 * h * LANES:(2 * h + 1) * LANES] = src_ref[0, :, 2 * h * LANES:(2 * h + 1) * LANES].astype(BF16)
            k_dst[:, (2 * h + 1) * LANES:(2 * h + 2) * LANES] = kpe
            v_dst[:, h * LANES:(h + 1) * LANES] = src_ref[0, :, (2 * h + 1) * LANES:(2 * h + 2) * LANES].astype(BF16)

    @pl.when(qi == 0)
    def _():
        kp = kpe_ref[0]
        if dec:
            kp = _rope(kp, cos_ref[...], sin_ref[...])
            stage(kvc_ref, kpec_ref[0].astype(BF16), kcb, vcb)
        stage(kv_ref, kp.astype(BF16), kb, vb)

    if dec:
        row0 = pl.multiple_of(qi * tq, tq)
        cq, sq = cos_ref[pl.ds(row0, tq), :], sin_ref[pl.ds(row0, tq), :]
    scale = (B_NOPE + B_ROPE) ** -0.5
    first = lax.broadcasted_iota(jnp.int32, (1, LANES), 1) < B_ROPE
    for h in range(B_HEADS):
        hs = slice(h * LANES, (h + 1) * LANES)
        if h % 2 == 0:
            pair = qp_ref[0, :, (h // 2) * LANES:(h // 2 + 1) * LANES]
            if dec:
                pair = _rope(pair, cq, sq)
            pair = pair * scale
        qp = (jnp.where(first, pair, 0.0) if h % 2 == 0 else jnp.where(first, 0.0, pair)).astype(BF16)
        qn = (qn_ref[0, :, hs] * scale).astype(BF16)
        q = jnp.concatenate([qn, qp], axis=1)
        ks = slice(2 * h * LANES, (2 * h + 2) * LANES)
        scores = [_dot_nt(q, kb[:, ks])]
        vals = [vb[:, hs]]
        if dec:
            scores.append(_dot_nt(q, kcb[:, ks]))
            vals.append(vcb[:, hs])
        ps, denom = _softmax_parts(scores)
        o = functools.reduce(jnp.add, [_dot(p.astype(BF16), vp) for p, vp in zip(ps, vals)]) * (1.0 / denom)
        o_ref[0, :, hs] = (o * _silu(gate_ref[0, :, hs])).astype(BF16)


def _mla_attn(proj, q_b, kv_b, ctx_kv, ctx_kpe, cos, sin, *, tq):
    bsz, t_len, _ = proj.shape
    dec = ctx_kv is not None
    w = B_HEADS * LANES
    nkv = kv_b.shape[2]
    in_specs = [
        pl.BlockSpec((1, tq, w), lambda b, i: (b, i, 0)),
        pl.BlockSpec((1, tq, B_HEADS * B_ROPE), lambda b, i: (b, i, 2)),
        pl.BlockSpec((1, t_len, nkv), lambda b, i: (b, 0, 0)),
        pl.BlockSpec((1, t_len, LANES), lambda b, i: (b, 0, L0_KPE)),
    ]
    args = [q_b, q_b, kv_b, proj]
    scratch = [pltpu.VMEM((t_len, 2 * w), BF16), pltpu.VMEM((t_len, w), BF16)]
    if dec:
        p_len = ctx_kv.shape[1]
        in_specs += [pl.BlockSpec((1, p_len, nkv), lambda b, i: (b, 0, 0)),
                     pl.BlockSpec((1, p_len, LANES), lambda b, i: (b, 0, 0))]
        args += [ctx_kv, ctx_kpe]
        scratch += [pltpu.VMEM((p_len, 2 * w), BF16), pltpu.VMEM((p_len, w), BF16)]
    in_specs.append(pl.BlockSpec((1, tq, w), lambda b, i: (b, i, L0_GATE_B)))
    args.append(proj)
    if dec:
        in_specs += [pl.BlockSpec((t_len, LANES), lambda b, i: (0, 0))] * 2
        args += [cos, sin]
    return pl.pallas_call(
        functools.partial(_mla_attn_body, dec=dec, tq=tq),
        grid=(bsz, t_len // tq),
        in_specs=in_specs,
        out_specs=pl.BlockSpec((1, tq, w), lambda b, i: (b, i, 0)),
        out_shape=jax.ShapeDtypeStruct((bsz, t_len, w), BF16),
        scratch_shapes=scratch,
        compiler_params=_cparams("parallel", "arbitrary"),
        name="mla_attn",
    )(*args)


def _outproj_body(x_ref, ya_ref, yb_ref, w_ref, gate_ref, o_ref):
    half = ya_ref.shape[1]
    acc = _dot(ya_ref[...], w_ref[0:half, :]) + _dot(yb_ref[...], w_ref[half:, :])
    o_ref[...] = x_ref[...] + gate_ref[0] * acc


def _outproj(x, ya, yb, w, gate, *, rows_per_mod, tm):
    m, d = x.shape
    k = w.shape[0]
    return pl.pallas_call(
        _outproj_body,
        grid=(m // tm,),
        in_specs=[
            pl.BlockSpec((tm, d), lambda i: (i, 0)),
            pl.BlockSpec((tm, k // 2), lambda i: (i, 0)),
            pl.BlockSpec((tm, k // 2), lambda i: (i, 0)),
            pl.BlockSpec((k, d), lambda i: (0, 0), pipeline_mode=pl.Buffered(1)),
            pl.BlockSpec((1, 1, d), lambda i: ((i * tm) // rows_per_mod, 0, 0)),
        ],
        out_specs=pl.BlockSpec((tm, d), lambda i: (i, 0)),
        out_shape=jax.ShapeDtypeStruct((m, d), F32),
        compiler_params=_cparams("parallel"),
        name="outproj",
    )(x, ya, yb, w, gate)


def _relayout_body(main_ref, lora_ref, w_ref, bias_ref, o_ref, r_scr, q_scr):
    q = pl.program_id(2)

    def park(tile_of, prepare=None):
        for b in range(PER):
            if prepare is not None:
                prepare(b)
            for c in range(C_WIDTH // LANES):
                at = tile_of(b, c).T
                p0 = b * C_HEADS + 2 * c
                r_scr[pl.ds(p0 * PITCH, C_HEAD), :] = at[:C_HEAD]
                r_scr[pl.ds((p0 + 1) * PITCH, C_HEAD), :] = at[C_HEAD:]

    @pl.when(q < 3)
    def _():
        park(lambda b, c: main_ref[0, b, :, c * LANES:(c + 1) * LANES])

    expanded = lambda b, c: q_scr[:, c * LANES:(c + 1) * LANES]

    @pl.when((q == 3) | (q == 4))
    def _():
        def decay(b):
            u = -(bias_ref[0] + _dot(jnp.tanh(lora_ref[0, b]).astype(BF16), w_ref[0]))
            softplus = jnp.maximum(u, 0.0) + jnp.log(1.0 + jnp.exp(-jnp.abs(u)))
            q_scr[...] = jnp.exp(-jnp.exp(-softplus - 0.5))
        park(expanded, decay)

    @pl.when(q >= 5)
    def _():
        def iclr(b):
            q_scr[...] = 1.0 / (1.0 + jnp.exp(-(bias_ref[0] + _dot(lora_ref[0, b].astype(BF16), w_ref[0]))))
        park(expanded, iclr)

    for j in range(C_HEAD):
        m = r_scr[pl.ds(j, LANES, stride=PITCH), :]
        o_ref[0, 0, pl.ds(j, TT, stride=PITCH), :] = m.T
    zero = jnp.zeros((TT, LANES), F32)
    for j in range(C_HEAD, PITCH):
        o_ref[0, 0, pl.ds(j, TT, stride=PITCH), :] = zero


def _relayout(proj, lora_w, lora_b, bsz, t_len):
    g = bsz // PER
    proj4 = proj.reshape(g, PER, t_len, L1_COLS)
    lora0 = 4 * C_WIDTH // LANES
    lora_ix = lambda gi, ti, q: (jnp.clip(q - 3, 0, 3), 0, 0)
    out = pl.pallas_call(
        _relayout_body,
        grid=(g, t_len // TT, N_LANE_ARRAYS),
        in_specs=[
            pl.BlockSpec((1, PER, TT, C_WIDTH), lambda gi, ti, q: (gi, 0, ti, jnp.minimum(q, 2))),
            pl.BlockSpec((1, PER, TT, LANES), lambda gi, ti, q: (gi, 0, ti, lora0 + jnp.clip(q - 3, 0, 3))),
            pl.BlockSpec((1, LORA_PAD, C_WIDTH), lora_ix),
            pl.BlockSpec((1, 1, C_WIDTH), lora_ix),
        ],
        out_specs=pl.BlockSpec((1, 1, TT * PITCH, LANES), lambda gi, ti, q: (q, gi, ti, 0)),
        out_shape=jax.ShapeDtypeStruct((N_LANE_ARRAYS, g, t_len * PITCH, LANES), F32),
        scratch_shapes=[pltpu.VMEM((LANES * PITCH, LANES), F32), pltpu.VMEM((TT, C_WIDTH), F32)],
        compiler_params=_cparams("parallel", "parallel", "arbitrary"),
        name="lane_relayout",
    )(proj4, proj4, lora_w, lora_b)
    return out.reshape(N_LANE_ARRAYS, g, t_len, PITCH, LANES)


def _lane_tile(p):
    return jnp.tile(p.reshape(C_HEADS, C_HEAD).T, (1, PER))


def _scan_body(*refs, tc, reverse, has_prev):
    r_ref, k_ref, v_ref, w_ref, a_ref, kkp_ref, kap_ref, rkp_ref, lnw_ref, lnb_ref, s0_ref = refs[:11]
    rest = refs[11:]
    if has_prev:
        prev_ref, rest = rest[0], rest[1:]
    o_ref, sfin_ref, s_scr, kka_scr, kz_scr, nkk_scr, y_scr = rest
    c = pl.program_id(1)

    @pl.when(c == 0)
    def _():
        s_scr[...] = s0_ref[0]

    used = lambda ref: ref[0, 0, :, :C_HEAD, :]
    k = used(k_ref)
    a = used(a_ref)
    kk = k * kkp_ref[...]
    kk = kk * lax.rsqrt(jnp.maximum(jnp.sum(kk * kk, axis=1, keepdims=True), 1e-12))
    kz = k * (1.0 + (a - 1.0) * kap_ref[...])
    kz_scr[...] = kz
    kka_scr[...] = kk * a
    nkk_scr[...] = -kk
    bonus = jnp.sum(used(r_ref) * kz * rkp_ref[...], axis=1, keepdims=True) * used(v_ref)

    first = tc - 1 if reverse else 0
    sa0 = jnp.zeros((C_HEAD, LANES), F32)
    for j in range(C_HEAD):
        sa0 = sa0 + s_scr[pl.ds(j * C_HEAD, C_HEAD), :] * nkk_scr[first, j:j + 1, :]

    def step(i, sa):
        t = (tc - 1 - i) if reverse else i
        t_next = jnp.maximum(t - 1, 0) if reverse else jnp.minimum(t + 1, tc - 1)
        v = v_ref[0, 0, t, :C_HEAD, :]
        y = jnp.zeros((C_HEAD, LANES), F32)
        sa_next = jnp.zeros((C_HEAD, LANES), F32)
        for j in range(C_HEAD):
            rows = pl.ds(j * C_HEAD, C_HEAD)
            s = s_scr[rows, :] * w_ref[0, 0, t, j:j + 1, :] + sa * kka_scr[t, j:j + 1, :] + v * kz_scr[t, j:j + 1, :]
            s_scr[rows, :] = s
            y = y + s * r_ref[0, 0, t, j:j + 1, :]
            sa_next = sa_next + s * nkk_scr[t_next, j:j + 1, :]
        y_scr[t] = y
        return sa_next

    lax.fori_loop(0, tc, step, sa0)

    y = y_scr[...]
    dev = y - jnp.mean(y, axis=1, keepdims=True)
    var = jnp.mean(dev * dev, axis=1, keepdims=True)
    out = dev * lax.rsqrt(var + GN_EPS) * lnw_ref[...] + lnb_ref[...] + bonus
    if has_prev:
        out = out + prev_ref[0, :, :C_HEAD, :]
    o_ref[0, :, :C_HEAD, :] = out
    o_ref[0, :, C_HEAD:, :] = jnp.zeros((tc, PITCH - C_HEAD, LANES), F32)

    @pl.when(c == pl.num_programs(1) - 1)
    def _():
        sfin_ref[0] = s_scr[...]


def _scan(z7, z, params, s0, prev, *, reverse, tc):
    _, g, t_len = z7.shape[:3]
    n_c = t_len // tc
    tix = (lambda ci: n_c - 1 - ci) if reverse else (lambda ci: ci)
    qblk = lambda q: pl.BlockSpec((1, 1, tc, PITCH, LANES), lambda gi, ci: (q, gi, tix(ci), 0, 0))
    tile = pl.BlockSpec((C_HEAD, LANES), lambda gi, ci: (0, 0))
    st = pl.BlockSpec((1, C_HEAD * C_HEAD, LANES), lambda gi, ci: (gi, 0, 0))
    oblk = pl.BlockSpec((1, tc, PITCH, LANES), lambda gi, ci: (gi, tix(ci), 0, 0))
    in_specs = [qblk(0), qblk(1), qblk(2), qblk(3 + z), qblk(5 + z)] + [tile] * len(params) + [st]
    args = [z7] * 5 + list(params) + [s0]
    if prev is not None:
        in_specs.append(oblk)
        args.append(prev)
    chunk = pltpu.VMEM((tc, C_HEAD, LANES), F32)
    return pl.pallas_call(
        functools.partial(_scan_body, tc=tc, reverse=reverse, has_prev=prev is not None),
        grid=(g, n_c),
        in_specs=in_specs,
        out_specs=[oblk, st],
        out_shape=[jax.ShapeDtypeStruct((g, t_len, PITCH, LANES), F32), jax.ShapeDtypeStruct(s0.shape, F32)],
        scratch_shapes=[pltpu.VMEM((C_HEAD * C_HEAD, LANES), F32)] + [chunk] * 4,
        compiler_params=_cparams("parallel", "arbitrary"),
        name="wkv_scan",
    )(*args)


def _state_to_lanes(s):
    per = LANES // C_HEADS
    bsz = s.shape[0]
    s = s.reshape(bsz // per, per, C_HEADS, C_HEAD, C_HEAD)
    return s.transpose(0, 4, 3, 1, 2).reshape(bsz // per, C_HEAD * C_HEAD, LANES)


def _state_from_lanes(s):
    per = LANES // C_HEADS
    g = s.shape[0]
    s = s.reshape(g, C_HEAD, C_HEAD, per, C_HEADS)
    return s.transpose(0, 3, 4, 2, 1).reshape(g * per, C_HEADS, C_HEAD, C_HEAD)


def _rwkv_out_body(y_ref, x_ref, gate_ref, w_ref, mg_ref, fg_ref, o_ref, r_scr, y_scr):
    b = pl.program_id(2)

    @pl.when(b == 0)
    def _():
        for i in range(C_HEAD):
            m = y_ref[0, pl.ds(i, TT, stride=PITCH), :]
            r_scr[pl.ds(i, LANES, stride=PITCH), :] = m.T
        for bb in range(PER):
            for c in range(C_WIDTH // LANES):
                p0 = bb * C_HEADS + 2 * c
                pair = jnp.concatenate([r_scr[pl.ds(p0 * PITCH, C_HEAD), :],
                                        r_scr[pl.ds((p0 + 1) * PITCH, C_HEAD), :]], axis=0)
                y_scr[bb, :, c * LANES:(c + 1) * LANES] = pair.T

    yg = (y_scr[b] * _silu(gate_ref[0, 0])).astype(BF16)
    xn = x_ref[0, 0] + mg_ref[0] * _dot(yg, w_ref[...])
    o_ref[0, 0] = _rms(xn, fg_ref[...])


def _rwkv_out(x, y, proj, w, gate, final_g, *, bsz, t_len):
    d = x.shape[1]
    g = bsz // PER
    n_mod = gate.shape[0]
    tok = lambda gi, ti, b: (gi, b, ti, 0)
    const = lambda gi, ti, b: (0, 0)
    out = pl.pallas_call(
        _rwkv_out_body,
        grid=(g, t_len // TT, PER),
        in_specs=[
            pl.BlockSpec((1, TT * PITCH, LANES), lambda gi, ti, b: (gi, ti, 0)),
            pl.BlockSpec((1, 1, TT, d), tok),
            pl.BlockSpec((1, 1, TT, C_WIDTH), lambda gi, ti, b: (gi, b, ti, 3)),
            pl.BlockSpec((C_WIDTH, d), const, pipeline_mode=pl.Buffered(1)),
            pl.BlockSpec((1, 1, d), lambda gi, ti, b: ((gi * PER + b) % n_mod, 0, 0)),
            pl.BlockSpec((1, d), const),
        ],
        out_specs=pl.BlockSpec((1, 1, TT, d), tok),
        out_shape=jax.ShapeDtypeStruct((g, PER, t_len, d), F32),
        scratch_shapes=[pltpu.VMEM((LANES * PITCH, LANES), F32), pltpu.VMEM((PER, TT, C_WIDTH), F32)],
        compiler_params=_cparams("parallel", "parallel", "arbitrary"),
        name="rwkv_out",
    )(y.reshape(g, t_len * PITCH, LANES), x.reshape(g, PER, t_len, d), proj.reshape(g, PER, t_len, L1_COLS),
      w, gate, final_g)
    return out.reshape(bsz * t_len, d)


def _rope_tables(t_len):
    pos = jnp.arange(t_len)
    half = 16
    freqs = ROPE_BASE ** (-jnp.arange(half, dtype=F32) / half)
    ang_r = (pos // GRID_W).astype(F32)[:, None] * freqs
    ang_c = (pos % GRID_W).astype(F32)[:, None] * freqs
    cos = jnp.concatenate([jnp.cos(ang_r)] * 2 + [jnp.cos(ang_c)] * 2, axis=-1)
    sin = jnp.concatenate([-jnp.sin(ang_r), jnp.sin(ang_r), -jnp.sin(ang_c), jnp.sin(ang_c)], axis=-1)
    return jnp.tile(cos, (1, 2)), jnp.tile(sin, (1, 2))


def _l0_w_in_layout(w):
    d = w.shape[0]
    aq, ak, av, cq, ckv, kpe, gate = jnp.split(w.astype(BF16), [1024, 2048, 3072, 3584, 3840, 3904], axis=1)
    pad = jnp.zeros((d, L0_COLS - 6016), BF16)
    return jnp.concatenate([gate, aq, ak, av, cq, ckv, kpe, kpe, pad], axis=1)


def _l1_cols_layout(w):
    main, wd, ad = jnp.split(w, [4 * C_WIDTH, 4 * C_WIDTH + 2 * C_LORA], axis=1)
    pad = jnp.zeros((w.shape[0], LORA_PAD - C_LORA), w.dtype)
    parts = [main]
    for seg in (wd, ad):
        for z in range(2):
            parts += [seg[:, z * C_LORA:(z + 1) * C_LORA], pad]
    return jnp.concatenate(parts, axis=1)


def _layer0(x, mods, rows_per_mod, seq_len, weights, ctx, tables):
    w_in, w_out, diff_lambda, subln_g, q_norm_g, w_uq, kv_norm_g, w_ukv = weights
    shift, scale, gate, norm_g = mods
    m = x.shape[0]
    bsz = m // seq_len
    proj = _inproj(x, shift, scale, norm_g, w_in, None, rows_per_mod=rows_per_mod, seq_len=seq_len, tm=1024, tn=1024)
    q_b, ckv_n, kv_b = _mla_prep(proj, q_norm_g, kv_norm_g, w_uq, w_ukv, tm=512)
    proj3 = proj.reshape(bsz, seq_len, L0_COLS)
    q_b3 = q_b.reshape(bsz, seq_len, -1)
    kv_b3 = kv_b.reshape(bsz, seq_len, -1)
    if ctx is None:
        ya = _diff_attn(proj3, None, None, None, None, diff_lambda, subln_g, tq=256)
        yb = _mla_attn(proj3, q_b3, kv_b3, None, None, None, None, tq=256)
    else:
        k_ctx, v_ctx, ckv_ctx, kpe_ctx = ctx
        p_len = k_ctx.shape[1]
        cos, sin = tables
        kv_ctx = _matmul(ckv_ctx.reshape(bsz * p_len, B_KV_LORA), w_ukv, tm=512).reshape(bsz, p_len, -1)
        ya = _diff_attn(proj3, k_ctx.reshape(bsz, p_len, -1), v_ctx.reshape(bsz, p_len, -1), cos, sin,
                        diff_lambda, subln_g, tq=256)
        yb = _mla_attn(proj3, q_b3, kv_b3, kv_ctx, jnp.concatenate([kpe_ctx, kpe_ctx], axis=-1), cos, sin, tq=256)
    x_new = _outproj(x, ya.reshape(m, -1), yb.reshape(m, -1), w_out, gate, rows_per_mod=rows_per_mod, tm=512)
    return x_new, proj, ckv_n


def _layer1(x, mods, rows_per_mod, seq_len, weights, states, final_g):
    w_in, w_out, mu, lora_w, lora_b, lane_params = weights
    shift, scale, gate, norm_g = mods
    m = x.shape[0]
    bsz = m // seq_len
    proj = _inproj(x, shift, scale, norm_g, w_in, mu, rows_per_mod=rows_per_mod, seq_len=seq_len, tm=1024, tn=512)
    z7 = _relayout(proj, lora_w, lora_b, bsz, seq_len)
    y, finals = None, []
    for z in range(2):
        y, s_fin = _scan(z7, z, lane_params, states[z], y, reverse=(z == 1), tc=32)
        finals.append(_state_from_lanes(s_fin))
    out = _rwkv_out(x, y, proj, w_out, gate, final_g, bsz=bsz, t_len=seq_len)
    return out, finals


def kernel(x_prompt, x_sample, cache_l0_a_k, cache_l0_a_v, cache_l0_mla_ckv, cache_l0_mla_kpe, state_l1_fwd, state_l1_bwd, c, c_ctx, mod_w, mod_b, norm_g, final_norm_g, l0_w_in, l0_w_out, l0_diff_lambda, l0_subln_g, l0_q_norm_g, l0_w_uq, l0_kv_norm_g, l0_w_ukv, l1_w_in, l1_w_out, l1_mu, l1_w0, l1_w2, l1_a0, l1_a2, l1_k_k, l1_k_a, l1_r_k, l1_ln_w, l1_ln_b):
    d = D_MODEL
    bp, tp, _ = x_prompt.shape
    bs, ts, _ = x_sample.shape

    cond = jnp.concatenate([c_ctx[None, :], c, jnp.zeros((MOD_ROWS - 1 - bs, d), F32)], axis=0)
    mods = _adaln(cond, mod_w, mod_b)

    def mod_rows(layer, lo, hi):
        rows = mods[layer, lo:hi]
        shift, scale, gate = (rows[:, i * d:(i + 1) * d].reshape(hi - lo, 1, d) for i in range(3))
        return shift, scale, gate, norm_g[layer].reshape(1, d)

    w_uq = l0_w_uq.reshape(B_Q_LORA, B_HEADS, B_NOPE + B_ROPE)
    w_uq = jnp.concatenate([w_uq[:, :, :B_NOPE].reshape(B_Q_LORA, -1), w_uq[:, :, B_NOPE:].reshape(B_Q_LORA, -1)],
                           axis=1).astype(BF16)
    l0_weights = (_l0_w_in_layout(l0_w_in), l0_w_out.astype(BF16), l0_diff_lambda, l0_subln_g.reshape(1, -1),
                  l0_q_norm_g.reshape(1, -1), w_uq, l0_kv_norm_g.reshape(1, -1), l0_w_ukv.astype(BF16))
    lora_pad = jnp.zeros((2, LORA_PAD - C_LORA, C_WIDTH), F32)
    lora_w = jnp.concatenate([jnp.concatenate([l1_w2, lora_pad], axis=1),
                              jnp.concatenate([l1_a2, lora_pad], axis=1)], axis=0).astype(BF16)
    lora_b = jnp.concatenate([l1_w0, l1_a0], axis=0).reshape(4, 1, C_WIDTH)
    lane_params = [_lane_tile(p) for p in (l1_k_k, l1_k_a, l1_r_k, l1_ln_w, l1_ln_b)]
    l1_weights = (_l1_cols_layout(l1_w_in.astype(BF16)), l1_w_out.astype(BF16), _l1_cols_layout(l1_mu),
                  lora_w, lora_b, lane_params)
    tables = _rope_tables(ts)
    final_g = final_norm_g.reshape(1, d)

    xp = x_prompt.reshape(bp * tp, d)
    xs = x_sample.reshape(bs * ts, d)
    ctx0 = (cache_l0_a_k, cache_l0_a_v, cache_l0_mla_ckv, cache_l0_mla_kpe)

    xp1, proj_p, ckv_p = _layer0(xp, mod_rows(0, 0, 1), bp * tp, tp, l0_weights, None, None)
    xs1, _, _ = _layer0(xs, mod_rows(0, 1, 1 + bs), ts, ts, l0_weights, ctx0, tables)

    zero_state = jnp.zeros((bp // PER, C_HEAD * C_HEAD, LANES), F32)
    y_prompt, finals = _layer1(xp1, mod_rows(1, 0, 1), bp * tp, tp, l1_weights, (zero_state, zero_state), final_g)
    y_sample, _ = _layer1(xs1, mod_rows(1, 1, 1 + bs), ts, ts, l1_weights,
                          (_state_to_lanes(state_l1_fwd), _state_to_lanes(state_l1_bwd)), final_g)

    new_a_k = proj_p[:, 3072:4096].reshape(bp, tp, A_HEADS, 2, A_QK_DIM)
    new_a_v = proj_p[:, 4096:5120].reshape(bp, tp, A_HEADS, A_V_DIM)
    new_ckv = ckv_p.reshape(bp, tp, B_KV_LORA)
    new_kpe = proj_p[:, 5888:5888 + B_ROPE].reshape(bp, tp, B_ROPE)
    return (y_prompt.reshape(bp, tp, d), y_sample.reshape(bs, ts, d), new_a_k, new_a_v, new_ckv, new_kpe,
            finals[0], finals[1])
```

```python
import functools
import math

import jax
import jax.numpy as jnp
from jax import lax
from jax.experimental import pallas as pl
from jax.experimental.pallas import tpu as pltpu

F32 = jnp.float32
BF16 = jnp.bfloat16

D_MODEL = 2048
GRID_W = 64
ROPE_BASE = 10000.0
NORM_EPS = 1e-6
GN_EPS = 64e-5

A_HEADS = 8
A_QK_DIM = 64
A_V_DIM = 128
B_HEADS = 8
B_NOPE = 128
B_ROPE = 64
B_V = 128
B_Q_LORA = 512
B_KV_LORA = 256
AB_WIDTH = A_HEADS * A_V_DIM + B_HEADS * B_V
C_HEAD = 64
C_HEADS = D_MODEL // C_HEAD
C_WIDTH = C_HEADS * C_HEAD
C_LORA = 96
LAM_INIT_L0 = 0.8 - 0.6 * math.exp(-0.3 * 0)

LANES = 128
SUBLANES = 8
LORA_PAD = 128
MOD_ROWS = 8
VMEM_LIMIT = 56 * 1024 * 1024

L0_COLS = 6144
L0_GATE_A, L0_GATE_B, L0_AQ, L0_AK, L0_AV = 0, 1, 2, 3, 4
L0_CQ = 10
L0_CKV = 22
L0_KPE = 46
L1_COLS = 4 * C_WIDTH + 4 * LORA_PAD
PER = LANES // C_HEADS
PITCH = 72
TT = LANES
N_LANE_ARRAYS = 7


def _cparams(*sem):
    return pltpu.CompilerParams(dimension_semantics=sem, vmem_limit_bytes=VMEM_LIMIT)


def _silu(x):
    return x / (1.0 + jnp.exp(-x))


def _dot(a, b):
    return jnp.dot(a, b, preferred_element_type=F32)


def _dot_nt(a, b):
    return lax.dot_general(a, b, (((1,), (1,)), ((), ())), preferred_element_type=F32)


def _rope(x, cos, sin):
    lane = lax.broadcasted_iota(jnp.int32, x.shape, 1)
    low = (lane % 32) < 16
    rot = jnp.where(low, pltpu.roll(x, LANES - 16, 1), pltpu.roll(x, 16, 1))
    return x * cos + rot * sin


def _adaln_body(c_ref, w_ref, b_ref, o_ref):
    s = _silu(c_ref[...]).astype(BF16)
    o_ref[0] = _dot(s, w_ref[0].astype(BF16)) + b_ref[0]


def _adaln(cond, mod_w, mod_b):
    depth, d, n = mod_w.shape
    tn = 1024
    return pl.pallas_call(
        _adaln_body,
        grid=(depth, n // tn),
        in_specs=[
            pl.BlockSpec((MOD_ROWS, d), lambda l, j: (0, 0)),
            pl.BlockSpec((1, d, tn), lambda l, j: (l, 0, j)),
            pl.BlockSpec((1, 1, tn), lambda l, j: (l, 0, j)),
        ],
        out_specs=pl.BlockSpec((1, MOD_ROWS, tn), lambda l, j: (l, 0, j)),
        out_shape=jax.ShapeDtypeStruct((depth, MOD_ROWS, n), F32),
        compiler_params=_cparams("parallel", "parallel"),
        name="adaln",
    )(cond, mod_w, mod_b.reshape(depth, 1, n))


def _inproj_body(x_ref, sh_ref, sc_ref, g_ref, w_ref, *rest, seq_len):
    if seq_len:
        mu_ref, o_ref, h_ref, p_scr = rest
    else:
        o_ref, h_ref = rest

    @pl.when(pl.program_id(1) == 0)
    def _():
        x = x_ref[...]
        y = x * lax.rsqrt(jnp.mean(x * x, axis=-1, keepdims=True) + NORM_EPS) * g_ref[...]
        h_ref[...] = (y * (1.0 + sc_ref[0]) + sh_ref[0]).astype(BF16)

    if not seq_len:
        o_ref[...] = _dot(h_ref[...], w_ref[...])
        return
    tm, tn = o_ref.shape
    zeros = jnp.zeros((SUBLANES, tn), F32)
    mu0, mu1 = mu_ref[0:1, :], mu_ref[1:2, :]
    keep = 1.0 - mu0 - mu1
    p_scr[pl.ds(0, SUBLANES), :] = zeros
    p_scr[pl.ds(SUBLANES, tm), :] = _dot(h_ref[...], w_ref[...])
    p_scr[pl.ds(SUBLANES + tm, SUBLANES), :] = zeros
    at = lambda r0, n, off: p_scr[pl.ds(SUBLANES + r0 + off, n), :]
    o_ref[...] = at(0, tm, 0) * keep + at(0, tm, -1) * mu0 + at(0, tm, 1) * mu1
    edge = lax.broadcasted_iota(jnp.int32, (2 * SUBLANES, 1), 0)
    for s in range(1, tm // seq_len):
        r0 = s * seq_len - SUBLANES
        prev = jnp.where(edge == SUBLANES, 0.0, at(r0, 2 * SUBLANES, -1))
        nxt = jnp.where(edge == SUBLANES - 1, 0.0, at(r0, 2 * SUBLANES, 1))
        o_ref[pl.ds(r0, 2 * SUBLANES), :] = at(r0, 2 * SUBLANES, 0) * keep + prev * mu0 + nxt * mu1


def _inproj(x, shift, scale, g, w, mu, *, rows_per_mod, seq_len, tm, tn):
    m, d = x.shape
    n = w.shape[1]
    mod_map = lambda i, j: ((i * tm) // rows_per_mod, 0, 0)
    in_specs = [
        pl.BlockSpec((tm, d), lambda i, j: (i, 0)),
        pl.BlockSpec((1, 1, d), mod_map),
        pl.BlockSpec((1, 1, d), mod_map),
        pl.BlockSpec((1, d), lambda i, j: (0, 0)),
        pl.BlockSpec((d, tn), lambda i, j: (0, j)),
    ]
    args = [x, shift, scale, g, w]
    if mu is not None:
        in_specs.append(pl.BlockSpec((2, tn), lambda i, j: (0, j)))
        args.append(mu)
    scratch = [pltpu.VMEM((tm, d), BF16)]
    if mu is not None:
        scratch.append(pltpu.VMEM((tm + 2 * SUBLANES, tn), F32))
    return pl.pallas_call(
        functools.partial(_inproj_body, seq_len=seq_len if mu is not None else 0),
        grid=(m // tm, n // tn),
        in_specs=in_specs,
        out_specs=pl.BlockSpec((tm, tn), lambda i, j: (i, j)),
        out_shape=jax.ShapeDtypeStruct((m, n), F32),
        scratch_shapes=scratch,
        compiler_params=_cparams("parallel", "arbitrary"),
        name="inproj",
    )(*args)


def _rms(x, g):
    return x * lax.rsqrt(jnp.mean(x * x, axis=-1, keepdims=True) + NORM_EPS) * g


def _mla_prep_body(cq_ref, ckv_ref, qg_ref, kg_ref, wuq_ref, wukv_ref, qb_ref, ckvn_ref, kvb_ref):
    qb_ref[...] = _dot(_rms(cq_ref[...], qg_ref[...]).astype(BF16), wuq_ref[...])
    ckv = _rms(ckv_ref[...], kg_ref[...])
    ckvn_ref[...] = ckv
    kvb_ref[...] = _dot(ckv.astype(BF16), wukv_ref[...])


def _mla_prep(proj, q_norm_g, kv_norm_g, w_uq, w_ukv, *, tm):
    m = proj.shape[0]
    nq, nkv = w_uq.shape[1], w_ukv.shape[1]
    const = lambda i: (0, 0)
    return pl.pallas_call(
        _mla_prep_body,
        grid=(m // tm,),
        in_specs=[
            pl.BlockSpec((tm, B_Q_LORA), lambda i: (i, L0_CQ)),
            pl.BlockSpec((tm, B_KV_LORA), lambda i: (i, L0_CKV)),
            pl.BlockSpec((1, B_Q_LORA), const),
            pl.BlockSpec((1, B_KV_LORA), const),
            pl.BlockSpec((B_Q_LORA, nq), const),
            pl.BlockSpec((B_KV_LORA, nkv), const),
        ],
        out_specs=[
            pl.BlockSpec((tm, nq), lambda i: (i, 0)),
            pl.BlockSpec((tm, B_KV_LORA), lambda i: (i, 0)),
            pl.BlockSpec((tm, nkv), lambda i: (i, 0)),
        ],
        out_shape=[
            jax.ShapeDtypeStruct((m, nq), F32),
            jax.ShapeDtypeStruct((m, B_KV_LORA), F32),
            jax.ShapeDtypeStruct((m, nkv), F32),
        ],
        compiler_params=_cparams("parallel"),
        name="mla_prep",
    )(proj, proj, q_norm_g, kv_norm_g, w_uq, w_ukv)


def _matmul_body(x_ref, w_ref, o_ref):
    o_ref[...] = _dot(x_ref[...].astype(BF16), w_ref[...])


def _matmul(x, w, *, tm):
    m, k = x.shape
    n = w.shape[1]
    return pl.pallas_call(
        _matmul_body,
        grid=(m // tm,),
        in_specs=[pl.BlockSpec((tm, k), lambda i: (i, 0)), pl.BlockSpec((k, n), lambda i: (0, 0))],
        out_specs=pl.BlockSpec((tm, n), lambda i: (i, 0)),
        out_shape=jax.ShapeDtypeStruct((m, n), F32),
        compiler_params=_cparams("parallel"),
        name="matmul",
    )(x, w)


def _softmax_parts(scores):
    m = functools.reduce(jnp.maximum, [jnp.max(s, axis=-1, keepdims=True) for s in scores])
    ps = [jnp.exp(s - m) for s in scores]
    denom = functools.reduce(jnp.add, [jnp.sum(p, axis=-1, keepdims=True) for p in ps])
    return ps, denom


def _diff_attn_body(*refs, dec, tq):
    if dec:
        (q_ref, k_ref, v_ref, kc_ref, vc_ref, gate_ref, cos_ref, sin_ref, lam_ref, sg_ref,
         o_ref, kb, vb, kcb, vcb) = refs
    else:
        q_ref, k_ref, v_ref, gate_ref, lam_ref, sg_ref, o_ref, kb, vb = refs
    qi = pl.program_id(1)

    @pl.when(qi == 0)
    def _():
        for h in range(A_HEADS):
            hs = slice(h * LANES, (h + 1) * LANES)
            kt = k_ref[0, :, hs]
            if dec:
                kt = _rope(kt, cos_ref[...], sin_ref[...])
                kcb[:, hs] = kc_ref[0, :, hs].astype(BF16)
                vcb[:, hs] = vc_ref[0, :, hs].astype(BF16)
            kb[:, hs] = kt.astype(BF16)
            vb[:, hs] = v_ref[0, :, hs].astype(BF16)

    lp = lam_ref[...]
    lam = (jnp.exp(jnp.sum(lp[0:1] * lp[1:2], keepdims=True))
           - jnp.exp(jnp.sum(lp[2:3] * lp[3:4], keepdims=True)) + LAM_INIT_L0)
    if dec:
        row0 = pl.multiple_of(qi * tq, tq)
        cq, sq = cos_ref[pl.ds(row0, tq), :], sin_ref[pl.ds(row0, tq), :]
    first = lax.broadcasted_iota(jnp.int32, (1, LANES), 1) < A_QK_DIM
    for h in range(A_HEADS):
        hs = slice(h * LANES, (h + 1) * LANES)
        qh = q_ref[0, :, hs]
        if dec:
            qh = _rope(qh, cq, sq)
        qh = qh * (A_QK_DIM ** -0.5)
        q1 = jnp.where(first, qh, 0.0).astype(BF16)
        q2 = jnp.where(first, 0.0, qh).astype(BF16)
        keys = [kb[:, hs]] + ([kcb[:, hs]] if dec else [])
        vals = [vb[:, hs]] + ([vcb[:, hs]] if dec else [])
        p1, l1 = _softmax_parts([_dot_nt(q1, kp) for kp in keys])
        p2, l2 = _softmax_parts([_dot_nt(q2, kp) for kp in keys])
        o1 = functools.reduce(jnp.add, [_dot(x.astype(BF16), vp) for x, vp in zip(p1, vals)])
        o2 = functools.reduce(jnp.add, [_dot(x.astype(BF16), vp) for x, vp in zip(p2, vals)])
        o = o1 * (1.0 / l1) - o2 * (lam / l2)
        o = _rms(o, sg_ref[...]) * (1.0 - LAM_INIT_L0)
        o_ref[0, :, hs] = (o * _silu(gate_ref[0, :, hs])).astype(BF16)


def _diff_attn(proj, ctx_k, ctx_v, cos, sin, diff_lambda, subln_g, *, tq):
    bsz, t_len, _ = proj.shape
    dec = ctx_k is not None
    w = A_HEADS * LANES
    full = lambda col: pl.BlockSpec((1, t_len, w), lambda b, i: (b, 0, col))
    in_specs = [pl.BlockSpec((1, tq, w), lambda b, i: (b, i, L0_AQ)), full(L0_AK), full(L0_AV)]
    args = [proj, proj, proj]
    scratch = [pltpu.VMEM((t_len, w), BF16), pltpu.VMEM((t_len, w), BF16)]
    if dec:
        p_len = ctx_k.shape[1]
        in_specs += [pl.BlockSpec((1, p_len, w), lambda b, i: (b, 0, 0))] * 2
        args += [ctx_k, ctx_v]
        scratch += [pltpu.VMEM((p_len, w), BF16), pltpu.VMEM((p_len, w), BF16)]
    in_specs.append(pl.BlockSpec((1, tq, w), lambda b, i: (b, i, L0_GATE_A)))
    args.append(proj)
    if dec:
        in_specs += [pl.BlockSpec((t_len, LANES), lambda b, i: (0, 0))] * 2
        args += [cos, sin]
    in_specs += [pl.BlockSpec((4, A_QK_DIM), lambda b, i: (0, 0)),
                 pl.BlockSpec((1, A_V_DIM), lambda b, i: (0, 0))]
    args += [diff_lambda, subln_g]
    return pl.pallas_call(
        functools.partial(_diff_attn_body, dec=dec, tq=tq),
        grid=(bsz, t_len // tq),
        in_specs=in_specs,
        out_specs=pl.BlockSpec((1, tq, w), lambda b, i: (b, i, 0)),
        out_shape=jax.ShapeDtypeStruct((bsz, t_len, w), BF16),
        scratch_shapes=scratch,
        compiler_params=_cparams("parallel", "arbitrary"),
        name="diff_attn",
    )(*args)


def _mla_attn_body(*refs, dec, tq):
    if dec:
        (qn_ref, qp_ref, kv_ref, kpe_ref, kvc_ref, kpec_ref, gate_ref, cos_ref, sin_ref,
         o_ref, kb, vb, kcb, vcb) = refs
    else:
        qn_ref, qp_ref, kv_ref, kpe_ref, gate_ref, o_ref, kb, vb = refs
    qi = pl.program_id(1)

    def stage(src_ref, kpe, k_dst, v_dst):
        for h in range(B_HEADS):
            k_dst[:, 2 * h * LANES:(2 * h + 1) * LANES] = src_ref[0, :, 2 * h * LANES:(2 * h + 1) * LANES].astype(BF16)
            k_dst[:, (2 * h + 1) * LANES:(2 * h + 2) * LANES] = kpe
            v_dst[:, h * LANES:(h + 1) * LANES] = src_ref[0, :, (2 * h + 1) * LANES:(2 * h + 2) * LANES].astype(BF16)

    @pl.when(qi == 0)
    def _():
        kp = kpe_ref[0]
        if dec:
            kp = _rope(kp, cos_ref[...], sin_ref[...])
            stage(kvc_ref, kpec_ref[0].astype(BF16), kcb, vcb)
        stage(kv_ref, kp.astype(BF16), kb, vb)

    if dec:
        row0 = pl.multiple_of(qi * tq, tq)
        cq, sq = cos_ref[pl.ds(row0, tq), :], sin_ref[pl.ds(row0, tq), :]
    scale = (B_NOPE + B_ROPE) ** -0.5
    first = lax.broadcasted_iota(jnp.int32, (1, LANES), 1) < B_ROPE
    for h in range(B_HEADS):
        hs = slice(h * LANES, (h + 1) * LANES)
        if h % 2 == 0:
            pair = qp_ref[0, :, (h // 2) * LANES:(h // 2 + 1) * LANES]
            if dec:
                pair = _rope(pair, cq, sq)
            pair = pair * scale
        qp = (jnp.where(first, pair, 0.0) if h % 2 == 0 else jnp.where(first, 0.0, pair)).astype(BF16)
        qn = (qn_ref[0, :, hs] * scale).astype(BF16)
        q = jnp.concatenate([qn, qp], axis=1)
        ks = slice(2 * h * LANES, (2 * h + 2) * LANES)
        scores = [_dot_nt(q, kb[:, ks])]
        vals = [vb[:, hs]]
        if dec:
            scores.append(_dot_nt(q, kcb[:, ks]))
            vals.append(vcb[:, hs])
        ps, denom = _softmax_parts(scores)
        o = functools.reduce(jnp.add, [_dot(p.astype(BF16), vp) for p, vp in zip(ps, vals)]) * (1.0 / denom)
        o_ref[0, :, hs] = (o * _silu(gate_ref[0, :, hs])).astype(BF16)


def _mla_attn(proj, q_b, kv_b, ctx_kv, ctx_kpe, cos, sin, *, tq):
    bsz, t_len, _ = proj.shape
    dec = ctx_kv is not None
    w = B_HEADS * LANES
    nkv = kv_b.shape[2]
    in_specs = [
        pl.BlockSpec((1, tq, w), lambda b, i: (b, i, 0)),
        pl.BlockSpec((1, tq, B_HEADS * B_ROPE), lambda b, i: (b, i, 2)),
        pl.BlockSpec((1, t_len, nkv), lambda b, i: (b, 0, 0)),
        pl.BlockSpec((1, t_len, LANES), lambda b, i: (b, 0, L0_KPE)),
    ]
    args = [q_b, q_b, kv_b, proj]
    scratch = [pltpu.VMEM((t_len, 2 * w), BF16), pltpu.VMEM((t_len, w), BF16)]
    if dec:
        p_len = ctx_kv.shape[1]
        in_specs += [pl.BlockSpec((1, p_len, nkv), lambda b, i: (b, 0, 0)),
                     pl.BlockSpec((1, p_len, LANES), lambda b, i: (b, 0, 0))]
        args += [ctx_kv, ctx_kpe]
        scratch += [pltpu.VMEM((p_len, 2 * w), BF16), pltpu.VMEM((p_len, w), BF16)]
    in_specs.append(pl.BlockSpec((1, tq, w), lambda b, i: (b, i, L0_GATE_B)))
    args.append(proj)
    if dec:
        in_specs += [pl.BlockSpec((t_len, LANES), lambda b, i: (0, 0))] * 2
        args += [cos, sin]
    return pl.pallas_call(
        functools.partial(_mla_attn_body, dec=dec, tq=tq),
        grid=(bsz, t_len // tq),
        in_specs=in_specs,
        out_specs=pl.BlockSpec((1, tq, w), lambda b, i: (b, i, 0)),
        out_shape=jax.ShapeDtypeStruct((bsz, t_len, w), BF16),
        scratch_shapes=scratch,
        compiler_params=_cparams("parallel", "arbitrary"),
        name="mla_attn",
    )(*args)


def _outproj_body(x_ref, ya_ref, yb_ref, w_ref, gate_ref, o_ref):
    half = ya_ref.shape[1]
    acc = _dot(ya_ref[...], w_ref[0:half, :]) + _dot(yb_ref[...], w_ref[half:, :])
    o_ref[...] = x_ref[...] + gate_ref[0] * acc


def _outproj(x, ya, yb, w, gate, *, rows_per_mod, tm):
    m, d = x.shape
    k = w.shape[0]
    return pl.pallas_call(
        _outproj_body,
        grid=(m // tm,),
        in_specs=[
            pl.BlockSpec((tm, d), lambda i: (i, 0)),
            pl.BlockSpec((tm, k // 2), lambda i: (i, 0)),
            pl.BlockSpec((tm, k // 2), lambda i: (i, 0)),
            pl.BlockSpec((k, d), lambda i: (0, 0), pipeline_mode=pl.Buffered(1)),
            pl.BlockSpec((1, 1, d), lambda i: ((i * tm) // rows_per_mod, 0, 0)),
        ],
        out_specs=pl.BlockSpec((tm, d), lambda i: (i, 0)),
        out_shape=jax.ShapeDtypeStruct((m, d), F32),
        compiler_params=_cparams("parallel"),
        name="outproj",
    )(x, ya, yb, w, gate)


def _relayout_body(main_ref, lora_ref, w_ref, bias_ref, o_ref, r_scr, q_scr):
    q = pl.program_id(2)

    def park(tile_of, prepare=None):
        for b in range(PER):
            if prepare is not None:
                prepare(b)
            for c in range(C_WIDTH // LANES):
                at = tile_of(b, c).T
                p0 = b * C_HEADS + 2 * c
                r_scr[pl.ds(p0 * PITCH, C_HEAD), :] = at[:C_HEAD]
                r_scr[pl.ds((p0 + 1) * PITCH, C_HEAD), :] = at[C_HEAD:]

    @pl.when(q < 3)
    def _():
        park(lambda b, c: main_ref[0, b, :, c * LANES:(c + 1) * LANES])

    expanded = lambda b, c: q_scr[:, c * LANES:(c + 1) * LANES]

    @pl.when((q == 3) | (q == 4))
    def _():
        def decay(b):
            u = -(bias_ref[0] + _dot(jnp.tanh(lora_ref[0, b]).astype(BF16), w_ref[0]))
            softplus = jnp.maximum(u, 0.0) + jnp.log(1.0 + jnp.exp(-jnp.abs(u)))
            q_scr[...] = jnp.exp(-jnp.exp(-softplus - 0.5))
        park(expanded, decay)

    @pl.when(q >= 5)
    def _():
        def iclr(b):
            q_scr[...] = 1.0 / (1.0 + jnp.exp(-(bias_ref[0] + _dot(lora_ref[0, b].astype(BF16), w_ref[0]))))
        park(expanded, iclr)

    for j in range(C_HEAD):
        m = r_scr[pl.ds(j, LANES, stride=PITCH), :]
        o_ref[0, 0, pl.ds(j, TT, stride=PITCH), :] = m.T
    zero = jnp.zeros((TT, LANES), F32)
    for j in range(C_HEAD, PITCH):
        o_ref[0, 0, pl.ds(j, TT, stride=PITCH), :] = zero


def _relayout(proj, lora_w, lora_b, bsz, t_len):
    g = bsz // PER
    proj4 = proj.reshape(g, PER, t_len, L1_COLS)
    lora0 = 4 * C_WIDTH // LANES
    lora_ix = lambda gi, ti, q: (jnp.clip(q - 3, 0, 3), 0, 0)
    out = pl.pallas_call(
        _relayout_body,
        grid=(g, t_len // TT, N_LANE_ARRAYS),
        in_specs=[
            pl.BlockSpec((1, PER, TT, C_WIDTH), lambda gi, ti, q: (gi, 0, ti, jnp.minimum(q, 2))),
            pl.BlockSpec((1, PER, TT, LANES), lambda gi, ti, q: (gi, 0, ti, lora0 + jnp.clip(q - 3, 0, 3))),
            pl.BlockSpec((1, LORA_PAD, C_WIDTH), lora_ix),
            pl.BlockSpec((1, 1, C_WIDTH), lora_ix),
        ],
        out_specs=pl.BlockSpec((1, 1, TT * PITCH, LANES), lambda gi, ti, q: (q, gi, ti, 0)),
        out_shape=jax.ShapeDtypeStruct((N_LANE_ARRAYS, g, t_len * PITCH, LANES), F32),
        scratch_shapes=[pltpu.VMEM((LANES * PITCH, LANES), F32), pltpu.VMEM((TT, C_WIDTH), F32)],
        compiler_params=_cparams("parallel", "parallel", "arbitrary"),
        name="lane_relayout",
    )(proj4, proj4, lora_w, lora_b)
    return out.reshape(N_LANE_ARRAYS, g, t_len, PITCH, LANES)


def _lane_tile(p):
    return jnp.tile(p.reshape(C_HEADS, C_HEAD).T, (1, PER))


def _scan_body(*refs, tc, reverse, has_prev):
    r_ref, k_ref, v_ref, w_ref, a_ref, kkp_ref, kap_ref, rkp_ref, lnw_ref, lnb_ref, s0_ref = refs[:11]
    rest = refs[11:]
    if has_prev:
        prev_ref, rest = rest[0], rest[1:]
    o_ref, sfin_ref, s_scr, kka_scr, kz_scr, nkk_scr, y_scr = rest
    c = pl.program_id(1)

    @pl.when(c == 0)
    def _():
        s_scr[...] = s0_ref[0]

    used = lambda ref: ref[0, 0, :, :C_HEAD, :]
    k = used(k_ref)
    a = used(a_ref)
    kk = k * kkp_ref[...]
    kk = kk * lax.rsqrt(jnp.maximum(jnp.sum(kk * kk, axis=1, keepdims=True), 1e-12))
    kz = k * (1.0 + (a - 1.0) * kap_ref[...])
    kz_scr[...] = kz
    kka_scr[...] = kk * a
    nkk_scr[...] = -kk
    bonus = jnp.sum(used(r_ref) * kz * rkp_ref[...], axis=1, keepdims=True) * used(v_ref)

    first = tc - 1 if reverse else 0
    sa0 = jnp.zeros((C_HEAD, LANES), F32)
    for j in range(C_HEAD):
        sa0 = sa0 + s_scr[j] * nkk_scr[first, j:j + 1, :]

    def step(i, sa):
        t = (tc - 1 - i) if reverse else i
        t_next = jnp.maximum(t - 1, 0) if reverse else jnp.minimum(t + 1, tc - 1)
        v = v_ref[0, 0, t, :C_HEAD, :]
        y = jnp.zeros((C_HEAD, LANES), F32)
        sa_next = jnp.zeros((C_HEAD, LANES), F32)
        for j in range(C_HEAD):
            s = s_scr[j] * w_ref[0, 0, t, j:j + 1, :] + sa * kka_scr[t, j:j + 1, :] + v * kz_scr[t, j:j + 1, :]
            s_scr[j] = s
            y = y + s * r_ref[0, 0, t, j:j + 1, :]
            sa_next = sa_next + s * nkk_scr[t_next, j:j + 1, :]
        y_scr[t] = y
        return sa_next

    lax.fori_loop(0, tc, step, sa0)

    y = y_scr[...]
    dev = y - jnp.mean(y, axis=1, keepdims=True)
    var = jnp.mean(dev * dev, axis=1, keepdims=True)
    out = dev * lax.rsqrt(var + GN_EPS) * lnw_ref[...] + lnb_ref[...] + bonus
    if has_prev:
        out = out + prev_ref[0, :, :C_HEAD, :]
    o_ref[0, :, :C_HEAD, :] = out
    o_ref[0, :, C_HEAD:, :] = jnp.zeros((tc, PITCH - C_HEAD, LANES), F32)

    @pl.when(c == pl.num_programs(1) - 1)
    def _():
        sfin_ref[0] = s_scr[...]


def _scan(z7, z, params, s0, prev, *, reverse, tc):
    _, g, t_len = z7.shape[:3]
    n_c = t_len // tc
    tix = (lambda ci: n_c - 1 - ci) if reverse else (lambda ci: ci)
    qblk = lambda q: pl.BlockSpec((1, 1, tc, PITCH, LANES), lambda gi, ci: (q, gi, tix(ci), 0, 0))
    tile = pl.BlockSpec((C_HEAD, LANES), lambda gi, ci: (0, 0))
    st = pl.BlockSpec((1, C_HEAD, C_HEAD, LANES), lambda gi, ci: (gi, 0, 0, 0))
    oblk = pl.BlockSpec((1, tc, PITCH, LANES), lambda gi, ci: (gi, tix(ci), 0, 0))
    in_specs = [qblk(0), qblk(1), qblk(2), qblk(3 + z), qblk(5 + z)] + [tile] * len(params) + [st]
    args = [z7] * 5 + list(params) + [s0]
    if prev is not None:
        in_specs.append(oblk)
        args.append(prev)
    chunk = pltpu.VMEM((tc, C_HEAD, LANES), F32)
    return pl.pallas_call(
        functools.partial(_scan_body, tc=tc, reverse=reverse, has_prev=prev is not None),
        grid=(g, n_c),
        in_specs=in_specs,
        out_specs=[oblk, st],
        out_shape=[jax.ShapeDtypeStruct((g, t_len, PITCH, LANES), F32), jax.ShapeDtypeStruct(s0.shape, F32)],
        scratch_shapes=[pltpu.VMEM((C_HEAD, C_HEAD, LANES), F32)] + [chunk] * 4,
        compiler_params=_cparams("parallel", "arbitrary"),
        name="wkv_scan",
    )(*args)


def _state_to_lanes(s):
    per = LANES // C_HEADS
    bsz = s.shape[0]
    s = s.reshape(bsz // per, per, C_HEADS, C_HEAD, C_HEAD)
    return s.transpose(0, 4, 3, 1, 2).reshape(bsz // per, C_HEAD, C_HEAD, LANES)


def _state_from_lanes(s):
    per = LANES // C_HEADS
    g = s.shape[0]
    s = s.reshape(g, C_HEAD, C_HEAD, per, C_HEADS)
    return s.transpose(0, 3, 4, 2, 1).reshape(g * per, C_HEADS, C_HEAD, C_HEAD)


def _rwkv_out_body(y_ref, x_ref, gate_ref, w_ref, mg_ref, fg_ref, o_ref, r_scr, y_scr):
    b = pl.program_id(2)

    @pl.when(b == 0)
    def _():
        for i in range(C_HEAD):
            m = y_ref[0, pl.ds(i, TT, stride=PITCH), :]
            r_scr[pl.ds(i, LANES, stride=PITCH), :] = m.T
        for bb in range(PER):
            for c in range(C_WIDTH // LANES):
                p0 = bb * C_HEADS + 2 * c
                pair = jnp.concatenate([r_scr[pl.ds(p0 * PITCH, C_HEAD), :],
                                        r_scr[pl.ds((p0 + 1) * PITCH, C_HEAD), :]], axis=0)
                y_scr[bb, :, c * LANES:(c + 1) * LANES] = pair.T

    yg = (y_scr[b] * _silu(gate_ref[0, 0])).astype(BF16)
    xn = x_ref[0, 0] + mg_ref[0] * _dot(yg, w_ref[...])
    o_ref[0, 0] = _rms(xn, fg_ref[...])


def _rwkv_out(x, y, proj, w, gate, final_g, *, bsz, t_len):
    d = x.shape[1]
    g = bsz // PER
    n_mod = gate.shape[0]
    tok = lambda gi, ti, b: (gi, b, ti, 0)
    const = lambda gi, ti, b: (0, 0)
    out = pl.pallas_call(
        _rwkv_out_body,
        grid=(g, t_len // TT, PER),
        in_specs=[
            pl.BlockSpec((1, TT * PITCH, LANES), lambda gi, ti, b: (gi, ti, 0)),
            pl.BlockSpec((1, 1, TT, d), tok),
            pl.BlockSpec((1, 1, TT, C_WIDTH), lambda gi, ti, b: (gi, b, ti, 3)),
            pl.BlockSpec((C_WIDTH, d), const, pipeline_mode=pl.Buffered(1)),
            pl.BlockSpec((1, 1, d), lambda gi, ti, b: ((gi * PER + b) % n_mod, 0, 0)),
            pl.BlockSpec((1, d), const),
        ],
        out_specs=pl.BlockSpec((1, 1, TT, d), tok),
        out_shape=jax.ShapeDtypeStruct((g, PER, t_len, d), F32),
        scratch_shapes=[pltpu.VMEM((LANES * PITCH, LANES), F32), pltpu.VMEM((PER, TT, C_WIDTH), F32)],
        compiler_params=_cparams("parallel", "parallel", "arbitrary"),
        name="rwkv_out",
    )(y.reshape(g, t_len * PITCH, LANES), x.reshape(g, PER, t_len, d), proj.reshape(g, PER, t_len, L1_COLS),
      w, gate, final_g)
    return out.reshape(bsz * t_len, d)


def _rope_tables(t_len):
    pos = jnp.arange(t_len)
    half = 16
    freqs = ROPE_BASE ** (-jnp.arange(half, dtype=F32) / half)
    ang_r = (pos // GRID_W).astype(F32)[:, None] * freqs
    ang_c = (pos % GRID_W).astype(F32)[:, None] * freqs
    cos = jnp.concatenate([jnp.cos(ang_r)] * 2 + [jnp.cos(ang_c)] * 2, axis=-1)
    sin = jnp.concatenate([-jnp.sin(ang_r), jnp.sin(ang_r), -jnp.sin(ang_c), jnp.sin(ang_c)], axis=-1)
    return jnp.tile(cos, (1, 2)), jnp.tile(sin, (1, 2))


def _l0_w_in_layout(w):
    d = w.shape[0]
    aq, ak, av, cq, ckv, kpe, gate = jnp.split(w.astype(BF16), [1024, 2048, 3072, 3584, 3840, 3904], axis=1)
    pad = jnp.zeros((d, L0_COLS - 6016), BF16)
    return jnp.concatenate([gate, aq, ak, av, cq, ckv, kpe, kpe, pad], axis=1)


def _l1_cols_layout(w):
    main, wd, ad = jnp.split(w, [4 * C_WIDTH, 4 * C_WIDTH + 2 * C_LORA], axis=1)
    pad = jnp.zeros((w.shape[0], LORA_PAD - C_LORA), w.dtype)
    parts = [main]
    for seg in (wd, ad):
        for z in range(2):
            parts += [seg[:, z * C_LORA:(z + 1) * C_LORA], pad]
    return jnp.concatenate(parts, axis=1)


def _layer0(x, mods, rows_per_mod, seq_len, weights, ctx, tables):
    w_in, w_out, diff_lambda, subln_g, q_norm_g, w_uq, kv_norm_g, w_ukv = weights
    shift, scale, gate, norm_g = mods
    m = x.shape[0]
    bsz = m // seq_len
    proj = _inproj(x, shift, scale, norm_g, w_in, None, rows_per_mod=rows_per_mod, seq_len=seq_len, tm=1024, tn=1024)
    q_b, ckv_n, kv_b = _mla_prep(proj, q_norm_g, kv_norm_g, w_uq, w_ukv, tm=512)
    proj3 = proj.reshape(bsz, seq_len, L0_COLS)
    q_b3 = q_b.reshape(bsz, seq_len, -1)
    kv_b3 = kv_b.reshape(bsz, seq_len, -1)
    if ctx is None:
        ya = _diff_attn(proj3, None, None, None, None, diff_lambda, subln_g, tq=256)
        yb = _mla_attn(proj3, q_b3, kv_b3, None, None, None, None, tq=256)
    else:
        k_ctx, v_ctx, ckv_ctx, kpe_ctx = ctx
        p_len = k_ctx.shape[1]
        cos, sin = tables
        kv_ctx = _matmul(ckv_ctx.reshape(bsz * p_len, B_KV_LORA), w_ukv, tm=512).reshape(bsz, p_len, -1)
        ya = _diff_attn(proj3, k_ctx.reshape(bsz, p_len, -1), v_ctx.reshape(bsz, p_len, -1), cos, sin,
                        diff_lambda, subln_g, tq=256)
        yb = _mla_attn(proj3, q_b3, kv_b3, kv_ctx, jnp.concatenate([kpe_ctx, kpe_ctx], axis=-1), cos, sin, tq=256)
    x_new = _outproj(x, ya.reshape(m, -1), yb.reshape(m, -1), w_out, gate, rows_per_mod=rows_per_mod, tm=512)
    return x_new, proj, ckv_n


def _layer1(x, mods, rows_per_mod, seq_len, weights, states, final_g):
    w_in, w_out, mu, lora_w, lora_b, lane_params = weights
    shift, scale, gate, norm_g = mods
    m = x.shape[0]
    bsz = m // seq_len
    proj = _inproj(x, shift, scale, norm_g, w_in, mu, rows_per_mod=rows_per_mod, seq_len=seq_len, tm=1024, tn=512)
    z7 = _relayout(proj, lora_w, lora_b, bsz, seq_len)
    y, finals = None, []
    for z in range(2):
        y, s_fin = _scan(z7, z, lane_params, states[z], y, reverse=(z == 1), tc=32)
        finals.append(_state_from_lanes(s_fin))
    out = _rwkv_out(x, y, proj, w_out, gate, final_g, bsz=bsz, t_len=seq_len)
    return out, finals


def kernel(x_prompt, x_sample, cache_l0_a_k, cache_l0_a_v, cache_l0_mla_ckv, cache_l0_mla_kpe, state_l1_fwd, state_l1_bwd, c, c_ctx, mod_w, mod_b, norm_g, final_norm_g, l0_w_in, l0_w_out, l0_diff_lambda, l0_subln_g, l0_q_norm_g, l0_w_uq, l0_kv_norm_g, l0_w_ukv, l1_w_in, l1_w_out, l1_mu, l1_w0, l1_w2, l1_a0, l1_a2, l1_k_k, l1_k_a, l1_r_k, l1_ln_w, l1_ln_b):
    d = D_MODEL
    bp, tp, _ = x_prompt.shape
    bs, ts, _ = x_sample.shape

    cond = jnp.concatenate([c_ctx[None, :], c, jnp.zeros((MOD_ROWS - 1 - bs, d), F32)], axis=0)
    mods = _adaln(cond, mod_w, mod_b)

    def mod_rows(layer, lo, hi):
        rows = mods[layer, lo:hi]
        shift, scale, gate = (rows[:, i * d:(i + 1) * d].reshape(hi - lo, 1, d) for i in range(3))
        return shift, scale, gate, norm_g[layer].reshape(1, d)

    w_uq = l0_w_uq.reshape(B_Q_LORA, B_HEADS, B_NOPE + B_ROPE)
    w_uq = jnp.concatenate([w_uq[:, :, :B_NOPE].reshape(B_Q_LORA, -1), w_uq[:, :, B_NOPE:].reshape(B_Q_LORA, -1)],
                           axis=1).astype(BF16)
    l0_weights = (_l0_w_in_layout(l0_w_in), l0_w_out.astype(BF16), l0_diff_lambda, l0_subln_g.reshape(1, -1),
                  l0_q_norm_g.reshape(1, -1), w_uq, l0_kv_norm_g.reshape(1, -1), l0_w_ukv.astype(BF16))
    lora_pad = jnp.zeros((2, LORA_PAD - C_LORA, C_WIDTH), F32)
    lora_w = jnp.concatenate([jnp.concatenate([l1_w2, lora_pad], axis=1),
                              jnp.concatenate([l1_a2, lora_pad], axis=1)], axis=0).astype(BF16)
    lora_b = jnp.concatenate([l1_w0, l1_a0], axis=0).reshape(4, 1, C_WIDTH)
    lane_params = [_lane_tile(p) for p in (l1_k_k, l1_k_a, l1_r_k, l1_ln_w, l1_ln_b)]
    l1_weights = (_l1_cols_layout(l1_w_in.astype(BF16)), l1_w_out.astype(BF16), _l1_cols_layout(l1_mu),
                  lora_w, lora_b, lane_params)
    tables = _rope_tables(ts)
    final_g = final_norm_g.reshape(1, d)

    xp = x_prompt.reshape(bp * tp, d)
    xs = x_sample.reshape(bs * ts, d)
    ctx0 = (cache_l0_a_k, cache_l0_a_v, cache_l0_mla_ckv, cache_l0_mla_kpe)

    xp1, proj_p, ckv_p = _layer0(xp, mod_rows(0, 0, 1), bp * tp, tp, l0_weights, None, None)
    xs1, _, _ = _layer0(xs, mod_rows(0, 1, 1 + bs), ts, ts, l0_weights, ctx0, tables)

    zero_state = jnp.zeros((bp // PER, C_HEAD, C_HEAD, LANES), F32)
    y_prompt, finals = _layer1(xp1, mod_rows(1, 0, 1), bp * tp, tp, l1_weights, (zero_state, zero_state), final_g)
    y_sample, _ = _layer1(xs1, mod_rows(1, 1, 1 + bs), ts, ts, l1_weights,
                          (_state_to_lanes(state_l1_fwd), _state_to_lanes(state_l1_bwd)), final_g)

    new_a_k = proj_p[:, 3072:4096].reshape(bp, tp, A_HEADS, 2, A_QK_DIM)
    new_a_v = proj_p[:, 4096:5120].reshape(bp, tp, A_HEADS, A_V_DIM)
    new_ckv = ckv_p.reshape(bp, tp, B_KV_LORA)
    new_kpe = proj_p[:, 5888:5888 + B_ROPE].reshape(bp, tp, B_ROPE)
    return (y_prompt.reshape(bp, tp, d), y_sample.reshape(bs, ts, d), new_a_k, new_a_v, new_ckv, new_kpe,
            finals[0], finals[1])
```

```python
import functools
import math

import jax
import jax.numpy as jnp
from jax import lax
from jax.experimental import pallas as pl
from jax.experimental.pallas import tpu as pltpu

F32 = jnp.float32
BF16 = jnp.bfloat16

D_MODEL = 2048
GRID_W = 64
ROPE_BASE = 10000.0
NORM_EPS = 1e-6
GN_EPS = 64e-5

A_HEADS = 8
A_QK_DIM = 64
A_V_DIM = 128
B_HEADS = 8
B_NOPE = 128
B_ROPE = 64
B_V = 128
B_Q_LORA = 512
B_KV_LORA = 256
AB_WIDTH = A_HEADS * A_V_DIM + B_HEADS * B_V
C_HEAD = 64
C_HEADS = D_MODEL // C_HEAD
C_WIDTH = C_HEADS * C_HEAD
C_LORA = 96
LAM_INIT_L0 = 0.8 - 0.6 * math.exp(-0.3 * 0)
LOG2_E = math.log2(math.e)

LANES = 128
SUBLANES = 8
LORA_PAD = 128
MOD_ROWS = 8
VMEM_LIMIT = 56 * 1024 * 1024

L0_COLS = 6144
L0_GATE_A, L0_GATE_B, L0_AQ, L0_AK, L0_AV = 0, 1, 2, 3, 4
L0_CQ = 10
L0_CKV = 22
L0_KPE = 46
L1_COLS = 4 * C_WIDTH + 4 * LORA_PAD
PER = LANES // C_HEADS
PITCH = 72
TT = LANES
N_LANE_ARRAYS = 7


def _cparams(*sem):
    return pltpu.CompilerParams(dimension_semantics=sem, vmem_limit_bytes=VMEM_LIMIT)


def _silu(x):
    return x / (1.0 + jnp.exp(-x))


def _dot(a, b):
    return jnp.dot(a, b, preferred_element_type=F32)


def _dot_nt(a, b):
    return lax.dot_general(a, b, (((1,), (1,)), ((), ())), preferred_element_type=F32)


def _rope(x, cos, sin):
    lane = lax.broadcasted_iota(jnp.int32, x.shape, 1)
    low = (lane % 32) < 16
    rot = jnp.where(low, pltpu.roll(x, LANES - 16, 1), pltpu.roll(x, 16, 1))
    return x * cos + rot * sin


def _adaln_body(c_ref, w_ref, b_ref, o_ref):
    s = _silu(c_ref[...]).astype(BF16)
    o_ref[0] = _dot(s, w_ref[0].astype(BF16)) + b_ref[0]


def _adaln(cond, mod_w, mod_b):
    depth, d, n = mod_w.shape
    tn = 1024
    return pl.pallas_call(
        _adaln_body,
        grid=(depth, n // tn),
        in_specs=[
            pl.BlockSpec((MOD_ROWS, d), lambda l, j: (0, 0)),
            pl.BlockSpec((1, d, tn), lambda l, j: (l, 0, j)),
            pl.BlockSpec((1, 1, tn), lambda l, j: (l, 0, j)),
        ],
        out_specs=pl.BlockSpec((1, MOD_ROWS, tn), lambda l, j: (l, 0, j)),
        out_shape=jax.ShapeDtypeStruct((depth, MOD_ROWS, n), F32),
        compiler_params=_cparams("parallel", "parallel"),
        name="adaln",
    )(cond, mod_w, mod_b.reshape(depth, 1, n))


def _inproj_body(x_ref, sh_ref, sc_ref, g_ref, w_ref, *rest, seq_len):
    if seq_len:
        mu_ref, o_ref, h_ref, p_scr = rest
    else:
        o_ref, h_ref = rest

    @pl.when(pl.program_id(1) == 0)
    def _():
        x = x_ref[...]
        y = x * lax.rsqrt(jnp.mean(x * x, axis=-1, keepdims=True) + NORM_EPS) * g_ref[...]
        h_ref[...] = (y * (1.0 + sc_ref[0]) + sh_ref[0]).astype(BF16)

    if not seq_len:
        o_ref[...] = _dot(h_ref[...], w_ref[...])
        return
    tm, tn = o_ref.shape
    zeros = jnp.zeros((SUBLANES, tn), F32)
    mu0, mu1 = mu_ref[0:1, :], mu_ref[1:2, :]
    keep = 1.0 - mu0 - mu1
    p_scr[pl.ds(0, SUBLANES), :] = zeros
    p_scr[pl.ds(SUBLANES, tm), :] = _dot(h_ref[...], w_ref[...])
    p_scr[pl.ds(SUBLANES + tm, SUBLANES), :] = zeros
    at = lambda r0, n, off: p_scr[pl.ds(SUBLANES + r0 + off, n), :]
    o_ref[...] = at(0, tm, 0) * keep + at(0, tm, -1) * mu0 + at(0, tm, 1) * mu1
    edge = lax.broadcasted_iota(jnp.int32, (2 * SUBLANES, 1), 0)
    for s in range(1, tm // seq_len):
        r0 = s * seq_len - SUBLANES
        prev = jnp.where(edge == SUBLANES, 0.0, at(r0, 2 * SUBLANES, -1))
        nxt = jnp.where(edge == SUBLANES - 1, 0.0, at(r0, 2 * SUBLANES, 1))
        o_ref[pl.ds(r0, 2 * SUBLANES), :] = at(r0, 2 * SUBLANES, 0) * keep + prev * mu0 + nxt * mu1


def _inproj(x, shift, scale, g, w, mu, *, rows_per_mod, seq_len, tm, tn):
    m, d = x.shape
    n = w.shape[1]
    mod_map = lambda i, j: ((i * tm) // rows_per_mod, 0, 0)
    in_specs = [
        pl.BlockSpec((tm, d), lambda i, j: (i, 0)),
        pl.BlockSpec((1, 1, d), mod_map),
        pl.BlockSpec((1, 1, d), mod_map),
        pl.BlockSpec((1, d), lambda i, j: (0, 0)),
        pl.BlockSpec((d, tn), lambda i, j: (0, j)),
    ]
    args = [x, shift, scale, g, w]
    if mu is not None:
        in_specs.append(pl.BlockSpec((2, tn), lambda i, j: (0, j)))
        args.append(mu)
    scratch = [pltpu.VMEM((tm, d), BF16)]
    if mu is not None:
        scratch.append(pltpu.VMEM((tm + 2 * SUBLANES, tn), F32))
    return pl.pallas_call(
        functools.partial(_inproj_body, seq_len=seq_len if mu is not None else 0),
        grid=(m // tm, n // tn),
        in_specs=in_specs,
        out_specs=pl.BlockSpec((tm, tn), lambda i, j: (i, j)),
        out_shape=jax.ShapeDtypeStruct((m, n), F32),
        scratch_shapes=scratch,
        compiler_params=_cparams("parallel", "arbitrary"),
        name="inproj",
    )(*args)


def _rms(x, g):
    return x * lax.rsqrt(jnp.mean(x * x, axis=-1, keepdims=True) + NORM_EPS) * g


def _mla_prep_body(cq_ref, ckv_ref, qg_ref, kg_ref, wuq_ref, wukv_ref, qb_ref, ckvn_ref, kvb_ref):
    qb_ref[...] = _dot(_rms(cq_ref[...], qg_ref[...]).astype(BF16), wuq_ref[...])
    ckv = _rms(ckv_ref[...], kg_ref[...])
    ckvn_ref[...] = ckv
    kvb_ref[...] = _dot(ckv.astype(BF16), wukv_ref[...])


def _mla_prep(proj, q_norm_g, kv_norm_g, w_uq, w_ukv, *, tm):
    m = proj.shape[0]
    nq, nkv = w_uq.shape[1], w_ukv.shape[1]
    const = lambda i: (0, 0)
    return pl.pallas_call(
        _mla_prep_body,
        grid=(m // tm,),
        in_specs=[
            pl.BlockSpec((tm, B_Q_LORA), lambda i: (i, L0_CQ)),
            pl.BlockSpec((tm, B_KV_LORA), lambda i: (i, L0_CKV)),
            pl.BlockSpec((1, B_Q_LORA), const),
            pl.BlockSpec((1, B_KV_LORA), const),
            pl.BlockSpec((B_Q_LORA, nq), const),
            pl.BlockSpec((B_KV_LORA, nkv), const),
        ],
        out_specs=[
            pl.BlockSpec((tm, nq), lambda i: (i, 0)),
            pl.BlockSpec((tm, B_KV_LORA), lambda i: (i, 0)),
            pl.BlockSpec((tm, nkv), lambda i: (i, 0)),
        ],
        out_shape=[
            jax.ShapeDtypeStruct((m, nq), F32),
            jax.ShapeDtypeStruct((m, B_KV_LORA), F32),
            jax.ShapeDtypeStruct((m, nkv), F32),
        ],
        compiler_params=_cparams("parallel"),
        name="mla_prep",
    )(proj, proj, q_norm_g, kv_norm_g, w_uq, w_ukv)


def _matmul_body(x_ref, w_ref, o_ref):
    o_ref[...] = _dot(x_ref[...].astype(BF16), w_ref[...])


def _matmul(x, w, *, tm):
    m, k = x.shape
    n = w.shape[1]
    return pl.pallas_call(
        _matmul_body,
        grid=(m // tm,),
        in_specs=[pl.BlockSpec((tm, k), lambda i: (i, 0)), pl.BlockSpec((k, n), lambda i: (0, 0))],
        out_specs=pl.BlockSpec((tm, n), lambda i: (i, 0)),
        out_shape=jax.ShapeDtypeStruct((m, n), F32),
        compiler_params=_cparams("parallel"),
        name="matmul",
    )(x, w)


def _softmax_parts(scores):
    m = functools.reduce(jnp.maximum, [jnp.max(s, axis=-1, keepdims=True) for s in scores])
    ps = [jnp.exp2(s - m) for s in scores]
    denom = functools.reduce(jnp.add, [jnp.sum(p, axis=-1, keepdims=True) for p in ps])
    return ps, denom


def _diff_attn_body(*refs, dec, tq):
    if dec:
        (q_ref, k_ref, v_ref, kc_ref, vc_ref, gate_ref, cos_ref, sin_ref, lam_ref, sg_ref,
         o_ref, kb, vb, kcb, vcb) = refs
    else:
        q_ref, k_ref, v_ref, gate_ref, lam_ref, sg_ref, o_ref, kb, vb = refs
    qi = pl.program_id(1)

    @pl.when(qi == 0)
    def _():
        for h in range(A_HEADS):
            hs = slice(h * LANES, (h + 1) * LANES)
            kt = k_ref[0, :, hs]
            if dec:
                kt = _rope(kt, cos_ref[...], sin_ref[...])
                kcb[:, hs] = kc_ref[0, :, hs].astype(BF16)
                vcb[:, hs] = vc_ref[0, :, hs].astype(BF16)
            kb[:, hs] = kt.astype(BF16)
            vb[:, hs] = v_ref[0, :, hs].astype(BF16)

    lp = lam_ref[...]
    lam = (jnp.exp(jnp.sum(lp[0:1] * lp[1:2], keepdims=True))
           - jnp.exp(jnp.sum(lp[2:3] * lp[3:4], keepdims=True)) + LAM_INIT_L0)
    if dec:
        row0 = pl.multiple_of(qi * tq, tq)
        cq, sq = cos_ref[pl.ds(row0, tq), :], sin_ref[pl.ds(row0, tq), :]
    first = lax.broadcasted_iota(jnp.int32, (1, LANES), 1) < A_QK_DIM
    for h in range(A_HEADS):
        hs = slice(h * LANES, (h + 1) * LANES)
        qh = q_ref[0, :, hs]
        if dec:
            qh = _rope(qh, cq, sq)
        qh = qh * (A_QK_DIM ** -0.5 * LOG2_E)
        q1 = jnp.where(first, qh, 0.0).astype(BF16)
        q2 = jnp.where(first, 0.0, qh).astype(BF16)
        keys = [kb[:, hs]] + ([kcb[:, hs]] if dec else [])
        vals = [vb[:, hs]] + ([vcb[:, hs]] if dec else [])
        p1, l1 = _softmax_parts([_dot_nt(q1, kp) for kp in keys])
        p2, l2 = _softmax_parts([_dot_nt(q2, kp) for kp in keys])
        o1 = functools.reduce(jnp.add, [_dot(x.astype(BF16), vp) for x, vp in zip(p1, vals)])
        o2 = functools.reduce(jnp.add, [_dot(x.astype(BF16), vp) for x, vp in zip(p2, vals)])
        o = o1 * (1.0 / l1) - o2 * (lam / l2)
        o = _rms(o, sg_ref[...]) * (1.0 - LAM_INIT_L0)
        o_ref[0, :, hs] = (o * _silu(gate_ref[0, :, hs])).astype(BF16)


def _diff_attn(proj, ctx_k, ctx_v, cos, sin, diff_lambda, subln_g, *, tq):
    bsz, t_len, _ = proj.shape
    dec = ctx_k is not None
    w = A_HEADS * LANES
    full = lambda col: pl.BlockSpec((1, t_len, w), lambda b, i: (b, 0, col))
    in_specs = [pl.BlockSpec((1, tq, w), lambda b, i: (b, i, L0_AQ)), full(L0_AK), full(L0_AV)]
    args = [proj, proj, proj]
    scratch = [pltpu.VMEM((t_len, w), BF16), pltpu.VMEM((t_len, w), BF16)]
    if dec:
        p_len = ctx_k.shape[1]
        in_specs += [pl.BlockSpec((1, p_len, w), lambda b, i: (b, 0, 0))] * 2
        args += [ctx_k, ctx_v]
        scratch += [pltpu.VMEM((p_len, w), BF16), pltpu.VMEM((p_len, w), BF16)]
    in_specs.append(pl.BlockSpec((1, tq, w), lambda b, i: (b, i, L0_GATE_A)))
    args.append(proj)
    if dec:
        in_specs += [pl.BlockSpec((t_len, LANES), lambda b, i: (0, 0))] * 2
        args += [cos, sin]
    in_specs += [pl.BlockSpec((4, A_QK_DIM), lambda b, i: (0, 0)),
                 pl.BlockSpec((1, A_V_DIM), lambda b, i: (0, 0))]
    args += [diff_lambda, subln_g]
    return pl.pallas_call(
        functools.partial(_diff_attn_body, dec=dec, tq=tq),
        grid=(bsz, t_len // tq),
        in_specs=in_specs,
        out_specs=pl.BlockSpec((1, tq, w), lambda b, i: (b, i, 0)),
        out_shape=jax.ShapeDtypeStruct((bsz, t_len, w), BF16),
        scratch_shapes=scratch,
        compiler_params=_cparams("parallel", "arbitrary"),
        name="diff_attn",
    )(*args)


def _mla_attn_body(*refs, dec, tq):
    if dec:
        (qn_ref, qp_ref, kv_ref, kpe_ref, kvc_ref, kpec_ref, gate_ref, cos_ref, sin_ref,
         o_ref, kb, vb, kcb, vcb) = refs
    else:
        qn_ref, qp_ref, kv_ref, kpe_ref, gate_ref, o_ref, kb, vb = refs
    qi = pl.program_id(1)

    def stage(src_ref, kpe, k_dst, v_dst):
        for h in range(B_HEADS):
            k_dst[:, 2 * h * LANES:(2 * h + 1) * LANES] = src_ref[0, :, 2 * h * LANES:(2 * h + 1) * LANES].astype(BF16)
            k_dst[:, (2 * h + 1) * LANES:(2 * h + 2) * LANES] = kpe
            v_dst[:, h * LANES:(h + 1) * LANES] = src_ref[0, :, (2 * h + 1) * LANES:(2 * h + 2) * LANES].astype(BF16)

    @pl.when(qi == 0)
    def _():
        kp = kpe_ref[0]
        if dec:
            kp = _rope(kp, cos_ref[...], sin_ref[...])
            stage(kvc_ref, kpec_ref[0].astype(BF16), kcb, vcb)
        stage(kv_ref, kp.astype(BF16), kb, vb)

    if dec:
        row0 = pl.multiple_of(qi * tq, tq)
        cq, sq = cos_ref[pl.ds(row0, tq), :], sin_ref[pl.ds(row0, tq), :]
    scale = (B_NOPE + B_ROPE) ** -0.5 * LOG2_E
    first = lax.broadcasted_iota(jnp.int32, (1, LANES), 1) < B_ROPE
    for h in range(B_HEADS):
        hs = slice(h * LANES, (h + 1) * LANES)
        if h % 2 == 0:
            pair = qp_ref[0, :, (h // 2) * LANES:(h // 2 + 1) * LANES]
            if dec:
                pair = _rope(pair, cq, sq)
            pair = pair * scale
        qp = (jnp.where(first, pair, 0.0) if h % 2 == 0 else jnp.where(first, 0.0, pair)).astype(BF16)
        qn = (qn_ref[0, :, hs] * scale).astype(BF16)
        q = jnp.concatenate([qn, qp], axis=1)
        ks = slice(2 * h * LANES, (2 * h + 2) * LANES)
        scores = [_dot_nt(q, kb[:, ks])]
        vals = [vb[:, hs]]
        if dec:
            scores.append(_dot_nt(q, kcb[:, ks]))
            vals.append(vcb[:, hs])
        ps, denom = _softmax_parts(scores)
        o = functools.reduce(jnp.add, [_dot(p.astype(BF16), vp) for p, vp in zip(ps, vals)]) * (1.0 / denom)
        o_ref[0, :, hs] = (o * _silu(gate_ref[0, :, hs])).astype(BF16)


def _mla_attn(proj, q_b, kv_b, ctx_kv, ctx_kpe, cos, sin, *, tq):
    bsz, t_len, _ = proj.shape
    dec = ctx_kv is not None
    w = B_HEADS * LANES
    nkv = kv_b.shape[2]
    in_specs = [
        pl.BlockSpec((1, tq, w), lambda b, i: (b, i, 0)),
        pl.BlockSpec((1, tq, B_HEADS * B_ROPE), lambda b, i: (b, i, 2)),
        pl.BlockSpec((1, t_len, nkv), lambda b, i: (b, 0, 0)),
        pl.BlockSpec((1, t_len, LANES), lambda b, i: (b, 0, L0_KPE)),
    ]
    args = [q_b, q_b, kv_b, proj]
    scratch = [pltpu.VMEM((t_len, 2 * w), BF16), pltpu.VMEM((t_len, w), BF16)]
    if dec:
        p_len = ctx_kv.shape[1]
        in_specs += [pl.BlockSpec((1, p_len, nkv), lambda b, i: (b, 0, 0)),
                     pl.BlockSpec((1, p_len, LANES), lambda b, i: (b, 0, 0))]
        args += [ctx_kv, ctx_kpe]
        scratch += [pltpu.VMEM((p_len, 2 * w), BF16), pltpu.VMEM((p_len, w), BF16)]
    in_specs.append(pl.BlockSpec((1, tq, w), lambda b, i: (b, i, L0_GATE_B)))
    args.append(proj)
    if dec:
        in_specs += [pl.BlockSpec((t_len, LANES), lambda b, i: (0, 0))] * 2
        args += [cos, sin]
    return pl.pallas_call(
        functools.partial(_mla_attn_body, dec=dec, tq=tq),
        grid=(bsz, t_len // tq),
        in_specs=in_specs,
        out_specs=pl.BlockSpec((1, tq, w), lambda b, i: (b, i, 0)),
        out_shape=jax.ShapeDtypeStruct((bsz, t_len, w), BF16),
        scratch_shapes=scratch,
        compiler_params=_cparams("parallel", "arbitrary"),
        name="mla_attn",
    )(*args)


def _outproj_body(x_ref, ya_ref, yb_ref, w_ref, gate_ref, o_ref):
    half = ya_ref.shape[1]
    acc = _dot(ya_ref[...], w_ref[0:half, :]) + _dot(yb_ref[...], w_ref[half:, :])
    o_ref[...] = x_ref[...] + gate_ref[0] * acc


def _outproj(x, ya, yb, w, gate, *, rows_per_mod, tm):
    m, d = x.shape
    k = w.shape[0]
    return pl.pallas_call(
        _outproj_body,
        grid=(m // tm,),
        in_specs=[
            pl.BlockSpec((tm, d), lambda i: (i, 0)),
            pl.BlockSpec((tm, k // 2), lambda i: (i, 0)),
            pl.BlockSpec((tm, k // 2), lambda i: (i, 0)),
            pl.BlockSpec((k, d), lambda i: (0, 0), pipeline_mode=pl.Buffered(1)),
            pl.BlockSpec((1, 1, d), lambda i: ((i * tm) // rows_per_mod, 0, 0)),
        ],
        out_specs=pl.BlockSpec((tm, d), lambda i: (i, 0)),
        out_shape=jax.ShapeDtypeStruct((m, d), F32),
        compiler_params=_cparams("parallel"),
        name="outproj",
    )(x, ya, yb, w, gate)


def _relayout_body(main_ref, lora_ref, w_ref, bias_ref, o_ref, r_scr, q_scr):
    q = pl.program_id(2)

    def park(tile_of, prepare=None):
        for b in range(PER):
            if prepare is not None:
                prepare(b)
            for c in range(C_WIDTH // LANES):
                at = tile_of(b, c).T
                p0 = b * C_HEADS + 2 * c
                r_scr[pl.ds(p0 * PITCH, C_HEAD), :] = at[:C_HEAD]
                r_scr[pl.ds((p0 + 1) * PITCH, C_HEAD), :] = at[C_HEAD:]

    @pl.when(q < 3)
    def _():
        park(lambda b, c: main_ref[0, b, :, c * LANES:(c + 1) * LANES])

    expanded = lambda b, c: q_scr[:, c * LANES:(c + 1) * LANES]

    @pl.when((q == 3) | (q == 4))
    def _():
        def decay(b):
            u = -(bias_ref[0] + _dot(jnp.tanh(lora_ref[0, b]).astype(BF16), w_ref[0]))
            softplus = jnp.maximum(u, 0.0) + jnp.log(1.0 + jnp.exp(-jnp.abs(u)))
            q_scr[...] = jnp.exp(-jnp.exp(-softplus - 0.5))
        park(expanded, decay)

    @pl.when(q >= 5)
    def _():
        def iclr(b):
            q_scr[...] = 1.0 / (1.0 + jnp.exp(-(bias_ref[0] + _dot(lora_ref[0, b].astype(BF16), w_ref[0]))))
        park(expanded, iclr)

    for j in range(C_HEAD):
        m = r_scr[pl.ds(j, LANES, stride=PITCH), :]
        o_ref[0, 0, pl.ds(j, TT, stride=PITCH), :] = m.T
    zero = jnp.zeros((TT, LANES), F32)
    for j in range(C_HEAD, PITCH):
        o_ref[0, 0, pl.ds(j, TT, stride=PITCH), :] = zero


def _relayout(proj, lora_w, lora_b, bsz, t_len):
    g = bsz // PER
    proj4 = proj.reshape(g, PER, t_len, L1_COLS)
    lora0 = 4 * C_WIDTH // LANES
    lora_ix = lambda gi, ti, q: (jnp.clip(q - 3, 0, 3), 0, 0)
    out = pl.pallas_call(
        _relayout_body,
        grid=(g, t_len // TT, N_LANE_ARRAYS),
        in_specs=[
            pl.BlockSpec((1, PER, TT, C_WIDTH), lambda gi, ti, q: (gi, 0, ti, jnp.minimum(q, 2))),
            pl.BlockSpec((1, PER, TT, LANES), lambda gi, ti, q: (gi, 0, ti, lora0 + jnp.clip(q - 3, 0, 3))),
            pl.BlockSpec((1, LORA_PAD, C_WIDTH), lora_ix),
            pl.BlockSpec((1, 1, C_WIDTH), lora_ix),
        ],
        out_specs=pl.BlockSpec((1, 1, TT * PITCH, LANES), lambda gi, ti, q: (q, gi, ti, 0)),
        out_shape=jax.ShapeDtypeStruct((N_LANE_ARRAYS, g, t_len * PITCH, LANES), F32),
        scratch_shapes=[pltpu.VMEM((LANES * PITCH, LANES), F32), pltpu.VMEM((TT, C_WIDTH), F32)],
        compiler_params=_cparams("parallel", "parallel", "arbitrary"),
        name="lane_relayout",
    )(proj4, proj4, lora_w, lora_b)
    return out.reshape(N_LANE_ARRAYS, g, t_len, PITCH, LANES)


def _lane_tile(p):
    return jnp.tile(p.reshape(C_HEADS, C_HEAD).T, (1, PER))


def _scan_body(*refs, tc, reverse, has_prev):
    r_ref, k_ref, v_ref, w_ref, a_ref, kkp_ref, kap_ref, rkp_ref, lnw_ref, lnb_ref, s0_ref = refs[:11]
    rest = refs[11:]
    if has_prev:
        prev_ref, rest = rest[0], rest[1:]
    o_ref, sfin_ref, s_scr, kka_scr, kz_scr, nkk_scr, y_scr = rest
    c = pl.program_id(1)

    @pl.when(c == 0)
    def _():
        s_scr[...] = s0_ref[0]

    used = lambda ref: ref[0, 0, :, :C_HEAD, :]
    k = used(k_ref)
    a = used(a_ref)
    kk = k * kkp_ref[...]
    kk = kk * lax.rsqrt(jnp.maximum(jnp.sum(kk * kk, axis=1, keepdims=True), 1e-12))
    kz = k * (1.0 + (a - 1.0) * kap_ref[...])
    kz_scr[...] = kz
    kka_scr[...] = kk * a
    nkk_scr[...] = -kk
    bonus = jnp.sum(used(r_ref) * kz * rkp_ref[...], axis=1, keepdims=True) * used(v_ref)

    first = tc - 1 if reverse else 0
    sa0 = jnp.zeros((C_HEAD, LANES), F32)
    for j in range(C_HEAD):
        sa0 = sa0 + s_scr[j] * nkk_scr[first, j:j + 1, :]

    def step(i, sa):
        t = (tc - 1 - i) if reverse else i
        t_next = jnp.maximum(t - 1, 0) if reverse else jnp.minimum(t + 1, tc - 1)
        v = v_ref[0, 0, t, :C_HEAD, :]
        y = jnp.zeros((C_HEAD, LANES), F32)
        sa_next = jnp.zeros((C_HEAD, LANES), F32)
        for j in range(C_HEAD):
            s = s_scr[j] * w_ref[0, 0, t, j:j + 1, :] + sa * kka_scr[t, j:j + 1, :] + v * kz_scr[t, j:j + 1, :]
            s_scr[j] = s
            y = y + s * r_ref[0, 0, t, j:j + 1, :]
            sa_next = sa_next + s * nkk_scr[t_next, j:j + 1, :]
        y_scr[t] = y
        return sa_next

    lax.fori_loop(0, tc, step, sa0)

    y = y_scr[...]
    dev = y - jnp.mean(y, axis=1, keepdims=True)
    var = jnp.mean(dev * dev, axis=1, keepdims=True)
    out = dev * lax.rsqrt(var + GN_EPS) * lnw_ref[...] + lnb_ref[...] + bonus
    if has_prev:
        out = out + prev_ref[0, :, :C_HEAD, :]
    o_ref[0, :, :C_HEAD, :] = out
    o_ref[0, :, C_HEAD:, :] = jnp.zeros((tc, PITCH - C_HEAD, LANES), F32)

    @pl.when(c == pl.num_programs(1) - 1)
    def _():
        sfin_ref[0] = s_scr[...]


def _scan(z7, z, params, s0, prev, *, reverse, tc):
    _, g, t_len = z7.shape[:3]
    n_c = t_len // tc
    tix = (lambda ci: n_c - 1 - ci) if reverse else (lambda ci: ci)
    qblk = lambda q: pl.BlockSpec((1, 1, tc, PITCH, LANES), lambda gi, ci: (q, gi, tix(ci), 0, 0))
    tile = pl.BlockSpec((C_HEAD, LANES), lambda gi, ci: (0, 0))
    st = pl.BlockSpec((1, C_HEAD, C_HEAD, LANES), lambda gi, ci: (gi, 0, 0, 0))
    oblk = pl.BlockSpec((1, tc, PITCH, LANES), lambda gi, ci: (gi, tix(ci), 0, 0))
    in_specs = [qblk(0), qblk(1), qblk(2), qblk(3 + z), qblk(5 + z)] + [tile] * len(params) + [st]
    args = [z7] * 5 + list(params) + [s0]
    if prev is not None:
        in_specs.append(oblk)
        args.append(prev)
    chunk = pltpu.VMEM((tc, C_HEAD, LANES), F32)
    return pl.pallas_call(
        functools.partial(_scan_body, tc=tc, reverse=reverse, has_prev=prev is not None),
        grid=(g, n_c),
        in_specs=in_specs,
        out_specs=[oblk, st],
        out_shape=[jax.ShapeDtypeStruct((g, t_len, PITCH, LANES), F32), jax.ShapeDtypeStruct(s0.shape, F32)],
        scratch_shapes=[pltpu.VMEM((C_HEAD, C_HEAD, LANES), F32)] + [chunk] * 4,
        compiler_params=_cparams("parallel", "arbitrary"),
        name="wkv_scan",
    )(*args)


def _state_to_lanes(s):
    per = LANES // C_HEADS
    bsz = s.shape[0]
    s = s.reshape(bsz // per, per, C_HEADS, C_HEAD, C_HEAD)
    return s.transpose(0, 4, 3, 1, 2).reshape(bsz // per, C_HEAD, C_HEAD, LANES)


def _state_from_lanes(s):
    per = LANES // C_HEADS
    g = s.shape[0]
    s = s.reshape(g, C_HEAD, C_HEAD, per, C_HEADS)
    return s.transpose(0, 3, 4, 2, 1).reshape(g * per, C_HEADS, C_HEAD, C_HEAD)


def _rwkv_out_body(y_ref, x_ref, gate_ref, w_ref, mg_ref, fg_ref, o_ref, r_scr, y_scr):
    b = pl.program_id(2)

    @pl.when(b == 0)
    def _():
        for i in range(C_HEAD):
            m = y_ref[0, pl.ds(i, TT, stride=PITCH), :]
            r_scr[pl.ds(i, LANES, stride=PITCH), :] = m.T
        for bb in range(PER):
            for c in range(C_WIDTH // LANES):
                p0 = bb * C_HEADS + 2 * c
                pair = jnp.concatenate([r_scr[pl.ds(p0 * PITCH, C_HEAD), :],
                                        r_scr[pl.ds((p0 + 1) * PITCH, C_HEAD), :]], axis=0)
                y_scr[bb, :, c * LANES:(c + 1) * LANES] = pair.T

    yg = (y_scr[b] * _silu(gate_ref[0, 0])).astype(BF16)
    xn = x_ref[0, 0] + mg_ref[0] * _dot(yg, w_ref[...])
    o_ref[0, 0] = _rms(xn, fg_ref[...])


def _rwkv_out(x, y, proj, w, gate, final_g, *, bsz, t_len):
    d = x.shape[1]
    g = bsz // PER
    n_mod = gate.shape[0]
    tok = lambda gi, ti, b: (gi, b, ti, 0)
    const = lambda gi, ti, b: (0, 0)
    out = pl.pallas_call(
        _rwkv_out_body,
        grid=(g, t_len // TT, PER),
        in_specs=[
            pl.BlockSpec((1, TT * PITCH, LANES), lambda gi, ti, b: (gi, ti, 0)),
            pl.BlockSpec((1, 1, TT, d), tok),
            pl.BlockSpec((1, 1, TT, C_WIDTH), lambda gi, ti, b: (gi, b, ti, 3)),
            pl.BlockSpec((C_WIDTH, d), const, pipeline_mode=pl.Buffered(1)),
            pl.BlockSpec((1, 1, d), lambda gi, ti, b: ((gi * PER + b) % n_mod, 0, 0)),
            pl.BlockSpec((1, d), const),
        ],
        out_specs=pl.BlockSpec((1, 1, TT, d), tok),
        out_shape=jax.ShapeDtypeStruct((g, PER, t_len, d), F32),
        scratch_shapes=[pltpu.VMEM((LANES * PITCH, LANES), F32), pltpu.VMEM((PER, TT, C_WIDTH), F32)],
        compiler_params=_cparams("parallel", "parallel", "arbitrary"),
        name="rwkv_out",
    )(y.reshape(g, t_len * PITCH, LANES), x.reshape(g, PER, t_len, d), proj.reshape(g, PER, t_len, L1_COLS),
      w, gate, final_g)
    return out.reshape(bsz * t_len, d)


def _rope_tables(t_len):
    pos = jnp.arange(t_len)
    half = 16
    freqs = ROPE_BASE ** (-jnp.arange(half, dtype=F32) / half)
    ang_r = (pos // GRID_W).astype(F32)[:, None] * freqs
    ang_c = (pos % GRID_W).astype(F32)[:, None] * freqs
    cos = jnp.concatenate([jnp.cos(ang_r)] * 2 + [jnp.cos(ang_c)] * 2, axis=-1)
    sin = jnp.concatenate([-jnp.sin(ang_r), jnp.sin(ang_r), -jnp.sin(ang_c), jnp.sin(ang_c)], axis=-1)
    return jnp.tile(cos, (1, 2)), jnp.tile(sin, (1, 2))


def _l0_w_in_layout(w):
    d = w.shape[0]
    aq, ak, av, cq, ckv, kpe, gate = jnp.split(w.astype(BF16), [1024, 2048, 3072, 3584, 3840, 3904], axis=1)
    pad = jnp.zeros((d, L0_COLS - 6016), BF16)
    return jnp.concatenate([gate, aq, ak, av, cq, ckv, kpe, kpe, pad], axis=1)


def _l1_cols_layout(w):
    main, wd, ad = jnp.split(w, [4 * C_WIDTH, 4 * C_WIDTH + 2 * C_LORA], axis=1)
    pad = jnp.zeros((w.shape[0], LORA_PAD - C_LORA), w.dtype)
    parts = [main]
    for seg in (wd, ad):
        for z in range(2):
            parts += [seg[:, z * C_LORA:(z + 1) * C_LORA], pad]
    return jnp.concatenate(parts, axis=1)


def _layer0(x, mods, rows_per_mod, seq_len, weights, ctx, tables):
    w_in, w_out, diff_lambda, subln_g, q_norm_g, w_uq, kv_norm_g, w_ukv = weights
    shift, scale, gate, norm_g = mods
    m = x.shape[0]
    bsz = m // seq_len
    proj = _inproj(x, shift, scale, norm_g, w_in, None, rows_per_mod=rows_per_mod, seq_len=seq_len, tm=1024, tn=1024)
    q_b, ckv_n, kv_b = _mla_prep(proj, q_norm_g, kv_norm_g, w_uq, w_ukv, tm=512)
    proj3 = proj.reshape(bsz, seq_len, L0_COLS)
    q_b3 = q_b.reshape(bsz, seq_len, -1)
    kv_b3 = kv_b.reshape(bsz, seq_len, -1)
    if ctx is None:
        ya = _diff_attn(proj3, None, None, None, None, diff_lambda, subln_g, tq=256)
        yb = _mla_attn(proj3, q_b3, kv_b3, None, None, None, None, tq=256)
    else:
        k_ctx, v_ctx, ckv_ctx, kpe_ctx = ctx
        p_len = k_ctx.shape[1]
        cos, sin = tables
        kv_ctx = _matmul(ckv_ctx.reshape(bsz * p_len, B_KV_LORA), w_ukv, tm=512).reshape(bsz, p_len, -1)
        ya = _diff_attn(proj3, k_ctx.reshape(bsz, p_len, -1), v_ctx.reshape(bsz, p_len, -1), cos, sin,
                        diff_lambda, subln_g, tq=256)
        yb = _mla_attn(proj3, q_b3, kv_b3, kv_ctx, jnp.concatenate([kpe_ctx, kpe_ctx], axis=-1), cos, sin, tq=256)
    x_new = _outproj(x, ya.reshape(m, -1), yb.reshape(m, -1), w_out, gate, rows_per_mod=rows_per_mod, tm=512)
    return x_new, proj, ckv_n


def _layer1(x, mods, rows_per_mod, seq_len, weights, states, final_g):
    w_in, w_out, mu, lora_w, lora_b, lane_params = weights
    shift, scale, gate, norm_g = mods
    m = x.shape[0]
    bsz = m // seq_len
    proj = _inproj(x, shift, scale, norm_g, w_in, mu, rows_per_mod=rows_per_mod, seq_len=seq_len, tm=1024, tn=512)
    z7 = _relayout(proj, lora_w, lora_b, bsz, seq_len)
    y, finals = None, []
    for z in range(2):
        y, s_fin = _scan(z7, z, lane_params, states[z], y, reverse=(z == 1), tc=32)
        finals.append(_state_from_lanes(s_fin))
    out = _rwkv_out(x, y, proj, w_out, gate, final_g, bsz=bsz, t_len=seq_len)
    return out, finals


def kernel(x_prompt, x_sample, cache_l0_a_k, cache_l0_a_v, cache_l0_mla_ckv, cache_l0_mla_kpe, state_l1_fwd, state_l1_bwd, c, c_ctx, mod_w, mod_b, norm_g, final_norm_g, l0_w_in, l0_w_out, l0_diff_lambda, l0_subln_g, l0_q_norm_g, l0_w_uq, l0_kv_norm_g, l0_w_ukv, l1_w_in, l1_w_out, l1_mu, l1_w0, l1_w2, l1_a0, l1_a2, l1_k_k, l1_k_a, l1_r_k, l1_ln_w, l1_ln_b):
    d = D_MODEL
    bp, tp, _ = x_prompt.shape
    bs, ts, _ = x_sample.shape

    cond = jnp.concatenate([c_ctx[None, :], c, jnp.zeros((MOD_ROWS - 1 - bs, d), F32)], axis=0)
    mods = _adaln(cond, mod_w, mod_b)

    def mod_rows(layer, lo, hi):
        rows = mods[layer, lo:hi]
        shift, scale, gate = (rows[:, i * d:(i + 1) * d].reshape(hi - lo, 1, d) for i in range(3))
        return shift, scale, gate, norm_g[layer].reshape(1, d)

    w_uq = l0_w_uq.reshape(B_Q_LORA, B_HEADS, B_NOPE + B_ROPE)
    w_uq = jnp.concatenate([w_uq[:, :, :B_NOPE].reshape(B_Q_LORA, -1), w_uq[:, :, B_NOPE:].reshape(B_Q_LORA, -1)],
                           axis=1).astype(BF16)
    l0_weights = (_l0_w_in_layout(l0_w_in), l0_w_out.astype(BF16), l0_diff_lambda, l0_subln_g.reshape(1, -1),
                  l0_q_norm_g.reshape(1, -1), w_uq, l0_kv_norm_g.reshape(1, -1), l0_w_ukv.astype(BF16))
    lora_pad = jnp.zeros((2, LORA_PAD - C_LORA, C_WIDTH), F32)
    lora_w = jnp.concatenate([jnp.concatenate([l1_w2, lora_pad], axis=1),
                              jnp.concatenate([l1_a2, lora_pad], axis=1)], axis=0).astype(BF16)
    lora_b = jnp.concatenate([l1_w0, l1_a0], axis=0).reshape(4, 1, C_WIDTH)
    lane_params = [_lane_tile(p) for p in (l1_k_k, l1_k_a, l1_r_k, l1_ln_w, l1_ln_b)]
    l1_weights = (_l1_cols_layout(l1_w_in.astype(BF16)), l1_w_out.astype(BF16), _l1_cols_layout(l1_mu),
                  lora_w, lora_b, lane_params)
    tables = _rope_tables(ts)
    final_g = final_norm_g.reshape(1, d)

    xp = x_prompt.reshape(bp * tp, d)
    xs = x_sample.reshape(bs * ts, d)
    ctx0 = (cache_l0_a_k, cache_l0_a_v, cache_l0_mla_ckv, cache_l0_mla_kpe)

    xp1, proj_p, ckv_p = _layer0(xp, mod_rows(0, 0, 1), bp * tp, tp, l0_weights, None, None)
    xs1, _, _ = _layer0(xs, mod_rows(0, 1, 1 + bs), ts, ts, l0_weights, ctx0, tables)

    zero_state = jnp.zeros((bp // PER, C_HEAD, C_HEAD, LANES), F32)
    y_prompt, finals = _layer1(xp1, mod_rows(1, 0, 1), bp * tp, tp, l1_weights, (zero_state, zero_state), final_g)
    y_sample, _ = _layer1(xs1, mod_rows(1, 1, 1 + bs), ts, ts, l1_weights,
                          (_state_to_lanes(state_l1_fwd), _state_to_lanes(state_l1_bwd)), final_g)

    new_a_k = proj_p[:, 3072:4096].reshape(bp, tp, A_HEADS, 2, A_QK_DIM)
    new_a_v = proj_p[:, 4096:5120].reshape(bp, tp, A_HEADS, A_V_DIM)
    new_ckv = ckv_p.reshape(bp, tp, B_KV_LORA)
    new_kpe = proj_p[:, 5888:5888 + B_ROPE].reshape(bp, tp, B_ROPE)
    return (y_prompt.reshape(bp, tp, d), y_sample.reshape(bs, ts, d), new_a_k, new_a_v, new_ckv, new_kpe,
            finals[0], finals[1])
```

```python
import functools
import math

import jax
import jax.numpy as jnp
from jax import lax
from jax.experimental import pallas as pl
from jax.experimental.pallas import tpu as pltpu

F32 = jnp.float32
BF16 = jnp.bfloat16

D_MODEL = 2048
GRID_W = 64
ROPE_BASE = 10000.0
NORM_EPS = 1e-6
GN_EPS = 64e-5

A_HEADS = 8
A_QK_DIM = 64
A_V_DIM = 128
B_HEADS = 8
B_NOPE = 128
B_ROPE = 64
B_V = 128
B_Q_LORA = 512
B_KV_LORA = 256
AB_WIDTH = A_HEADS * A_V_DIM + B_HEADS * B_V
C_HEAD = 64
C_HEADS = D_MODEL // C_HEAD
C_WIDTH = C_HEADS * C_HEAD
C_LORA = 96
LAM_INIT_L0 = 0.8 - 0.6 * math.exp(-0.3 * 0)
LOG2_E = math.log2(math.e)

LANES = 128
SUBLANES = 8
LORA_PAD = 128
MOD_ROWS = 8
VMEM_LIMIT = 56 * 1024 * 1024

L0_COLS = 6144
L0_GATE_A, L0_GATE_B, L0_AQ, L0_AK, L0_AV = 0, 1, 2, 3, 4
L0_CQ = 10
L0_CKV = 22
L0_KPE = 46
L1_COLS = 4 * C_WIDTH + 4 * LORA_PAD
PER = LANES // C_HEADS
PITCH = 72
TT = LANES
N_LANE_ARRAYS = 7


def _cparams(*sem):
    return pltpu.CompilerParams(dimension_semantics=sem, vmem_limit_bytes=VMEM_LIMIT)


def _silu(x):
    return x / (1.0 + jnp.exp(-x))


def _dot(a, b):
    return jnp.dot(a, b, preferred_element_type=F32)


def _dot_nt(a, b):
    return lax.dot_general(a, b, (((1,), (1,)), ((), ())), preferred_element_type=F32)


def _rope(x, cos, sin):
    lane = lax.broadcasted_iota(jnp.int32, x.shape, 1)
    low = (lane % 32) < 16
    rot = jnp.where(low, pltpu.roll(x, LANES - 16, 1), pltpu.roll(x, 16, 1))
    return x * cos + rot * sin


def _adaln_body(c_ref, w_ref, b_ref, o_ref):
    s = _silu(c_ref[...]).astype(BF16)
    o_ref[0] = _dot(s, w_ref[0].astype(BF16)) + b_ref[0]


def _adaln(cond, mod_w, mod_b):
    depth, d, n = mod_w.shape
    tn = 1024
    return pl.pallas_call(
        _adaln_body,
        grid=(depth, n // tn),
        in_specs=[
            pl.BlockSpec((MOD_ROWS, d), lambda l, j: (0, 0)),
            pl.BlockSpec((1, d, tn), lambda l, j: (l, 0, j)),
            pl.BlockSpec((1, 1, tn), lambda l, j: (l, 0, j)),
        ],
        out_specs=pl.BlockSpec((1, MOD_ROWS, tn), lambda l, j: (l, 0, j)),
        out_shape=jax.ShapeDtypeStruct((depth, MOD_ROWS, n), F32),
        compiler_params=_cparams("parallel", "parallel"),
        name="adaln",
    )(cond, mod_w, mod_b.reshape(depth, 1, n))


def _inproj_body(x_ref, sh_ref, sc_ref, g_ref, w_ref, *rest, seq_len):
    if seq_len:
        mu_ref, o_ref, h_ref, p_scr = rest
    else:
        o_ref, h_ref = rest

    @pl.when(pl.program_id(1) == 0)
    def _():
        x = x_ref[...]
        y = x * lax.rsqrt(jnp.mean(x * x, axis=-1, keepdims=True) + NORM_EPS) * g_ref[...]
        h_ref[...] = (y * (1.0 + sc_ref[0]) + sh_ref[0]).astype(BF16)

    if not seq_len:
        o_ref[...] = _dot(h_ref[...], w_ref[...])
        return
    tm, tn = o_ref.shape
    zeros = jnp.zeros((SUBLANES, tn), F32)
    mu0, mu1 = mu_ref[0:1, :], mu_ref[1:2, :]
    keep = 1.0 - mu0 - mu1
    p_scr[pl.ds(0, SUBLANES), :] = zeros
    p_scr[pl.ds(SUBLANES, tm), :] = _dot(h_ref[...], w_ref[...])
    p_scr[pl.ds(SUBLANES + tm, SUBLANES), :] = zeros
    at = lambda r0, n, off: p_scr[pl.ds(SUBLANES + r0 + off, n), :]
    o_ref[...] = at(0, tm, 0) * keep + at(0, tm, -1) * mu0 + at(0, tm, 1) * mu1
    edge = lax.broadcasted_iota(jnp.int32, (2 * SUBLANES, 1), 0)
    for s in range(1, tm // seq_len):
        r0 = s * seq_len - SUBLANES
        prev = jnp.where(edge == SUBLANES, 0.0, at(r0, 2 * SUBLANES, -1))
        nxt = jnp.where(edge == SUBLANES - 1, 0.0, at(r0, 2 * SUBLANES, 1))
        o_ref[pl.ds(r0, 2 * SUBLANES), :] = at(r0, 2 * SUBLANES, 0) * keep + prev * mu0 + nxt * mu1


def _inproj(x, shift, scale, g, w, mu, *, rows_per_mod, seq_len, tm, tn):
    m, d = x.shape
    n = w.shape[1]
    mod_map = lambda i, j: ((i * tm) // rows_per_mod, 0, 0)
    in_specs = [
        pl.BlockSpec((tm, d), lambda i, j: (i, 0)),
        pl.BlockSpec((1, 1, d), mod_map),
        pl.BlockSpec((1, 1, d), mod_map),
        pl.BlockSpec((1, d), lambda i, j: (0, 0)),
        pl.BlockSpec((d, tn), lambda i, j: (0, j)),
    ]
    args = [x, shift, scale, g, w]
    if mu is not None:
        in_specs.append(pl.BlockSpec((2, tn), lambda i, j: (0, j)))
        args.append(mu)
    scratch = [pltpu.VMEM((tm, d), BF16)]
    if mu is not None:
        scratch.append(pltpu.VMEM((tm + 2 * SUBLANES, tn), F32))
    return pl.pallas_call(
        functools.partial(_inproj_body, seq_len=seq_len if mu is not None else 0),
        grid=(m // tm, n // tn),
        in_specs=in_specs,
        out_specs=pl.BlockSpec((tm, tn), lambda i, j: (i, j)),
        out_shape=jax.ShapeDtypeStruct((m, n), F32),
        scratch_shapes=scratch,
        compiler_params=_cparams("parallel", "arbitrary"),
        name="inproj",
    )(*args)


def _rms(x, g):
    return x * lax.rsqrt(jnp.mean(x * x, axis=-1, keepdims=True) + NORM_EPS) * g


def _mla_prep_body(cq_ref, ckv_ref, qg_ref, kg_ref, wuq_ref, wukv_ref, qb_ref, ckvn_ref, kvb_ref):
    qb_ref[...] = _dot(_rms(cq_ref[...], qg_ref[...]).astype(BF16), wuq_ref[...])
    ckv = _rms(ckv_ref[...], kg_ref[...])
    ckvn_ref[...] = ckv
    kvb_ref[...] = _dot(ckv.astype(BF16), wukv_ref[...]).astype(BF16)


def _mla_prep(proj, q_norm_g, kv_norm_g, w_uq, w_ukv, *, tm):
    m = proj.shape[0]
    nq, nkv = w_uq.shape[1], w_ukv.shape[1]
    const = lambda i: (0, 0)
    return pl.pallas_call(
        _mla_prep_body,
        grid=(m // tm,),
        in_specs=[
            pl.BlockSpec((tm, B_Q_LORA), lambda i: (i, L0_CQ)),
            pl.BlockSpec((tm, B_KV_LORA), lambda i: (i, L0_CKV)),
            pl.BlockSpec((1, B_Q_LORA), const),
            pl.BlockSpec((1, B_KV_LORA), const),
            pl.BlockSpec((B_Q_LORA, nq), const),
            pl.BlockSpec((B_KV_LORA, nkv), const),
        ],
        out_specs=[
            pl.BlockSpec((tm, nq), lambda i: (i, 0)),
            pl.BlockSpec((tm, B_KV_LORA), lambda i: (i, 0)),
            pl.BlockSpec((tm, nkv), lambda i: (i, 0)),
        ],
        out_shape=[
            jax.ShapeDtypeStruct((m, nq), F32),
            jax.ShapeDtypeStruct((m, B_KV_LORA), F32),
            jax.ShapeDtypeStruct((m, nkv), BF16),
        ],
        compiler_params=_cparams("parallel"),
        name="mla_prep",
    )(proj, proj, q_norm_g, kv_norm_g, w_uq, w_ukv)


def _matmul_body(x_ref, w_ref, o_ref):
    o_ref[...] = _dot(x_ref[...].astype(BF16), w_ref[...]).astype(o_ref.dtype)


def _matmul(x, w, *, tm):
    m, k = x.shape
    n = w.shape[1]
    return pl.pallas_call(
        _matmul_body,
        grid=(m // tm,),
        in_specs=[pl.BlockSpec((tm, k), lambda i: (i, 0)), pl.BlockSpec((k, n), lambda i: (0, 0))],
        out_specs=pl.BlockSpec((tm, n), lambda i: (i, 0)),
        out_shape=jax.ShapeDtypeStruct((m, n), BF16),
        compiler_params=_cparams("parallel"),
        name="matmul",
    )(x, w)


def _softmax_parts(scores):
    m = functools.reduce(jnp.maximum, [jnp.max(s, axis=-1, keepdims=True) for s in scores])
    ps = [jnp.exp2(s - m) for s in scores]
    denom = functools.reduce(jnp.add, [jnp.sum(p, axis=-1, keepdims=True) for p in ps])
    return ps, denom


def _diff_attn_body(*refs, dec, tq):
    if dec:
        (q_ref, k_ref, v_ref, kc_ref, vc_ref, gate_ref, cos_ref, sin_ref, lam_ref, sg_ref,
         o_ref, kb, vb, kcb, vcb) = refs
    else:
        q_ref, k_ref, v_ref, gate_ref, lam_ref, sg_ref, o_ref, kb, vb = refs
    qi = pl.program_id(1)

    @pl.when(qi == 0)
    def _():
        for h in range(A_HEADS):
            hs = slice(h * LANES, (h + 1) * LANES)
            kt = k_ref[0, :, hs]
            if dec:
                kt = _rope(kt, cos_ref[...], sin_ref[...])
                kcb[:, hs] = kc_ref[0, :, hs].astype(BF16)
                vcb[:, hs] = vc_ref[0, :, hs].astype(BF16)
            kb[:, hs] = kt.astype(BF16)
            vb[:, hs] = v_ref[0, :, hs].astype(BF16)

    lp = lam_ref[...]
    lam = (jnp.exp(jnp.sum(lp[0:1] * lp[1:2], keepdims=True))
           - jnp.exp(jnp.sum(lp[2:3] * lp[3:4], keepdims=True)) + LAM_INIT_L0)
    if dec:
        row0 = pl.multiple_of(qi * tq, tq)
        cq, sq = cos_ref[pl.ds(row0, tq), :], sin_ref[pl.ds(row0, tq), :]
    first = lax.broadcasted_iota(jnp.int32, (1, LANES), 1) < A_QK_DIM
    for h in range(A_HEADS):
        hs = slice(h * LANES, (h + 1) * LANES)
        qh = q_ref[0, :, hs]
        if dec:
            qh = _rope(qh, cq, sq)
        qh = qh * (A_QK_DIM ** -0.5 * LOG2_E)
        q1 = jnp.where(first, qh, 0.0).astype(BF16)
        q2 = jnp.where(first, 0.0, qh).astype(BF16)
        keys = [kb[:, hs]] + ([kcb[:, hs]] if dec else [])
        vals = [vb[:, hs]] + ([vcb[:, hs]] if dec else [])
        p1, l1 = _softmax_parts([_dot_nt(q1, kp) for kp in keys])
        p2, l2 = _softmax_parts([_dot_nt(q2, kp) for kp in keys])
        o1 = functools.reduce(jnp.add, [_dot(x.astype(BF16), vp) for x, vp in zip(p1, vals)])
        o2 = functools.reduce(jnp.add, [_dot(x.astype(BF16), vp) for x, vp in zip(p2, vals)])
        o = o1 * (1.0 / l1) - o2 * (lam / l2)
        o = _rms(o, sg_ref[...]) * (1.0 - LAM_INIT_L0)
        o_ref[0, :, hs] = (o * _silu(gate_ref[0, :, hs])).astype(BF16)


def _diff_attn(proj, ctx_k, ctx_v, cos, sin, diff_lambda, subln_g, *, tq):
    bsz, t_len, _ = proj.shape
    dec = ctx_k is not None
    w = A_HEADS * LANES
    full = lambda col: pl.BlockSpec((1, t_len, w), lambda b, i: (b, 0, col))
    in_specs = [pl.BlockSpec((1, tq, w), lambda b, i: (b, i, L0_AQ)), full(L0_AK), full(L0_AV)]
    args = [proj, proj, proj]
    scratch = [pltpu.VMEM((t_len, w), BF16), pltpu.VMEM((t_len, w), BF16)]
    if dec:
        p_len = ctx_k.shape[1]
        in_specs += [pl.BlockSpec((1, p_len, w), lambda b, i: (b, 0, 0))] * 2
        args += [ctx_k, ctx_v]
        scratch += [pltpu.VMEM((p_len, w), BF16), pltpu.VMEM((p_len, w), BF16)]
    in_specs.append(pl.BlockSpec((1, tq, w), lambda b, i: (b, i, L0_GATE_A)))
    args.append(proj)
    if dec:
        in_specs += [pl.BlockSpec((t_len, LANES), lambda b, i: (0, 0))] * 2
        args += [cos, sin]
    in_specs += [pl.BlockSpec((4, A_QK_DIM), lambda b, i: (0, 0)),
                 pl.BlockSpec((1, A_V_DIM), lambda b, i: (0, 0))]
    args += [diff_lambda, subln_g]
    return pl.pallas_call(
        functools.partial(_diff_attn_body, dec=dec, tq=tq),
        grid=(bsz, t_len // tq),
        in_specs=in_specs,
        out_specs=pl.BlockSpec((1, tq, w), lambda b, i: (b, i, 0)),
        out_shape=jax.ShapeDtypeStruct((bsz, t_len, w), BF16),
        scratch_shapes=scratch,
        compiler_params=_cparams("parallel", "arbitrary"),
        name="diff_attn",
    )(*args)


def _mla_attn_body(*refs, dec, tq):
    if dec:
        (qn_ref, qp_ref, kv_ref, kpe_ref, kvc_ref, kpec_ref, gate_ref, cos_ref, sin_ref,
         o_ref, kb, vb, kcb, vcb) = refs
    else:
        qn_ref, qp_ref, kv_ref, kpe_ref, gate_ref, o_ref, kb, vb = refs
    qi = pl.program_id(1)

    def stage(src_ref, kpe, k_dst, v_dst):
        for h in range(B_HEADS):
            k_dst[:, 2 * h * LANES:(2 * h + 1) * LANES] = src_ref[0, :, 2 * h * LANES:(2 * h + 1) * LANES].astype(BF16)
            k_dst[:, (2 * h + 1) * LANES:(2 * h + 2) * LANES] = kpe
            v_dst[:, h * LANES:(h + 1) * LANES] = src_ref[0, :, (2 * h + 1) * LANES:(2 * h + 2) * LANES].astype(BF16)

    @pl.when(qi == 0)
    def _():
        kp = kpe_ref[0]
        if dec:
            kp = _rope(kp, cos_ref[...], sin_ref[...])
            stage(kvc_ref, kpec_ref[0].astype(BF16), kcb, vcb)
        stage(kv_ref, kp.astype(BF16), kb, vb)

    if dec:
        row0 = pl.multiple_of(qi * tq, tq)
        cq, sq = cos_ref[pl.ds(row0, tq), :], sin_ref[pl.ds(row0, tq), :]
    scale = (B_NOPE + B_ROPE) ** -0.5 * LOG2_E
    first = lax.broadcasted_iota(jnp.int32, (1, LANES), 1) < B_ROPE
    for h in range(B_HEADS):
        hs = slice(h * LANES, (h + 1) * LANES)
        if h % 2 == 0:
            pair = qp_ref[0, :, (h // 2) * LANES:(h // 2 + 1) * LANES]
            if dec:
                pair = _rope(pair, cq, sq)
            pair = pair * scale
        qp = (jnp.where(first, pair, 0.0) if h % 2 == 0 else jnp.where(first, 0.0, pair)).astype(BF16)
        qn = (qn_ref[0, :, hs] * scale).astype(BF16)
        q = jnp.concatenate([qn, qp], axis=1)
        ks = slice(2 * h * LANES, (2 * h + 2) * LANES)
        scores = [_dot_nt(q, kb[:, ks])]
        vals = [vb[:, hs]]
        if dec:
            scores.append(_dot_nt(q, kcb[:, ks]))
            vals.append(vcb[:, hs])
        ps, denom = _softmax_parts(scores)
        o = functools.reduce(jnp.add, [_dot(p.astype(BF16), vp) for p, vp in zip(ps, vals)]) * (1.0 / denom)
        o_ref[0, :, hs] = (o * _silu(gate_ref[0, :, hs])).astype(BF16)


def _mla_attn(proj, q_b, kv_b, ctx_kv, ctx_kpe, cos, sin, *, tq):
    bsz, t_len, _ = proj.shape
    dec = ctx_kv is not None
    w = B_HEADS * LANES
    nkv = kv_b.shape[2]
    in_specs = [
        pl.BlockSpec((1, tq, w), lambda b, i: (b, i, 0)),
        pl.BlockSpec((1, tq, B_HEADS * B_ROPE), lambda b, i: (b, i, 2)),
        pl.BlockSpec((1, t_len, nkv), lambda b, i: (b, 0, 0)),
        pl.BlockSpec((1, t_len, LANES), lambda b, i: (b, 0, L0_KPE)),
    ]
    args = [q_b, q_b, kv_b, proj]
    scratch = [pltpu.VMEM((t_len, 2 * w), BF16), pltpu.VMEM((t_len, w), BF16)]
    if dec:
        p_len = ctx_kv.shape[1]
        in_specs += [pl.BlockSpec((1, p_len, nkv), lambda b, i: (b, 0, 0)),
                     pl.BlockSpec((1, p_len, LANES), lambda b, i: (b, 0, 0))]
        args += [ctx_kv, ctx_kpe]
        scratch += [pltpu.VMEM((p_len, 2 * w), BF16), pltpu.VMEM((p_len, w), BF16)]
    in_specs.append(pl.BlockSpec((1, tq, w), lambda b, i: (b, i, L0_GATE_B)))
    args.append(proj)
    if dec:
        in_specs += [pl.BlockSpec((t_len, LANES), lambda b, i: (0, 0))] * 2
        args += [cos, sin]
    return pl.pallas_call(
        functools.partial(_mla_attn_body, dec=dec, tq=tq),
        grid=(bsz, t_len // tq),
        in_specs=in_specs,
        out_specs=pl.BlockSpec((1, tq, w), lambda b, i: (b, i, 0)),
        out_shape=jax.ShapeDtypeStruct((bsz, t_len, w), BF16),
        scratch_shapes=scratch,
        compiler_params=_cparams("parallel", "arbitrary"),
        name="mla_attn",
    )(*args)


def _outproj_body(x_ref, ya_ref, yb_ref, w_ref, gate_ref, o_ref):
    half = ya_ref.shape[1]
    acc = _dot(ya_ref[...], w_ref[0:half, :]) + _dot(yb_ref[...], w_ref[half:, :])
    o_ref[...] = x_ref[...] + gate_ref[0] * acc


def _outproj(x, ya, yb, w, gate, *, rows_per_mod, tm):
    m, d = x.shape
    k = w.shape[0]
    return pl.pallas_call(
        _outproj_body,
        grid=(m // tm,),
        in_specs=[
            pl.BlockSpec((tm, d), lambda i: (i, 0)),
            pl.BlockSpec((tm, k // 2), lambda i: (i, 0)),
            pl.BlockSpec((tm, k // 2), lambda i: (i, 0)),
            pl.BlockSpec((k, d), lambda i: (0, 0), pipeline_mode=pl.Buffered(1)),
            pl.BlockSpec((1, 1, d), lambda i: ((i * tm) // rows_per_mod, 0, 0)),
        ],
        out_specs=pl.BlockSpec((tm, d), lambda i: (i, 0)),
        out_shape=jax.ShapeDtypeStruct((m, d), F32),
        compiler_params=_cparams("parallel"),
        name="outproj",
    )(x, ya, yb, w, gate)


def _relayout_body(main_ref, lora_ref, w_ref, bias_ref, o_ref, r_scr, q_scr):
    q = pl.program_id(2)

    def park(tile_of, prepare=None):
        for b in range(PER):
            if prepare is not None:
                prepare(b)
            for c in range(C_WIDTH // LANES):
                at = tile_of(b, c).T
                p0 = b * C_HEADS + 2 * c
                r_scr[pl.ds(p0 * PITCH, C_HEAD), :] = at[:C_HEAD]
                r_scr[pl.ds((p0 + 1) * PITCH, C_HEAD), :] = at[C_HEAD:]

    @pl.when(q < 3)
    def _():
        park(lambda b, c: main_ref[0, b, :, c * LANES:(c + 1) * LANES])

    expanded = lambda b, c: q_scr[:, c * LANES:(c + 1) * LANES]

    @pl.when((q == 3) | (q == 4))
    def _():
        def decay(b):
            u = -(bias_ref[0] + _dot(jnp.tanh(lora_ref[0, b]).astype(BF16), w_ref[0]))
            softplus = jnp.maximum(u, 0.0) + jnp.log(1.0 + jnp.exp(-jnp.abs(u)))
            q_scr[...] = jnp.exp(-jnp.exp(-softplus - 0.5))
        park(expanded, decay)

    @pl.when(q >= 5)
    def _():
        def iclr(b):
            q_scr[...] = 1.0 / (1.0 + jnp.exp(-(bias_ref[0] + _dot(lora_ref[0, b].astype(BF16), w_ref[0]))))
        park(expanded, iclr)

    for j in range(C_HEAD):
        m = r_scr[pl.ds(j, LANES, stride=PITCH), :]
        o_ref[0, 0, pl.ds(j, TT, stride=PITCH), :] = m.T
    zero = jnp.zeros((TT, LANES), F32)
    for j in range(C_HEAD, PITCH):
        o_ref[0, 0, pl.ds(j, TT, stride=PITCH), :] = zero


def _relayout(proj, lora_w, lora_b, bsz, t_len):
    g = bsz // PER
    proj4 = proj.reshape(g, PER, t_len, L1_COLS)
    lora0 = 4 * C_WIDTH // LANES
    lora_ix = lambda gi, ti, q: (jnp.clip(q - 3, 0, 3), 0, 0)
    out = pl.pallas_call(
        _relayout_body,
        grid=(g, t_len // TT, N_LANE_ARRAYS),
        in_specs=[
            pl.BlockSpec((1, PER, TT, C_WIDTH), lambda gi, ti, q: (gi, 0, ti, jnp.minimum(q, 2))),
            pl.BlockSpec((1, PER, TT, LANES), lambda gi, ti, q: (gi, 0, ti, lora0 + jnp.clip(q - 3, 0, 3))),
            pl.BlockSpec((1, LORA_PAD, C_WIDTH), lora_ix),
            pl.BlockSpec((1, 1, C_WIDTH), lora_ix),
        ],
        out_specs=pl.BlockSpec((1, 1, TT * PITCH, LANES), lambda gi, ti, q: (q, gi, ti, 0)),
        out_shape=jax.ShapeDtypeStruct((N_LANE_ARRAYS, g, t_len * PITCH, LANES), F32),
        scratch_shapes=[pltpu.VMEM((LANES * PITCH, LANES), F32), pltpu.VMEM((TT, C_WIDTH), F32)],
        compiler_params=_cparams("parallel", "parallel", "arbitrary"),
        name="lane_relayout",
    )(proj4, proj4, lora_w, lora_b)
    return out.reshape(N_LANE_ARRAYS, g, t_len, PITCH, LANES)


def _lane_tile(p):
    return jnp.tile(p.reshape(C_HEADS, C_HEAD).T, (1, PER))


def _scan_body(*refs, tc, reverse, has_prev):
    r_ref, k_ref, v_ref, w_ref, a_ref, kkp_ref, kap_ref, rkp_ref, lnw_ref, lnb_ref, s0_ref = refs[:11]
    rest = refs[11:]
    if has_prev:
        prev_ref, rest = rest[0], rest[1:]
    o_ref, sfin_ref, s_scr, kka_scr, kz_scr, nkk_scr, y_scr = rest
    c = pl.program_id(1)

    @pl.when(c == 0)
    def _():
        s_scr[...] = s0_ref[0]

    used = lambda ref: ref[0, 0, :, :C_HEAD, :]
    k = used(k_ref)
    a = used(a_ref)
    kk = k * kkp_ref[...]
    kk = kk * lax.rsqrt(jnp.maximum(jnp.sum(kk * kk, axis=1, keepdims=True), 1e-12))
    kz = k * (1.0 + (a - 1.0) * kap_ref[...])
    kz_scr[...] = kz
    kka_scr[...] = kk * a
    nkk_scr[...] = -kk
    bonus = jnp.sum(used(r_ref) * kz * rkp_ref[...], axis=1, keepdims=True) * used(v_ref)

    first = tc - 1 if reverse else 0
    sa0 = jnp.zeros((C_HEAD, LANES), F32)
    for j in range(C_HEAD):
        sa0 = sa0 + s_scr[j] * nkk_scr[first, j:j + 1, :]

    def step(i, sa):
        t = (tc - 1 - i) if reverse else i
        t_next = jnp.maximum(t - 1, 0) if reverse else jnp.minimum(t + 1, tc - 1)
        v = v_ref[0, 0, t, :C_HEAD, :]
        y = jnp.zeros((C_HEAD, LANES), F32)
        sa_next = jnp.zeros((C_HEAD, LANES), F32)
        for j in range(C_HEAD):
            s = s_scr[j] * w_ref[0, 0, t, j:j + 1, :] + sa * kka_scr[t, j:j + 1, :] + v * kz_scr[t, j:j + 1, :]
            s_scr[j] = s
            y = y + s * r_ref[0, 0, t, j:j + 1, :]
            sa_next = sa_next + s * nkk_scr[t_next, j:j + 1, :]
        y_scr[t] = y
        return sa_next

    lax.fori_loop(0, tc, step, sa0)

    y = y_scr[...]
    dev = y - jnp.mean(y, axis=1, keepdims=True)
    var = jnp.mean(dev * dev, axis=1, keepdims=True)
    out = dev * lax.rsqrt(var + GN_EPS) * lnw_ref[...] + lnb_ref[...] + bonus
    if has_prev:
        out = out + prev_ref[0, :, :C_HEAD, :]
    o_ref[0, :, :C_HEAD, :] = out
    o_ref[0, :, C_HEAD:, :] = jnp.zeros((tc, PITCH - C_HEAD, LANES), F32)

    @pl.when(c == pl.num_programs(1) - 1)
    def _():
        sfin_ref[0] = s_scr[...]


def _scan(z7, z, params, s0, prev, *, reverse, tc):
    _, g, t_len = z7.shape[:3]
    n_c = t_len // tc
    tix = (lambda ci: n_c - 1 - ci) if reverse else (lambda ci: ci)
    qblk = lambda q: pl.BlockSpec((1, 1, tc, PITCH, LANES), lambda gi, ci: (q, gi, tix(ci), 0, 0))
    tile = pl.BlockSpec((C_HEAD, LANES), lambda gi, ci: (0, 0))
    st = pl.BlockSpec((1, C_HEAD, C_HEAD, LANES), lambda gi, ci: (gi, 0, 0, 0))
    oblk = pl.BlockSpec((1, tc, PITCH, LANES), lambda gi, ci: (gi, tix(ci), 0, 0))
    in_specs = [qblk(0), qblk(1), qblk(2), qblk(3 + z), qblk(5 + z)] + [tile] * len(params) + [st]
    args = [z7] * 5 + list(params) + [s0]
    if prev is not None:
        in_specs.append(oblk)
        args.append(prev)
    chunk = pltpu.VMEM((tc, C_HEAD, LANES), F32)
    return pl.pallas_call(
        functools.partial(_scan_body, tc=tc, reverse=reverse, has_prev=prev is not None),
        grid=(g, n_c),
        in_specs=in_specs,
        out_specs=[oblk, st],
        out_shape=[jax.ShapeDtypeStruct((g, t_len, PITCH, LANES), F32), jax.ShapeDtypeStruct(s0.shape, F32)],
        scratch_shapes=[pltpu.VMEM((C_HEAD, C_HEAD, LANES), F32)] + [chunk] * 4,
        compiler_params=_cparams("parallel", "arbitrary"),
        name="wkv_scan",
    )(*args)


def _state_to_lanes(s):
    per = LANES // C_HEADS
    bsz = s.shape[0]
    s = s.reshape(bsz // per, per, C_HEADS, C_HEAD, C_HEAD)
    return s.transpose(0, 4, 3, 1, 2).reshape(bsz // per, C_HEAD, C_HEAD, LANES)


def _state_from_lanes(s):
    per = LANES // C_HEADS
    g = s.shape[0]
    s = s.reshape(g, C_HEAD, C_HEAD, per, C_HEADS)
    return s.transpose(0, 3, 4, 2, 1).reshape(g * per, C_HEADS, C_HEAD, C_HEAD)


def _rwkv_out_body(y_ref, x_ref, gate_ref, w_ref, mg_ref, fg_ref, o_ref, r_scr, y_scr):
    b = pl.program_id(2)

    @pl.when(b == 0)
    def _():
        for i in range(C_HEAD):
            m = y_ref[0, pl.ds(i, TT, stride=PITCH), :]
            r_scr[pl.ds(i, LANES, stride=PITCH), :] = m.T
        for bb in range(PER):
            for c in range(C_WIDTH // LANES):
                p0 = bb * C_HEADS + 2 * c
                pair = jnp.concatenate([r_scr[pl.ds(p0 * PITCH, C_HEAD), :],
                                        r_scr[pl.ds((p0 + 1) * PITCH, C_HEAD), :]], axis=0)
                y_scr[bb, :, c * LANES:(c + 1) * LANES] = pair.T

    yg = (y_scr[b] * _silu(gate_ref[0, 0])).astype(BF16)
    xn = x_ref[0, 0] + mg_ref[0] * _dot(yg, w_ref[...])
    o_ref[0, 0] = _rms(xn, fg_ref[...])


def _rwkv_out(x, y, proj, w, gate, final_g, *, bsz, t_len):
    d = x.shape[1]
    g = bsz // PER
    n_mod = gate.shape[0]
    tok = lambda gi, ti, b: (gi, b, ti, 0)
    const = lambda gi, ti, b: (0, 0)
    out = pl.pallas_call(
        _rwkv_out_body,
        grid=(g, t_len // TT, PER),
        in_specs=[
            pl.BlockSpec((1, TT * PITCH, LANES), lambda gi, ti, b: (gi, ti, 0)),
            pl.BlockSpec((1, 1, TT, d), tok),
            pl.BlockSpec((1, 1, TT, C_WIDTH), lambda gi, ti, b: (gi, b, ti, 3)),
            pl.BlockSpec((C_WIDTH, d), const, pipeline_mode=pl.Buffered(1)),
            pl.BlockSpec((1, 1, d), lambda gi, ti, b: ((gi * PER + b) % n_mod, 0, 0)),
            pl.BlockSpec((1, d), const),
        ],
        out_specs=pl.BlockSpec((1, 1, TT, d), tok),
        out_shape=jax.ShapeDtypeStruct((g, PER, t_len, d), F32),
        scratch_shapes=[pltpu.VMEM((LANES * PITCH, LANES), F32), pltpu.VMEM((PER, TT, C_WIDTH), F32)],
        compiler_params=_cparams("parallel", "parallel", "arbitrary"),
        name="rwkv_out",
    )(y.reshape(g, t_len * PITCH, LANES), x.reshape(g, PER, t_len, d), proj.reshape(g, PER, t_len, L1_COLS),
      w, gate, final_g)
    return out.reshape(bsz * t_len, d)


def _rope_tables(t_len):
    pos = jnp.arange(t_len)
    half = 16
    freqs = ROPE_BASE ** (-jnp.arange(half, dtype=F32) / half)
    ang_r = (pos // GRID_W).astype(F32)[:, None] * freqs
    ang_c = (pos % GRID_W).astype(F32)[:, None] * freqs
    cos = jnp.concatenate([jnp.cos(ang_r)] * 2 + [jnp.cos(ang_c)] * 2, axis=-1)
    sin = jnp.concatenate([-jnp.sin(ang_r), jnp.sin(ang_r), -jnp.sin(ang_c), jnp.sin(ang_c)], axis=-1)
    return jnp.tile(cos, (1, 2)), jnp.tile(sin, (1, 2))


def _l0_w_in_layout(w):
    d = w.shape[0]
    aq, ak, av, cq, ckv, kpe, gate = jnp.split(w.astype(BF16), [1024, 2048, 3072, 3584, 3840, 3904], axis=1)
    pad = jnp.zeros((d, L0_COLS - 6016), BF16)
    return jnp.concatenate([gate, aq, ak, av, cq, ckv, kpe, kpe, pad], axis=1)


def _l1_cols_layout(w):
    main, wd, ad = jnp.split(w, [4 * C_WIDTH, 4 * C_WIDTH + 2 * C_LORA], axis=1)
    pad = jnp.zeros((w.shape[0], LORA_PAD - C_LORA), w.dtype)
    parts = [main]
    for seg in (wd, ad):
        for z in range(2):
            parts += [seg[:, z * C_LORA:(z + 1) * C_LORA], pad]
    return jnp.concatenate(parts, axis=1)


def _layer0(x, mods, rows_per_mod, seq_len, weights, ctx, tables):
    w_in, w_out, diff_lambda, subln_g, q_norm_g, w_uq, kv_norm_g, w_ukv = weights
    shift, scale, gate, norm_g = mods
    m = x.shape[0]
    bsz = m // seq_len
    proj = _inproj(x, shift, scale, norm_g, w_in, None, rows_per_mod=rows_per_mod, seq_len=seq_len, tm=1024, tn=1024)
    q_b, ckv_n, kv_b = _mla_prep(proj, q_norm_g, kv_norm_g, w_uq, w_ukv, tm=512)
    proj3 = proj.reshape(bsz, seq_len, L0_COLS)
    q_b3 = q_b.reshape(bsz, seq_len, -1)
    kv_b3 = kv_b.reshape(bsz, seq_len, -1)
    if ctx is None:
        ya = _diff_attn(proj3, None, None, None, None, diff_lambda, subln_g, tq=256)
        yb = _mla_attn(proj3, q_b3, kv_b3, None, None, None, None, tq=256)
    else:
        k_ctx, v_ctx, ckv_ctx, kpe_ctx = ctx
        p_len = k_ctx.shape[1]
        cos, sin = tables
        kv_ctx = _matmul(ckv_ctx.reshape(bsz * p_len, B_KV_LORA), w_ukv, tm=512).reshape(bsz, p_len, -1)
        ya = _diff_attn(proj3, k_ctx.reshape(bsz, p_len, -1), v_ctx.reshape(bsz, p_len, -1), cos, sin,
                        diff_lambda, subln_g, tq=256)
        yb = _mla_attn(proj3, q_b3, kv_b3, kv_ctx, jnp.concatenate([kpe_ctx, kpe_ctx], axis=-1), cos, sin, tq=256)
    x_new = _outproj(x, ya.reshape(m, -1), yb.reshape(m, -1), w_out, gate, rows_per_mod=rows_per_mod, tm=512)
    return x_new, proj, ckv_n


def _layer1(x, mods, rows_per_mod, seq_len, weights, states, final_g):
    w_in, w_out, mu, lora_w, lora_b, lane_params = weights
    shift, scale, gate, norm_g = mods
    m = x.shape[0]
    bsz = m // seq_len
    proj = _inproj(x, shift, scale, norm_g, w_in, mu, rows_per_mod=rows_per_mod, seq_len=seq_len, tm=1024, tn=512)
    z7 = _relayout(proj, lora_w, lora_b, bsz, seq_len)
    y, finals = None, []
    for z in range(2):
        y, s_fin = _scan(z7, z, lane_params, states[z], y, reverse=(z == 1), tc=32)
        finals.append(_state_from_lanes(s_fin))
    out = _rwkv_out(x, y, proj, w_out, gate, final_g, bsz=bsz, t_len=seq_len)
    return out, finals


def kernel(x_prompt, x_sample, cache_l0_a_k, cache_l0_a_v, cache_l0_mla_ckv, cache_l0_mla_kpe, state_l1_fwd, state_l1_bwd, c, c_ctx, mod_w, mod_b, norm_g, final_norm_g, l0_w_in, l0_w_out, l0_diff_lambda, l0_subln_g, l0_q_norm_g, l0_w_uq, l0_kv_norm_g, l0_w_ukv, l1_w_in, l1_w_out, l1_mu, l1_w0, l1_w2, l1_a0, l1_a2, l1_k_k, l1_k_a, l1_r_k, l1_ln_w, l1_ln_b):
    d = D_MODEL
    bp, tp, _ = x_prompt.shape
    bs, ts, _ = x_sample.shape

    cond = jnp.concatenate([c_ctx[None, :], c, jnp.zeros((MOD_ROWS - 1 - bs, d), F32)], axis=0)
    mods = _adaln(cond, mod_w, mod_b)

    def mod_rows(layer, lo, hi):
        rows = mods[layer, lo:hi]
        shift, scale, gate = (rows[:, i * d:(i + 1) * d].reshape(hi - lo, 1, d) for i in range(3))
        return shift, scale, gate, norm_g[layer].reshape(1, d)

    w_uq = l0_w_uq.reshape(B_Q_LORA, B_HEADS, B_NOPE + B_ROPE)
    w_uq = jnp.concatenate([w_uq[:, :, :B_NOPE].reshape(B_Q_LORA, -1), w_uq[:, :, B_NOPE:].reshape(B_Q_LORA, -1)],
                           axis=1).astype(BF16)
    l0_weights = (_l0_w_in_layout(l0_w_in), l0_w_out.astype(BF16), l0_diff_lambda, l0_subln_g.reshape(1, -1),
                  l0_q_norm_g.reshape(1, -1), w_uq, l0_kv_norm_g.reshape(1, -1), l0_w_ukv.astype(BF16))
    lora_pad = jnp.zeros((2, LORA_PAD - C_LORA, C_WIDTH), F32)
    lora_w = jnp.concatenate([jnp.concatenate([l1_w2, lora_pad], axis=1),
                              jnp.concatenate([l1_a2, lora_pad], axis=1)], axis=0).astype(BF16)
    lora_b = jnp.concatenate([l1_w0, l1_a0], axis=0).reshape(4, 1, C_WIDTH)
    lane_params = [_lane_tile(p) for p in (l1_k_k, l1_k_a, l1_r_k, l1_ln_w, l1_ln_b)]
    l1_weights = (_l1_cols_layout(l1_w_in.astype(BF16)), l1_w_out.astype(BF16), _l1_cols_layout(l1_mu),
                  lora_w, lora_b, lane_params)
    tables = _rope_tables(ts)
    final_g = final_norm_g.reshape(1, d)

    xp = x_prompt.reshape(bp * tp, d)
    xs = x_sample.reshape(bs * ts, d)
    ctx0 = (cache_l0_a_k, cache_l0_a_v, cache_l0_mla_ckv, cache_l0_mla_kpe)

    xp1, proj_p, ckv_p = _layer0(xp, mod_rows(0, 0, 1), bp * tp, tp, l0_weights, None, None)
    xs1, _, _ = _layer0(xs, mod_rows(0, 1, 1 + bs), ts, ts, l0_weights, ctx0, tables)

    zero_state = jnp.zeros((bp // PER, C_HEAD, C_HEAD, LANES), F32)
    y_prompt, finals = _layer1(xp1, mod_rows(1, 0, 1), bp * tp, tp, l1_weights, (zero_state, zero_state), final_g)
    y_sample, _ = _layer1(xs1, mod_rows(1, 1, 1 + bs), ts, ts, l1_weights,
                          (_state_to_lanes(state_l1_fwd), _state_to_lanes(state_l1_bwd)), final_g)

    new_a_k = proj_p[:, 3072:4096].reshape(bp, tp, A_HEADS, 2, A_QK_DIM)
    new_a_v = proj_p[:, 4096:5120].reshape(bp, tp, A_HEADS, A_V_DIM)
    new_ckv = ckv_p.reshape(bp, tp, B_KV_LORA)
    new_kpe = proj_p[:, 5888:5888 + B_ROPE].reshape(bp, tp, B_ROPE)
    return (y_prompt.reshape(bp, tp, d), y_sample.reshape(bs, ts, d), new_a_k, new_a_v, new_ckv, new_kpe,
            finals[0], finals[1])
```

```python
import functools
import math

import jax
import jax.numpy as jnp
from jax import lax
from jax.experimental import pallas as pl
from jax.experimental.pallas import tpu as pltpu

F32 = jnp.float32
BF16 = jnp.bfloat16

D_MODEL = 2048
GRID_W = 64
ROPE_BASE = 10000.0
NORM_EPS = 1e-6
GN_EPS = 64e-5

A_HEADS = 8
A_QK_DIM = 64
A_V_DIM = 128
B_HEADS = 8
B_NOPE = 128
B_ROPE = 64
B_V = 128
B_Q_LORA = 512
B_KV_LORA = 256
AB_WIDTH = A_HEADS * A_V_DIM + B_HEADS * B_V
C_HEAD = 64
C_HEADS = D_MODEL // C_HEAD
C_WIDTH = C_HEADS * C_HEAD
C_LORA = 96
LAM_INIT_L0 = 0.8 - 0.6 * math.exp(-0.3 * 0)
LOG2_E = math.log2(math.e)

LANES = 128
SUBLANES = 8
LORA_PAD = 128
MOD_ROWS = 8
VMEM_LIMIT = 56 * 1024 * 1024

L0_COLS = 6144
L0_GATE_A, L0_GATE_B, L0_AQ, L0_AK, L0_AV = 0, 1, 2, 3, 4
L0_CQ = 10
L0_CKV = 22
L0_KPE = 46
L1_COLS = 4 * C_WIDTH + 4 * LORA_PAD
PER = LANES // C_HEADS
PITCH = 72
TT = LANES
N_LANE_ARRAYS = 7


def _cparams(*sem):
    return pltpu.CompilerParams(dimension_semantics=sem, vmem_limit_bytes=VMEM_LIMIT)


def _silu(x):
    return x / (1.0 + jnp.exp(-x))


def _dot(a, b):
    return jnp.dot(a, b, preferred_element_type=F32)


def _dot_nt(a, b):
    return lax.dot_general(a, b, (((1,), (1,)), ((), ())), preferred_element_type=F32)


def _rope(x, cos, sin):
    lane = lax.broadcasted_iota(jnp.int32, x.shape, 1)
    low = (lane % 32) < 16
    rot = jnp.where(low, pltpu.roll(x, LANES - 16, 1), pltpu.roll(x, 16, 1))
    return x * cos + rot * sin


def _adaln_body(c_ref, w_ref, b_ref, o_ref):
    s = _silu(c_ref[...]).astype(BF16)
    o_ref[0] = _dot(s, w_ref[0].astype(BF16)) + b_ref[0]


def _adaln(cond, mod_w, mod_b):
    depth, d, n = mod_w.shape
    tn = 1024
    return pl.pallas_call(
        _adaln_body,
        grid=(depth, n // tn),
        in_specs=[
            pl.BlockSpec((MOD_ROWS, d), lambda l, j: (0, 0)),
            pl.BlockSpec((1, d, tn), lambda l, j: (l, 0, j)),
            pl.BlockSpec((1, 1, tn), lambda l, j: (l, 0, j)),
        ],
        out_specs=pl.BlockSpec((1, MOD_ROWS, tn), lambda l, j: (l, 0, j)),
        out_shape=jax.ShapeDtypeStruct((depth, MOD_ROWS, n), F32),
        compiler_params=_cparams("parallel", "parallel"),
        name="adaln",
    )(cond, mod_w, mod_b.reshape(depth, 1, n))


def _inproj_body(x_ref, sh_ref, sc_ref, g_ref, w_ref, *rest, seq_len):
    if seq_len:
        mu_ref, o_ref, h_ref, p_scr = rest
    else:
        o_ref, h_ref = rest

    @pl.when(pl.program_id(1) == 0)
    def _():
        x = x_ref[...]
        y = x * lax.rsqrt(jnp.mean(x * x, axis=-1, keepdims=True) + NORM_EPS) * g_ref[...]
        h_ref[...] = (y * (1.0 + sc_ref[0]) + sh_ref[0]).astype(BF16)

    if not seq_len:
        o_ref[...] = _dot(h_ref[...], w_ref[...])
        return
    tm, tn = o_ref.shape
    zeros = jnp.zeros((SUBLANES, tn), F32)
    mu0, mu1 = mu_ref[0:1, :], mu_ref[1:2, :]
    keep = 1.0 - mu0 - mu1
    p_scr[pl.ds(0, SUBLANES), :] = zeros
    p_scr[pl.ds(SUBLANES, tm), :] = _dot(h_ref[...], w_ref[...])
    p_scr[pl.ds(SUBLANES + tm, SUBLANES), :] = zeros
    at = lambda r0, n, off: p_scr[pl.ds(SUBLANES + r0 + off, n), :]
    o_ref[...] = at(0, tm, 0) * keep + at(0, tm, -1) * mu0 + at(0, tm, 1) * mu1
    edge = lax.broadcasted_iota(jnp.int32, (2 * SUBLANES, 1), 0)
    for s in range(1, tm // seq_len):
        r0 = s * seq_len - SUBLANES
        prev = jnp.where(edge == SUBLANES, 0.0, at(r0, 2 * SUBLANES, -1))
        nxt = jnp.where(edge == SUBLANES - 1, 0.0, at(r0, 2 * SUBLANES, 1))
        o_ref[pl.ds(r0, 2 * SUBLANES), :] = at(r0, 2 * SUBLANES, 0) * keep + prev * mu0 + nxt * mu1


def _inproj(x, shift, scale, g, w, mu, *, rows_per_mod, seq_len, tm, tn):
    m, d = x.shape
    n = w.shape[1]
    mod_map = lambda i, j: ((i * tm) // rows_per_mod, 0, 0)
    in_specs = [
        pl.BlockSpec((tm, d), lambda i, j: (i, 0)),
        pl.BlockSpec((1, 1, d), mod_map),
        pl.BlockSpec((1, 1, d), mod_map),
        pl.BlockSpec((1, d), lambda i, j: (0, 0)),
        pl.BlockSpec((d, tn), lambda i, j: (0, j)),
    ]
    args = [x, shift, scale, g, w]
    if mu is not None:
        in_specs.append(pl.BlockSpec((2, tn), lambda i, j: (0, j)))
        args.append(mu)
    scratch = [pltpu.VMEM((tm, d), BF16)]
    if mu is not None:
        scratch.append(pltpu.VMEM((tm + 2 * SUBLANES, tn), F32))
    return pl.pallas_call(
        functools.partial(_inproj_body, seq_len=seq_len if mu is not None else 0),
        grid=(m // tm, n // tn),
        in_specs=in_specs,
        out_specs=pl.BlockSpec((tm, tn), lambda i, j: (i, j)),
        out_shape=jax.ShapeDtypeStruct((m, n), F32),
        scratch_shapes=scratch,
        compiler_params=_cparams("parallel", "arbitrary"),
        name="inproj",
    )(*args)


def _rms(x, g):
    return x * lax.rsqrt(jnp.mean(x * x, axis=-1, keepdims=True) + NORM_EPS) * g


def _mla_prep_body(cq_ref, ckv_ref, qg_ref, kg_ref, wuq_ref, qb_ref, ckvn_ref):
    qb_ref[...] = _dot(_rms(cq_ref[...], qg_ref[...]).astype(BF16), wuq_ref[...])
    ckvn_ref[...] = _rms(ckv_ref[...], kg_ref[...])


def _mla_prep(proj, q_norm_g, kv_norm_g, w_uq, *, tm):
    m = proj.shape[0]
    nq = w_uq.shape[1]
    const = lambda i: (0, 0)
    return pl.pallas_call(
        _mla_prep_body,
        grid=(m // tm,),
        in_specs=[
            pl.BlockSpec((tm, B_Q_LORA), lambda i: (i, L0_CQ)),
            pl.BlockSpec((tm, B_KV_LORA), lambda i: (i, L0_CKV)),
            pl.BlockSpec((1, B_Q_LORA), const),
            pl.BlockSpec((1, B_KV_LORA), const),
            pl.BlockSpec((B_Q_LORA, nq), const),
        ],
        out_specs=[
            pl.BlockSpec((tm, nq), lambda i: (i, 0)),
            pl.BlockSpec((tm, B_KV_LORA), lambda i: (i, 0)),
        ],
        out_shape=[
            jax.ShapeDtypeStruct((m, nq), F32),
            jax.ShapeDtypeStruct((m, B_KV_LORA), F32),
        ],
        compiler_params=_cparams("parallel"),
        name="mla_prep",
    )(proj, proj, q_norm_g, kv_norm_g, w_uq)


def _softmax_parts(scores):
    m = functools.reduce(jnp.maximum, [jnp.max(s, axis=-1, keepdims=True) for s in scores])
    ps = [jnp.exp2(s - m) for s in scores]
    denom = functools.reduce(jnp.add, [jnp.sum(p, axis=-1, keepdims=True) for p in ps])
    return ps, denom


def _diff_attn_body(*refs, dec, tq):
    if dec:
        (q_ref, k_ref, v_ref, kc_ref, vc_ref, gate_ref, cos_ref, sin_ref, lam_ref, sg_ref,
         o_ref, kb, vb, kcb, vcb) = refs
    else:
        q_ref, k_ref, v_ref, gate_ref, lam_ref, sg_ref, o_ref, kb, vb = refs
    qi = pl.program_id(1)

    @pl.when(qi == 0)
    def _():
        for h in range(A_HEADS):
            hs = slice(h * LANES, (h + 1) * LANES)
            kt = k_ref[0, :, hs]
            if dec:
                kt = _rope(kt, cos_ref[...], sin_ref[...])
                kcb[:, hs] = kc_ref[0, :, hs].astype(BF16)
                vcb[:, hs] = vc_ref[0, :, hs].astype(BF16)
            kb[:, hs] = kt.astype(BF16)
            vb[:, hs] = v_ref[0, :, hs].astype(BF16)

    lp = lam_ref[...]
    lam = (jnp.exp(jnp.sum(lp[0:1] * lp[1:2], keepdims=True))
           - jnp.exp(jnp.sum(lp[2:3] * lp[3:4], keepdims=True)) + LAM_INIT_L0)
    if dec:
        row0 = pl.multiple_of(qi * tq, tq)
        cq, sq = cos_ref[pl.ds(row0, tq), :], sin_ref[pl.ds(row0, tq), :]
    first = lax.broadcasted_iota(jnp.int32, (1, LANES), 1) < A_QK_DIM
    for h in range(A_HEADS):
        hs = slice(h * LANES, (h + 1) * LANES)
        qh = q_ref[0, :, hs]
        if dec:
            qh = _rope(qh, cq, sq)
        qh = qh * (A_QK_DIM ** -0.5 * LOG2_E)
        q1 = jnp.where(first, qh, 0.0).astype(BF16)
        q2 = jnp.where(first, 0.0, qh).astype(BF16)
        keys = [kb[:, hs]] + ([kcb[:, hs]] if dec else [])
        vals = [vb[:, hs]] + ([vcb[:, hs]] if dec else [])
        p1, l1 = _softmax_parts([_dot_nt(q1, kp) for kp in keys])
        p2, l2 = _softmax_parts([_dot_nt(q2, kp) for kp in keys])
        o1 = functools.reduce(jnp.add, [_dot(x.astype(BF16), vp) for x, vp in zip(p1, vals)])
        o2 = functools.reduce(jnp.add, [_dot(x.astype(BF16), vp) for x, vp in zip(p2, vals)])
        o = o1 * (1.0 / l1) - o2 * (lam / l2)
        o = _rms(o, sg_ref[...]) * (1.0 - LAM_INIT_L0)
        o_ref[0, :, hs] = (o * _silu(gate_ref[0, :, hs])).astype(BF16)


def _diff_attn(proj, ctx_k, ctx_v, cos, sin, diff_lambda, subln_g, *, tq):
    bsz, t_len, _ = proj.shape
    dec = ctx_k is not None
    w = A_HEADS * LANES
    full = lambda col: pl.BlockSpec((1, t_len, w), lambda b, i: (b, 0, col))
    in_specs = [pl.BlockSpec((1, tq, w), lambda b, i: (b, i, L0_AQ)), full(L0_AK), full(L0_AV)]
    args = [proj, proj, proj]
    scratch = [pltpu.VMEM((t_len, w), BF16), pltpu.VMEM((t_len, w), BF16)]
    if dec:
        p_len = ctx_k.shape[1]
        in_specs += [pl.BlockSpec((1, p_len, w), lambda b, i: (b, 0, 0))] * 2
        args += [ctx_k, ctx_v]
        scratch += [pltpu.VMEM((p_len, w), BF16), pltpu.VMEM((p_len, w), BF16)]
    in_specs.append(pl.BlockSpec((1, tq, w), lambda b, i: (b, i, L0_GATE_A)))
    args.append(proj)
    if dec:
        in_specs += [pl.BlockSpec((t_len, LANES), lambda b, i: (0, 0))] * 2
        args += [cos, sin]
    in_specs += [pl.BlockSpec((4, A_QK_DIM), lambda b, i: (0, 0)),
                 pl.BlockSpec((1, A_V_DIM), lambda b, i: (0, 0))]
    args += [diff_lambda, subln_g]
    return pl.pallas_call(
        functools.partial(_diff_attn_body, dec=dec, tq=tq),
        grid=(bsz, t_len // tq),
        in_specs=in_specs,
        out_specs=pl.BlockSpec((1, tq, w), lambda b, i: (b, i, 0)),
        out_shape=jax.ShapeDtypeStruct((bsz, t_len, w), BF16),
        scratch_shapes=scratch,
        compiler_params=_cparams("parallel", "arbitrary"),
        name="diff_attn",
    )(*args)


def _mla_attn_body(*refs, dec, tq):
    if dec:
        (qn_ref, qp_ref, kv_ref, kpe_ref, kvc_ref, kpec_ref, gate_ref, cos_ref, sin_ref, wukv_ref,
         o_ref, kb, vb, kcb, vcb) = refs
    else:
        qn_ref, qp_ref, kv_ref, kpe_ref, gate_ref, wukv_ref, o_ref, kb, vb = refs
    qi = pl.program_id(1)

    def stage(src_ref, kpe, k_dst, v_dst):
        kv = _dot(src_ref[0].astype(BF16), wukv_ref[...]).astype(BF16)
        for h in range(B_HEADS):
            k_dst[:, 2 * h * LANES:(2 * h + 1) * LANES] = kv[:, 2 * h * LANES:(2 * h + 1) * LANES]
            k_dst[:, (2 * h + 1) * LANES:(2 * h + 2) * LANES] = kpe
            v_dst[:, h * LANES:(h + 1) * LANES] = kv[:, (2 * h + 1) * LANES:(2 * h + 2) * LANES]

    @pl.when(qi == 0)
    def _():
        kp = kpe_ref[0]
        if dec:
            kp = _rope(kp, cos_ref[...], sin_ref[...])
            stage(kvc_ref, kpec_ref[0].astype(BF16), kcb, vcb)
        stage(kv_ref, kp.astype(BF16), kb, vb)

    if dec:
        row0 = pl.multiple_of(qi * tq, tq)
        cq, sq = cos_ref[pl.ds(row0, tq), :], sin_ref[pl.ds(row0, tq), :]
    scale = (B_NOPE + B_ROPE) ** -0.5 * LOG2_E
    first = lax.broadcasted_iota(jnp.int32, (1, LANES), 1) < B_ROPE
    for h in range(B_HEADS):
        hs = slice(h * LANES, (h + 1) * LANES)
        if h % 2 == 0:
            pair = qp_ref[0, :, (h // 2) * LANES:(h // 2 + 1) * LANES]
            if dec:
                pair = _rope(pair, cq, sq)
            pair = pair * scale
        qp = (jnp.where(first, pair, 0.0) if h % 2 == 0 else jnp.where(first, 0.0, pair)).astype(BF16)
        qn = (qn_ref[0, :, hs] * scale).astype(BF16)
        q = jnp.concatenate([qn, qp], axis=1)
        ks = slice(2 * h * LANES, (2 * h + 2) * LANES)
        scores = [_dot_nt(q, kb[:, ks])]
        vals = [vb[:, hs]]
        if dec:
            scores.append(_dot_nt(q, kcb[:, ks]))
            vals.append(vcb[:, hs])
        ps, denom = _softmax_parts(scores)
        o = functools.reduce(jnp.add, [_dot(p.astype(BF16), vp) for p, vp in zip(ps, vals)]) * (1.0 / denom)
        o_ref[0, :, hs] = (o * _silu(gate_ref[0, :, hs])).astype(BF16)


def _mla_attn(proj, q_b, kv_b, ctx_kv, ctx_kpe, cos, sin, w_ukv, *, tq):
    bsz, t_len, _ = proj.shape
    dec = ctx_kv is not None
    w = B_HEADS * LANES
    nkv = kv_b.shape[2]
    in_specs = [
        pl.BlockSpec((1, tq, w), lambda b, i: (b, i, 0)),
        pl.BlockSpec((1, tq, B_HEADS * B_ROPE), lambda b, i: (b, i, 2)),
        pl.BlockSpec((1, t_len, nkv), lambda b, i: (b, 0, 0)),
        pl.BlockSpec((1, t_len, LANES), lambda b, i: (b, 0, L0_KPE)),
    ]
    args = [q_b, q_b, kv_b, proj]
    scratch = [pltpu.VMEM((t_len, 2 * w), BF16), pltpu.VMEM((t_len, w), BF16)]
    if dec:
        p_len = ctx_kv.shape[1]
        in_specs += [pl.BlockSpec((1, p_len, nkv), lambda b, i: (b, 0, 0)),
                     pl.BlockSpec((1, p_len, LANES), lambda b, i: (b, 0, 0))]
        args += [ctx_kv, ctx_kpe]
        scratch += [pltpu.VMEM((p_len, 2 * w), BF16), pltpu.VMEM((p_len, w), BF16)]
    in_specs.append(pl.BlockSpec((1, tq, w), lambda b, i: (b, i, L0_GATE_B)))
    args.append(proj)
    if dec:
        in_specs += [pl.BlockSpec((t_len, LANES), lambda b, i: (0, 0))] * 2
        args += [cos, sin]
    in_specs.append(pl.BlockSpec(w_ukv.shape, lambda b, i: (0, 0)))
    args.append(w_ukv)
    return pl.pallas_call(
        functools.partial(_mla_attn_body, dec=dec, tq=tq),
        grid=(bsz, t_len // tq),
        in_specs=in_specs,
        out_specs=pl.BlockSpec((1, tq, w), lambda b, i: (b, i, 0)),
        out_shape=jax.ShapeDtypeStruct((bsz, t_len, w), BF16),
        scratch_shapes=scratch,
        compiler_params=_cparams("parallel", "arbitrary"),
        name="mla_attn",
    )(*args)


def _outproj_body(x_ref, ya_ref, yb_ref, w_ref, gate_ref, o_ref):
    half = ya_ref.shape[1]
    acc = _dot(ya_ref[...], w_ref[0:half, :]) + _dot(yb_ref[...], w_ref[half:, :])
    o_ref[...] = x_ref[...] + gate_ref[0] * acc


def _outproj(x, ya, yb, w, gate, *, rows_per_mod, tm):
    m, d = x.shape
    k = w.shape[0]
    return pl.pallas_call(
        _outproj_body,
        grid=(m // tm,),
        in_specs=[
            pl.BlockSpec((tm, d), lambda i: (i, 0)),
            pl.BlockSpec((tm, k // 2), lambda i: (i, 0)),
            pl.BlockSpec((tm, k // 2), lambda i: (i, 0)),
            pl.BlockSpec((k, d), lambda i: (0, 0), pipeline_mode=pl.Buffered(1)),
            pl.BlockSpec((1, 1, d), lambda i: ((i * tm) // rows_per_mod, 0, 0)),
        ],
        out_specs=pl.BlockSpec((tm, d), lambda i: (i, 0)),
        out_shape=jax.ShapeDtypeStruct((m, d), F32),
        compiler_params=_cparams("parallel"),
        name="outproj",
    )(x, ya, yb, w, gate)


def _relayout_body(main_ref, lora_ref, w_ref, bias_ref, o_ref, r_scr, q_scr):
    q = pl.program_id(2)

    def park(tile_of, prepare=None):
        for b in range(PER):
            if prepare is not None:
                prepare(b)
            for c in range(C_WIDTH // LANES):
                at = tile_of(b, c).T
                p0 = b * C_HEADS + 2 * c
                r_scr[pl.ds(p0 * PITCH, C_HEAD), :] = at[:C_HEAD]
                r_scr[pl.ds((p0 + 1) * PITCH, C_HEAD), :] = at[C_HEAD:]

    @pl.when(q < 3)
    def _():
        park(lambda b, c: main_ref[0, b, :, c * LANES:(c + 1) * LANES])

    expanded = lambda b, c: q_scr[:, c * LANES:(c + 1) * LANES]

    @pl.when((q == 3) | (q == 4))
    def _():
        def decay(b):
            u = -(bias_ref[0] + _dot(jnp.tanh(lora_ref[0, b]).astype(BF16), w_ref[0]))
            softplus = jnp.maximum(u, 0.0) + jnp.log(1.0 + jnp.exp(-jnp.abs(u)))
            q_scr[...] = jnp.exp(-jnp.exp(-softplus - 0.5))
        park(expanded, decay)

    @pl.when(q >= 5)
    def _():
        def iclr(b):
            q_scr[...] = 1.0 / (1.0 + jnp.exp(-(bias_ref[0] + _dot(lora_ref[0, b].astype(BF16), w_ref[0]))))
        park(expanded, iclr)

    for j in range(C_HEAD):
        m = r_scr[pl.ds(j, LANES, stride=PITCH), :]
        o_ref[0, 0, pl.ds(j, TT, stride=PITCH), :] = m.T
    zero = jnp.zeros((TT, LANES), F32)
    for j in range(C_HEAD, PITCH):
        o_ref[0, 0, pl.ds(j, TT, stride=PITCH), :] = zero


def _relayout(proj, lora_w, lora_b, bsz, t_len):
    g = bsz // PER
    proj4 = proj.reshape(g, PER, t_len, L1_COLS)
    lora0 = 4 * C_WIDTH // LANES
    lora_ix = lambda gi, ti, q: (jnp.clip(q - 3, 0, 3), 0, 0)
    out = pl.pallas_call(
        _relayout_body,
        grid=(g, t_len // TT, N_LANE_ARRAYS),
        in_specs=[
            pl.BlockSpec((1, PER, TT, C_WIDTH), lambda gi, ti, q: (gi, 0, ti, jnp.minimum(q, 2))),
            pl.BlockSpec((1, PER, TT, LANES), lambda gi, ti, q: (gi, 0, ti, lora0 + jnp.clip(q - 3, 0, 3))),
            pl.BlockSpec((1, LORA_PAD, C_WIDTH), lora_ix),
            pl.BlockSpec((1, 1, C_WIDTH), lora_ix),
        ],
        out_specs=pl.BlockSpec((1, 1, TT * PITCH, LANES), lambda gi, ti, q: (q, gi, ti, 0)),
        out_shape=jax.ShapeDtypeStruct((N_LANE_ARRAYS, g, t_len * PITCH, LANES), F32),
        scratch_shapes=[pltpu.VMEM((LANES * PITCH, LANES), F32), pltpu.VMEM((TT, C_WIDTH), F32)],
        compiler_params=_cparams("parallel", "parallel", "arbitrary"),
        name="lane_relayout",
    )(proj4, proj4, lora_w, lora_b)
    return out.reshape(N_LANE_ARRAYS, g, t_len, PITCH, LANES)


def _lane_tile(p):
    return jnp.tile(p.reshape(C_HEADS, C_HEAD).T, (1, PER))


def _scan_body(*refs, tc, reverse, has_prev):
    r_ref, k_ref, v_ref, w_ref, a_ref, kkp_ref, kap_ref, rkp_ref, lnw_ref, lnb_ref, s0_ref = refs[:11]
    rest = refs[11:]
    if has_prev:
        prev_ref, rest = rest[0], rest[1:]
    o_ref, sfin_ref, s_scr, kka_scr, kz_scr, nkk_scr, y_scr = rest
    c = pl.program_id(1)

    @pl.when(c == 0)
    def _():
        s_scr[...] = s0_ref[0]

    used = lambda ref: ref[0, 0, :, :C_HEAD, :]
    k = used(k_ref)
    a = used(a_ref)
    kk = k * kkp_ref[...]
    kk = kk * lax.rsqrt(jnp.maximum(jnp.sum(kk * kk, axis=1, keepdims=True), 1e-12))
    kz = k * (1.0 + (a - 1.0) * kap_ref[...])
    kz_scr[...] = kz
    kka_scr[...] = kk * a
    nkk_scr[...] = -kk
    bonus = jnp.sum(used(r_ref) * kz * rkp_ref[...], axis=1, keepdims=True) * used(v_ref)

    first = tc - 1 if reverse else 0
    sa0 = jnp.zeros((C_HEAD, LANES), F32)
    for j in range(C_HEAD):
        sa0 = sa0 + s_scr[j] * nkk_scr[first, j:j + 1, :]

    def step(i, sa):
        t = (tc - 1 - i) if reverse else i
        t_next = jnp.maximum(t - 1, 0) if reverse else jnp.minimum(t + 1, tc - 1)
        v = v_ref[0, 0, t, :C_HEAD, :]
        y = jnp.zeros((C_HEAD, LANES), F32)
        sa_next = jnp.zeros((C_HEAD, LANES), F32)
        for j in range(C_HEAD):
            s = s_scr[j] * w_ref[0, 0, t, j:j + 1, :] + sa * kka_scr[t, j:j + 1, :] + v * kz_scr[t, j:j + 1, :]
            s_scr[j] = s
            y = y + s * r_ref[0, 0, t, j:j + 1, :]
            sa_next = sa_next + s * nkk_scr[t_next, j:j + 1, :]
        y_scr[t] = y
        return sa_next

    lax.fori_loop(0, tc, step, sa0)

    y = y_scr[...]
    dev = y - jnp.mean(y, axis=1, keepdims=True)
    var = jnp.mean(dev * dev, axis=1, keepdims=True)
    out = dev * lax.rsqrt(var + GN_EPS) * lnw_ref[...] + lnb_ref[...] + bonus
    if has_prev:
        out = out + prev_ref[0, :, :C_HEAD, :]
    o_ref[0, :, :C_HEAD, :] = out
    o_ref[0, :, C_HEAD:, :] = jnp.zeros((tc, PITCH - C_HEAD, LANES), F32)

    @pl.when(c == pl.num_programs(1) - 1)
    def _():
        sfin_ref[0] = s_scr[...]


def _scan(z7, z, params, s0, prev, *, reverse, tc):
    _, g, t_len = z7.shape[:3]
    n_c = t_len // tc
    tix = (lambda ci: n_c - 1 - ci) if reverse else (lambda ci: ci)
    qblk = lambda q: pl.BlockSpec((1, 1, tc, PITCH, LANES), lambda gi, ci: (q, gi, tix(ci), 0, 0))
    tile = pl.BlockSpec((C_HEAD, LANES), lambda gi, ci: (0, 0))
    st = pl.BlockSpec((1, C_HEAD, C_HEAD, LANES), lambda gi, ci: (gi, 0, 0, 0))
    oblk = pl.BlockSpec((1, tc, PITCH, LANES), lambda gi, ci: (gi, tix(ci), 0, 0))
    in_specs = [qblk(0), qblk(1), qblk(2), qblk(3 + z), qblk(5 + z)] + [tile] * len(params) + [st]
    args = [z7] * 5 + list(params) + [s0]
    if prev is not None:
        in_specs.append(oblk)
        args.append(prev)
    chunk = pltpu.VMEM((tc, C_HEAD, LANES), F32)
    return pl.pallas_call(
        functools.partial(_scan_body, tc=tc, reverse=reverse, has_prev=prev is not None),
        grid=(g, n_c),
        in_specs=in_specs,
        out_specs=[oblk, st],
        out_shape=[jax.ShapeDtypeStruct((g, t_len, PITCH, LANES), F32), jax.ShapeDtypeStruct(s0.shape, F32)],
        scratch_shapes=[pltpu.VMEM((C_HEAD, C_HEAD, LANES), F32)] + [chunk] * 4,
        compiler_params=_cparams("parallel", "arbitrary"),
        name="wkv_scan",
    )(*args)


def _state_to_lanes(s):
    per = LANES // C_HEADS
    bsz = s.shape[0]
    s = s.reshape(bsz // per, per, C_HEADS, C_HEAD, C_HEAD)
    return s.transpose(0, 4, 3, 1, 2).reshape(bsz // per, C_HEAD, C_HEAD, LANES)


def _state_from_lanes(s):
    per = LANES // C_HEADS
    g = s.shape[0]
    s = s.reshape(g, C_HEAD, C_HEAD, per, C_HEADS)
    return s.transpose(0, 3, 4, 2, 1).reshape(g * per, C_HEADS, C_HEAD, C_HEAD)


def _rwkv_out_body(y_ref, x_ref, gate_ref, w_ref, mg_ref, fg_ref, o_ref, r_scr, y_scr):
    b = pl.program_id(2)

    @pl.when(b == 0)
    def _():
        for i in range(C_HEAD):
            m = y_ref[0, pl.ds(i, TT, stride=PITCH), :]
            r_scr[pl.ds(i, LANES, stride=PITCH), :] = m.T
        for bb in range(PER):
            for c in range(C_WIDTH // LANES):
                p0 = bb * C_HEADS + 2 * c
                pair = jnp.concatenate([r_scr[pl.ds(p0 * PITCH, C_HEAD), :],
                                        r_scr[pl.ds((p0 + 1) * PITCH, C_HEAD), :]], axis=0)
                y_scr[bb, :, c * LANES:(c + 1) * LANES] = pair.T

    yg = (y_scr[b] * _silu(gate_ref[0, 0])).astype(BF16)
    xn = x_ref[0, 0] + mg_ref[0] * _dot(yg, w_ref[...])
    o_ref[0, 0] = _rms(xn, fg_ref[...])


def _rwkv_out(x, y, proj, w, gate, final_g, *, bsz, t_len):
    d = x.shape[1]
    g = bsz // PER
    n_mod = gate.shape[0]
    tok = lambda gi, ti, b: (gi, b, ti, 0)
    const = lambda gi, ti, b: (0, 0)
    out = pl.pallas_call(
        _rwkv_out_body,
        grid=(g, t_len // TT, PER),
        in_specs=[
            pl.BlockSpec((1, TT * PITCH, LANES), lambda gi, ti, b: (gi, ti, 0)),
            pl.BlockSpec((1, 1, TT, d), tok),
            pl.BlockSpec((1, 1, TT, C_WIDTH), lambda gi, ti, b: (gi, b, ti, 3)),
            pl.BlockSpec((C_WIDTH, d), const, pipeline_mode=pl.Buffered(1)),
            pl.BlockSpec((1, 1, d), lambda gi, ti, b: ((gi * PER + b) % n_mod, 0, 0)),
            pl.BlockSpec((1, d), const),
        ],
        out_specs=pl.BlockSpec((1, 1, TT, d), tok),
        out_shape=jax.ShapeDtypeStruct((g, PER, t_len, d), F32),
        scratch_shapes=[pltpu.VMEM((LANES * PITCH, LANES), F32), pltpu.VMEM((PER, TT, C_WIDTH), F32)],
        compiler_params=_cparams("parallel", "parallel", "arbitrary"),
        name="rwkv_out",
    )(y.reshape(g, t_len * PITCH, LANES), x.reshape(g, PER, t_len, d), proj.reshape(g, PER, t_len, L1_COLS),
      w, gate, final_g)
    return out.reshape(bsz * t_len, d)


def _rope_tables(t_len):
    pos = jnp.arange(t_len)
    half = 16
    freqs = ROPE_BASE ** (-jnp.arange(half, dtype=F32) / half)
    ang_r = (pos // GRID_W).astype(F32)[:, None] * freqs
    ang_c = (pos % GRID_W).astype(F32)[:, None] * freqs
    cos = jnp.concatenate([jnp.cos(ang_r)] * 2 + [jnp.cos(ang_c)] * 2, axis=-1)
    sin = jnp.concatenate([-jnp.sin(ang_r), jnp.sin(ang_r), -jnp.sin(ang_c), jnp.sin(ang_c)], axis=-1)
    return jnp.tile(cos, (1, 2)), jnp.tile(sin, (1, 2))


def _l0_w_in_layout(w):
    d = w.shape[0]
    aq, ak, av, cq, ckv, kpe, gate = jnp.split(w.astype(BF16), [1024, 2048, 3072, 3584, 3840, 3904], axis=1)
    pad = jnp.zeros((d, L0_COLS - 6016), BF16)
    return jnp.concatenate([gate, aq, ak, av, cq, ckv, kpe, kpe, pad], axis=1)


def _l1_cols_layout(w):
    main, wd, ad = jnp.split(w, [4 * C_WIDTH, 4 * C_WIDTH + 2 * C_LORA], axis=1)
    pad = jnp.zeros((w.shape[0], LORA_PAD - C_LORA), w.dtype)
    parts = [main]
    for seg in (wd, ad):
        for z in range(2):
            parts += [seg[:, z * C_LORA:(z + 1) * C_LORA], pad]
    return jnp.concatenate(parts, axis=1)


def _layer0(x, mods, rows_per_mod, seq_len, weights, ctx, tables):
    w_in, w_out, diff_lambda, subln_g, q_norm_g, w_uq, kv_norm_g, w_ukv = weights
    shift, scale, gate, norm_g = mods
    m = x.shape[0]
    bsz = m // seq_len
    proj = _inproj(x, shift, scale, norm_g, w_in, None, rows_per_mod=rows_per_mod, seq_len=seq_len, tm=1024, tn=1024)
    q_b, ckv_n = _mla_prep(proj, q_norm_g, kv_norm_g, w_uq, tm=512)
    proj3 = proj.reshape(bsz, seq_len, L0_COLS)
    q_b3 = q_b.reshape(bsz, seq_len, -1)
    ckv3 = ckv_n.reshape(bsz, seq_len, -1)
    if ctx is None:
        ya = _diff_attn(proj3, None, None, None, None, diff_lambda, subln_g, tq=256)
        yb = _mla_attn(proj3, q_b3, ckv3, None, None, None, None, w_ukv, tq=256)
    else:
        k_ctx, v_ctx, ckv_ctx, kpe_ctx = ctx
        p_len = k_ctx.shape[1]
        cos, sin = tables
        ya = _diff_attn(proj3, k_ctx.reshape(bsz, p_len, -1), v_ctx.reshape(bsz, p_len, -1), cos, sin,
                        diff_lambda, subln_g, tq=256)
        yb = _mla_attn(proj3, q_b3, ckv3, ckv_ctx, jnp.concatenate([kpe_ctx, kpe_ctx], axis=-1), cos, sin, w_ukv,
                       tq=256)
    x_new = _outproj(x, ya.reshape(m, -1), yb.reshape(m, -1), w_out, gate, rows_per_mod=rows_per_mod, tm=512)
    return x_new, proj, ckv_n


def _layer1(x, mods, rows_per_mod, seq_len, weights, states, final_g):
    w_in, w_out, mu, lora_w, lora_b, lane_params = weights
    shift, scale, gate, norm_g = mods
    m = x.shape[0]
    bsz = m // seq_len
    proj = _inproj(x, shift, scale, norm_g, w_in, mu, rows_per_mod=rows_per_mod, seq_len=seq_len, tm=1024, tn=512)
    z7 = _relayout(proj, lora_w, lora_b, bsz, seq_len)
    y, finals = None, []
    for z in range(2):
        y, s_fin = _scan(z7, z, lane_params, states[z], y, reverse=(z == 1), tc=32)
        finals.append(_state_from_lanes(s_fin))
    out = _rwkv_out(x, y, proj, w_out, gate, final_g, bsz=bsz, t_len=seq_len)
    return out, finals


def kernel(x_prompt, x_sample, cache_l0_a_k, cache_l0_a_v, cache_l0_mla_ckv, cache_l0_mla_kpe, state_l1_fwd, state_l1_bwd, c, c_ctx, mod_w, mod_b, norm_g, final_norm_g, l0_w_in, l0_w_out, l0_diff_lambda, l0_subln_g, l0_q_norm_g, l0_w_uq, l0_kv_norm_g, l0_w_ukv, l1_w_in, l1_w_out, l1_mu, l1_w0, l1_w2, l1_a0, l1_a2, l1_k_k, l1_k_a, l1_r_k, l1_ln_w, l1_ln_b):
    d = D_MODEL
    bp, tp, _ = x_prompt.shape
    bs, ts, _ = x_sample.shape

    cond = jnp.concatenate([c_ctx[None, :], c, jnp.zeros((MOD_ROWS - 1 - bs, d), F32)], axis=0)
    mods = _adaln(cond, mod_w, mod_b)

    def mod_rows(layer, lo, hi):
        rows = mods[layer, lo:hi]
        shift, scale, gate = (rows[:, i * d:(i + 1) * d].reshape(hi - lo, 1, d) for i in range(3))
        return shift, scale, gate, norm_g[layer].reshape(1, d)

    w_uq = l0_w_uq.reshape(B_Q_LORA, B_HEADS, B_NOPE + B_ROPE)
    w_uq = jnp.concatenate([w_uq[:, :, :B_NOPE].reshape(B_Q_LORA, -1), w_uq[:, :, B_NOPE:].reshape(B_Q_LORA, -1)],
                           axis=1).astype(BF16)
    l0_weights = (_l0_w_in_layout(l0_w_in), l0_w_out.astype(BF16), l0_diff_lambda, l0_subln_g.reshape(1, -1),
                  l0_q_norm_g.reshape(1, -1), w_uq, l0_kv_norm_g.reshape(1, -1), l0_w_ukv.astype(BF16))
    lora_pad = jnp.zeros((2, LORA_PAD - C_LORA, C_WIDTH), F32)
    lora_w = jnp.concatenate([jnp.concatenate([l1_w2, lora_pad], axis=1),
                              jnp.concatenate([l1_a2, lora_pad], axis=1)], axis=0).astype(BF16)
    lora_b = jnp.concatenate([l1_w0, l1_a0], axis=0).reshape(4, 1, C_WIDTH)
    lane_params = [_lane_tile(p) for p in (l1_k_k, l1_k_a, l1_r_k, l1_ln_w, l1_ln_b)]
    l1_weights = (_l1_cols_layout(l1_w_in.astype(BF16)), l1_w_out.astype(BF16), _l1_cols_layout(l1_mu),
                  lora_w, lora_b, lane_params)
    tables = _rope_tables(ts)
    final_g = final_norm_g.reshape(1, d)

    xp = x_prompt.reshape(bp * tp, d)
    xs = x_sample.reshape(bs * ts, d)
    ctx0 = (cache_l0_a_k, cache_l0_a_v, cache_l0_mla_ckv, cache_l0_mla_kpe)

    xp1, proj_p, ckv_p = _layer0(xp, mod_rows(0, 0, 1), bp * tp, tp, l0_weights, None, None)
    xs1, _, _ = _layer0(xs, mod_rows(0, 1, 1 + bs), ts, ts, l0_weights, ctx0, tables)

    zero_state = jnp.zeros((bp // PER, C_HEAD, C_HEAD, LANES), F32)
    y_prompt, finals = _layer1(xp1, mod_rows(1, 0, 1), bp * tp, tp, l1_weights, (zero_state, zero_state), final_g)
    y_sample, _ = _layer1(xs1, mod_rows(1, 1, 1 + bs), ts, ts, l1_weights,
                          (_state_to_lanes(state_l1_fwd), _state_to_lanes(state_l1_bwd)), final_g)

    new_a_k = proj_p[:, 3072:4096].reshape(bp, tp, A_HEADS, 2, A_QK_DIM)
    new_a_v = proj_p[:, 4096:5120].reshape(bp, tp, A_HEADS, A_V_DIM)
    new_ckv = ckv_p.reshape(bp, tp, B_KV_LORA)
    new_kpe = proj_p[:, 5888:5888 + B_ROPE].reshape(bp, tp, B_ROPE)
    return (y_prompt.reshape(bp, tp, d), y_sample.reshape(bs, ts, d), new_a_k, new_a_v, new_ckv, new_kpe,
            finals[0], finals[1])
```

```python
import functools
import math

import jax
import jax.numpy as jnp
from jax import lax
from jax.experimental import pallas as pl
from jax.experimental.pallas import tpu as pltpu

F32 = jnp.float32
BF16 = jnp.bfloat16

D_MODEL = 2048
GRID_W = 64
ROPE_BASE = 10000.0
NORM_EPS = 1e-6
GN_EPS = 64e-5

A_HEADS = 8
A_QK_DIM = 64
A_V_DIM = 128
B_HEADS = 8
B_NOPE = 128
B_ROPE = 64
B_V = 128
B_Q_LORA = 512
B_KV_LORA = 256
AB_WIDTH = A_HEADS * A_V_DIM + B_HEADS * B_V
C_HEAD = 64
C_HEADS = D_MODEL // C_HEAD
C_WIDTH = C_HEADS * C_HEAD
C_LORA = 96
LAM_INIT_L0 = 0.8 - 0.6 * math.exp(-0.3 * 0)
LOG2_E = math.log2(math.e)
MLA_Q_SCALE = (B_NOPE + B_ROPE) ** -0.5 * LOG2_E

LANES = 128
SUBLANES = 8
LORA_PAD = 128
MOD_ROWS = 8
VMEM_LIMIT = 56 * 1024 * 1024

L0_COLS = 6144
L0_GATE_A, L0_GATE_B, L0_AQ, L0_AK, L0_AV = 0, 1, 2, 3, 4
L0_CQ = 10
L0_CKV = 22
L0_KPE = 46
L1_COLS = 4 * C_WIDTH + 4 * LORA_PAD
PER = LANES // C_HEADS
PITCH = 72
TT = LANES
N_LANE_ARRAYS = 7


def _cparams(*sem):
    return pltpu.CompilerParams(dimension_semantics=sem, vmem_limit_bytes=VMEM_LIMIT)


def _silu(x):
    return x / (1.0 + jnp.exp(-x))


def _dot(a, b):
    return jnp.dot(a, b, preferred_element_type=F32)


def _dot_nt(a, b):
    return lax.dot_general(a, b, (((1,), (1,)), ((), ())), preferred_element_type=F32)


def _rope(x, cos, sin):
    lane = lax.broadcasted_iota(jnp.int32, x.shape, 1)
    low = (lane % 32) < 16
    rot = jnp.where(low, pltpu.roll(x, LANES - 16, 1), pltpu.roll(x, 16, 1))
    return x * cos + rot * sin


def _adaln_body(c_ref, w_ref, b_ref, o_ref):
    s = _silu(c_ref[...]).astype(BF16)
    o_ref[0] = _dot(s, w_ref[0].astype(BF16)) + b_ref[0]


def _adaln(cond, mod_w, mod_b):
    depth, d, n = mod_w.shape
    tn = 1024
    return pl.pallas_call(
        _adaln_body,
        grid=(depth, n // tn),
        in_specs=[
            pl.BlockSpec((MOD_ROWS, d), lambda l, j: (0, 0)),
            pl.BlockSpec((1, d, tn), lambda l, j: (l, 0, j)),
            pl.BlockSpec((1, 1, tn), lambda l, j: (l, 0, j)),
        ],
        out_specs=pl.BlockSpec((1, MOD_ROWS, tn), lambda l, j: (l, 0, j)),
        out_shape=jax.ShapeDtypeStruct((depth, MOD_ROWS, n), F32),
        compiler_params=_cparams("parallel", "parallel"),
        name="adaln",
    )(cond, mod_w, mod_b.reshape(depth, 1, n))


def _inproj_body(x_ref, sh_ref, sc_ref, g_ref, w_ref, *rest, seq_len):
    if seq_len:
        mu_ref, o_ref, h_ref, p_scr = rest
    else:
        o_ref, h_ref = rest

    @pl.when(pl.program_id(1) == 0)
    def _():
        x = x_ref[...]
        y = x * lax.rsqrt(jnp.mean(x * x, axis=-1, keepdims=True) + NORM_EPS) * g_ref[...]
        h_ref[...] = (y * (1.0 + sc_ref[0]) + sh_ref[0]).astype(BF16)

    if not seq_len:
        o_ref[...] = _dot(h_ref[...], w_ref[...])
        return
    tm, tn = o_ref.shape
    zeros = jnp.zeros((SUBLANES, tn), F32)
    mu0, mu1 = mu_ref[0:1, :], mu_ref[1:2, :]
    keep = 1.0 - mu0 - mu1
    p_scr[pl.ds(0, SUBLANES), :] = zeros
    p_scr[pl.ds(SUBLANES, tm), :] = _dot(h_ref[...], w_ref[...])
    p_scr[pl.ds(SUBLANES + tm, SUBLANES), :] = zeros
    at = lambda r0, n, off: p_scr[pl.ds(SUBLANES + r0 + off, n), :]
    o_ref[...] = at(0, tm, 0) * keep + at(0, tm, -1) * mu0 + at(0, tm, 1) * mu1
    edge = lax.broadcasted_iota(jnp.int32, (2 * SUBLANES, 1), 0)
    for s in range(1, tm // seq_len):
        r0 = s * seq_len - SUBLANES
        prev = jnp.where(edge == SUBLANES, 0.0, at(r0, 2 * SUBLANES, -1))
        nxt = jnp.where(edge == SUBLANES - 1, 0.0, at(r0, 2 * SUBLANES, 1))
        o_ref[pl.ds(r0, 2 * SUBLANES), :] = at(r0, 2 * SUBLANES, 0) * keep + prev * mu0 + nxt * mu1


def _inproj(x, shift, scale, g, w, mu, *, rows_per_mod, seq_len, tm, tn):
    m, d = x.shape
    n = w.shape[1]
    mod_map = lambda i, j: ((i * tm) // rows_per_mod, 0, 0)
    in_specs = [
        pl.BlockSpec((tm, d), lambda i, j: (i, 0)),
        pl.BlockSpec((1, 1, d), mod_map),
        pl.BlockSpec((1, 1, d), mod_map),
        pl.BlockSpec((1, d), lambda i, j: (0, 0)),
        pl.BlockSpec((d, tn), lambda i, j: (0, j)),
    ]
    args = [x, shift, scale, g, w]
    if mu is not None:
        in_specs.append(pl.BlockSpec((2, tn), lambda i, j: (0, j)))
        args.append(mu)
    scratch = [pltpu.VMEM((tm, d), BF16)]
    if mu is not None:
        scratch.append(pltpu.VMEM((tm + 2 * SUBLANES, tn), F32))
    return pl.pallas_call(
        functools.partial(_inproj_body, seq_len=seq_len if mu is not None else 0),
        grid=(m // tm, n // tn),
        in_specs=in_specs,
        out_specs=pl.BlockSpec((tm, tn), lambda i, j: (i, j)),
        out_shape=jax.ShapeDtypeStruct((m, n), F32),
        scratch_shapes=scratch,
        compiler_params=_cparams("parallel", "arbitrary"),
        name="inproj",
    )(*args)


def _rms(x, g):
    return x * lax.rsqrt(jnp.mean(x * x, axis=-1, keepdims=True) + NORM_EPS) * g


def _mla_prep_body(cq_ref, ckv_ref, qg_ref, kg_ref, wuq_ref, qb_ref, ckvn_ref):
    qb_ref[...] = _dot(_rms(cq_ref[...], qg_ref[...]).astype(BF16), wuq_ref[...]) * MLA_Q_SCALE
    ckvn_ref[...] = _rms(ckv_ref[...], kg_ref[...])


def _mla_prep(proj, q_norm_g, kv_norm_g, w_uq, *, tm):
    m = proj.shape[0]
    nq = w_uq.shape[1]
    const = lambda i: (0, 0)
    return pl.pallas_call(
        _mla_prep_body,
        grid=(m // tm,),
        in_specs=[
            pl.BlockSpec((tm, B_Q_LORA), lambda i: (i, L0_CQ)),
            pl.BlockSpec((tm, B_KV_LORA), lambda i: (i, L0_CKV)),
            pl.BlockSpec((1, B_Q_LORA), const),
            pl.BlockSpec((1, B_KV_LORA), const),
            pl.BlockSpec((B_Q_LORA, nq), const),
        ],
        out_specs=[
            pl.BlockSpec((tm, nq), lambda i: (i, 0)),
            pl.BlockSpec((tm, B_KV_LORA), lambda i: (i, 0)),
        ],
        out_shape=[
            jax.ShapeDtypeStruct((m, nq), F32),
            jax.ShapeDtypeStruct((m, B_KV_LORA), F32),
        ],
        compiler_params=_cparams("parallel"),
        name="mla_prep",
    )(proj, proj, q_norm_g, kv_norm_g, w_uq)


def _softmax_parts(scores):
    m = functools.reduce(jnp.maximum, [jnp.max(s, axis=-1, keepdims=True) for s in scores])
    ps = [jnp.exp2(s - m) for s in scores]
    denom = functools.reduce(jnp.add, [jnp.sum(p, axis=-1, keepdims=True) for p in ps])
    return ps, denom


def _diff_attn_body(*refs, dec, tq):
    if dec:
        (q_ref, k_ref, v_ref, kc_ref, vc_ref, gate_ref, cos_ref, sin_ref, lam_ref, sg_ref,
         o_ref, kb, vb, kcb, vcb) = refs
    else:
        q_ref, k_ref, v_ref, gate_ref, lam_ref, sg_ref, o_ref, kb, vb = refs
    qi = pl.program_id(1)

    @pl.when(qi == 0)
    def _():
        for h in range(A_HEADS):
            hs = slice(h * LANES, (h + 1) * LANES)
            kt = k_ref[0, :, hs]
            if dec:
                kt = _rope(kt, cos_ref[...], sin_ref[...])
                kcb[:, hs] = kc_ref[0, :, hs].astype(BF16)
                vcb[:, hs] = vc_ref[0, :, hs].astype(BF16)
            kb[:, hs] = kt.astype(BF16)
            vb[:, hs] = v_ref[0, :, hs].astype(BF16)

    lp = lam_ref[...]
    lam = (jnp.exp(jnp.sum(lp[0:1] * lp[1:2], keepdims=True))
           - jnp.exp(jnp.sum(lp[2:3] * lp[3:4], keepdims=True)) + LAM_INIT_L0)
    if dec:
        row0 = pl.multiple_of(qi * tq, tq)
        cq, sq = cos_ref[pl.ds(row0, tq), :], sin_ref[pl.ds(row0, tq), :]
    first = lax.broadcasted_iota(jnp.int32, (1, LANES), 1) < A_QK_DIM
    for h in range(A_HEADS):
        hs = slice(h * LANES, (h + 1) * LANES)
        qh = q_ref[0, :, hs]
        if dec:
            qh = _rope(qh, cq, sq)
        qh = qh * (A_QK_DIM ** -0.5 * LOG2_E)
        q1 = jnp.where(first, qh, 0.0).astype(BF16)
        q2 = jnp.where(first, 0.0, qh).astype(BF16)
        keys = [kb[:, hs]] + ([kcb[:, hs]] if dec else [])
        vals = [vb[:, hs]] + ([vcb[:, hs]] if dec else [])
        p1, l1 = _softmax_parts([_dot_nt(q1, kp) for kp in keys])
        p2, l2 = _softmax_parts([_dot_nt(q2, kp) for kp in keys])
        o1 = functools.reduce(jnp.add, [_dot(x.astype(BF16), vp) for x, vp in zip(p1, vals)])
        o2 = functools.reduce(jnp.add, [_dot(x.astype(BF16), vp) for x, vp in zip(p2, vals)])
        o = o1 * (1.0 / l1) - o2 * (lam / l2)
        o = _rms(o, sg_ref[...]) * (1.0 - LAM_INIT_L0)
        o_ref[0, :, hs] = (o * _silu(gate_ref[0, :, hs])).astype(BF16)


def _diff_attn(proj, ctx_k, ctx_v, cos, sin, diff_lambda, subln_g, *, tq):
    bsz, t_len, _ = proj.shape
    dec = ctx_k is not None
    w = A_HEADS * LANES
    full = lambda col: pl.BlockSpec((1, t_len, w), lambda b, i: (b, 0, col))
    in_specs = [pl.BlockSpec((1, tq, w), lambda b, i: (b, i, L0_AQ)), full(L0_AK), full(L0_AV)]
    args = [proj, proj, proj]
    scratch = [pltpu.VMEM((t_len, w), BF16), pltpu.VMEM((t_len, w), BF16)]
    if dec:
        p_len = ctx_k.shape[1]
        in_specs += [pl.BlockSpec((1, p_len, w), lambda b, i: (b, 0, 0))] * 2
        args += [ctx_k, ctx_v]
        scratch += [pltpu.VMEM((p_len, w), BF16), pltpu.VMEM((p_len, w), BF16)]
    in_specs.append(pl.BlockSpec((1, tq, w), lambda b, i: (b, i, L0_GATE_A)))
    args.append(proj)
    if dec:
        in_specs += [pl.BlockSpec((t_len, LANES), lambda b, i: (0, 0))] * 2
        args += [cos, sin]
    in_specs += [pl.BlockSpec((4, A_QK_DIM), lambda b, i: (0, 0)),
                 pl.BlockSpec((1, A_V_DIM), lambda b, i: (0, 0))]
    args += [diff_lambda, subln_g]
    return pl.pallas_call(
        functools.partial(_diff_attn_body, dec=dec, tq=tq),
        grid=(bsz, t_len // tq),
        in_specs=in_specs,
        out_specs=pl.BlockSpec((1, tq, w), lambda b, i: (b, i, 0)),
        out_shape=jax.ShapeDtypeStruct((bsz, t_len, w), BF16),
        scratch_shapes=scratch,
        compiler_params=_cparams("parallel", "arbitrary"),
        name="diff_attn",
    )(*args)


def _mla_attn_body(*refs, dec, tq):
    if dec:
        (qn_ref, qp_ref, kv_ref, kpe_ref, kvc_ref, kpec_ref, gate_ref, cos_ref, sin_ref, wukv_ref,
         o_ref, kb, vb, kcb, vcb) = refs
    else:
        qn_ref, qp_ref, kv_ref, kpe_ref, gate_ref, wukv_ref, o_ref, kb, vb = refs
    qi = pl.program_id(1)

    def stage(src_ref, kpe, k_dst, v_dst):
        kv = _dot(src_ref[0].astype(BF16), wukv_ref[...]).astype(BF16)
        for h in range(B_HEADS):
            k_dst[:, 2 * h * LANES:(2 * h + 1) * LANES] = kv[:, 2 * h * LANES:(2 * h + 1) * LANES]
            k_dst[:, (2 * h + 1) * LANES:(2 * h + 2) * LANES] = kpe
            v_dst[:, h * LANES:(h + 1) * LANES] = kv[:, (2 * h + 1) * LANES:(2 * h + 2) * LANES]

    @pl.when(qi == 0)
    def _():
        kp = kpe_ref[0]
        if dec:
            kp = _rope(kp, cos_ref[...], sin_ref[...])
            stage(kvc_ref, kpec_ref[0].astype(BF16), kcb, vcb)
        stage(kv_ref, kp.astype(BF16), kb, vb)

    if dec:
        row0 = pl.multiple_of(qi * tq, tq)
        cq, sq = cos_ref[pl.ds(row0, tq), :], sin_ref[pl.ds(row0, tq), :]
    first = lax.broadcasted_iota(jnp.int32, (1, LANES), 1) < B_ROPE
    for h in range(B_HEADS):
        hs = slice(h * LANES, (h + 1) * LANES)
        if h % 2 == 0:
            pair = qp_ref[0, :, (h // 2) * LANES:(h // 2 + 1) * LANES]
            if dec:
                pair = _rope(pair, cq, sq)
        qp = (jnp.where(first, pair, 0.0) if h % 2 == 0 else jnp.where(first, 0.0, pair)).astype(BF16)
        qn = qn_ref[0, :, hs].astype(BF16)
        q = jnp.concatenate([qn, qp], axis=1)
        ks = slice(2 * h * LANES, (2 * h + 2) * LANES)
        scores = [_dot_nt(q, kb[:, ks])]
        vals = [vb[:, hs]]
        if dec:
            scores.append(_dot_nt(q, kcb[:, ks]))
            vals.append(vcb[:, hs])
        ps, denom = _softmax_parts(scores)
        o = functools.reduce(jnp.add, [_dot(p.astype(BF16), vp) for p, vp in zip(ps, vals)]) * (1.0 / denom)
        o_ref[0, :, hs] = (o * _silu(gate_ref[0, :, hs])).astype(BF16)


def _mla_attn(proj, q_b, kv_b, ctx_kv, ctx_kpe, cos, sin, w_ukv, *, tq):
    bsz, t_len, _ = proj.shape
    dec = ctx_kv is not None
    w = B_HEADS * LANES
    nkv = kv_b.shape[2]
    in_specs = [
        pl.BlockSpec((1, tq, w), lambda b, i: (b, i, 0)),
        pl.BlockSpec((1, tq, B_HEADS * B_ROPE), lambda b, i: (b, i, 2)),
        pl.BlockSpec((1, t_len, nkv), lambda b, i: (b, 0, 0)),
        pl.BlockSpec((1, t_len, LANES), lambda b, i: (b, 0, L0_KPE)),
    ]
    args = [q_b, q_b, kv_b, proj]
    scratch = [pltpu.VMEM((t_len, 2 * w), BF16), pltpu.VMEM((t_len, w), BF16)]
    if dec:
        p_len = ctx_kv.shape[1]
        in_specs += [pl.BlockSpec((1, p_len, nkv), lambda b, i: (b, 0, 0)),
                     pl.BlockSpec((1, p_len, LANES), lambda b, i: (b, 0, 0))]
        args += [ctx_kv, ctx_kpe]
        scratch += [pltpu.VMEM((p_len, 2 * w), BF16), pltpu.VMEM((p_len, w), BF16)]
    in_specs.append(pl.BlockSpec((1, tq, w), lambda b, i: (b, i, L0_GATE_B)))
    args.append(proj)
    if dec:
        in_specs += [pl.BlockSpec((t_len, LANES), lambda b, i: (0, 0))] * 2
        args += [cos, sin]
    in_specs.append(pl.BlockSpec(w_ukv.shape, lambda b, i: (0, 0)))
    args.append(w_ukv)
    return pl.pallas_call(
        functools.partial(_mla_attn_body, dec=dec, tq=tq),
        grid=(bsz, t_len // tq),
        in_specs=in_specs,
        out_specs=pl.BlockSpec((1, tq, w), lambda b, i: (b, i, 0)),
        out_shape=jax.ShapeDtypeStruct((bsz, t_len, w), BF16),
        scratch_shapes=scratch,
        compiler_params=_cparams("parallel", "arbitrary"),
        name="mla_attn",
    )(*args)


def _outproj_body(x_ref, ya_ref, yb_ref, w_ref, gate_ref, o_ref):
    half = ya_ref.shape[1]
    acc = _dot(ya_ref[...], w_ref[0:half, :]) + _dot(yb_ref[...], w_ref[half:, :])
    o_ref[...] = x_ref[...] + gate_ref[0] * acc


def _outproj(x, ya, yb, w, gate, *, rows_per_mod, tm):
    m, d = x.shape
    k = w.shape[0]
    return pl.pallas_call(
        _outproj_body,
        grid=(m // tm,),
        in_specs=[
            pl.BlockSpec((tm, d), lambda i: (i, 0)),
            pl.BlockSpec((tm, k // 2), lambda i: (i, 0)),
            pl.BlockSpec((tm, k // 2), lambda i: (i, 0)),
            pl.BlockSpec((k, d), lambda i: (0, 0), pipeline_mode=pl.Buffered(1)),
            pl.BlockSpec((1, 1, d), lambda i: ((i * tm) // rows_per_mod, 0, 0)),
        ],
        out_specs=pl.BlockSpec((tm, d), lambda i: (i, 0)),
        out_shape=jax.ShapeDtypeStruct((m, d), F32),
        compiler_params=_cparams("parallel"),
        name="outproj",
    )(x, ya, yb, w, gate)


def _relayout_body(main_ref, lora_ref, w_ref, bias_ref, o_ref, r_scr, q_scr):
    q = pl.program_id(2)

    def park(tile_of, prepare=None):
        for b in range(PER):
            if prepare is not None:
                prepare(b)
            for c in range(C_WIDTH // LANES):
                at = tile_of(b, c).T
                p0 = b * C_HEADS + 2 * c
                r_scr[pl.ds(p0 * PITCH, C_HEAD), :] = at[:C_HEAD]
                r_scr[pl.ds((p0 + 1) * PITCH, C_HEAD), :] = at[C_HEAD:]

    @pl.when(q < 3)
    def _():
        park(lambda b, c: main_ref[0, b, :, c * LANES:(c + 1) * LANES])

    expanded = lambda b, c: q_scr[:, c * LANES:(c + 1) * LANES]

    @pl.when((q == 3) | (q == 4))
    def _():
        def decay(b):
            u = -(bias_ref[0] + _dot(jnp.tanh(lora_ref[0, b]).astype(BF16), w_ref[0]))
            softplus = jnp.maximum(u, 0.0) + jnp.log(1.0 + jnp.exp(-jnp.abs(u)))
            q_scr[...] = jnp.exp(-jnp.exp(-softplus - 0.5))
        park(expanded, decay)

    @pl.when(q >= 5)
    def _():
        def iclr(b):
            q_scr[...] = 1.0 / (1.0 + jnp.exp(-(bias_ref[0] + _dot(lora_ref[0, b].astype(BF16), w_ref[0]))))
        park(expanded, iclr)

    for j in range(C_HEAD):
        m = r_scr[pl.ds(j, LANES, stride=PITCH), :]
        o_ref[0, 0, pl.ds(j, TT, stride=PITCH), :] = m.T
    zero = jnp.zeros((TT, LANES), F32)
    for j in range(C_HEAD, PITCH):
        o_ref[0, 0, pl.ds(j, TT, stride=PITCH), :] = zero


def _relayout(proj, lora_w, lora_b, bsz, t_len):
    g = bsz // PER
    proj4 = proj.reshape(g, PER, t_len, L1_COLS)
    lora0 = 4 * C_WIDTH // LANES
    lora_ix = lambda gi, ti, q: (jnp.clip(q - 3, 0, 3), 0, 0)
    out = pl.pallas_call(
        _relayout_body,
        grid=(g, t_len // TT, N_LANE_ARRAYS),
        in_specs=[
            pl.BlockSpec((1, PER, TT, C_WIDTH), lambda gi, ti, q: (gi, 0, ti, jnp.minimum(q, 2))),
            pl.BlockSpec((1, PER, TT, LANES), lambda gi, ti, q: (gi, 0, ti, lora0 + jnp.clip(q - 3, 0, 3))),
            pl.BlockSpec((1, LORA_PAD, C_WIDTH), lora_ix),
            pl.BlockSpec((1, 1, C_WIDTH), lora_ix),
        ],
        out_specs=pl.BlockSpec((1, 1, TT * PITCH, LANES), lambda gi, ti, q: (q, gi, ti, 0)),
        out_shape=jax.ShapeDtypeStruct((N_LANE_ARRAYS, g, t_len * PITCH, LANES), F32),
        scratch_shapes=[pltpu.VMEM((LANES * PITCH, LANES), F32), pltpu.VMEM((TT, C_WIDTH), F32)],
        compiler_params=_cparams("parallel", "parallel", "arbitrary"),
        name="lane_relayout",
    )(proj4, proj4, lora_w, lora_b)
    return out.reshape(N_LANE_ARRAYS, g, t_len, PITCH, LANES)


def _lane_tile(p):
    return jnp.tile(p.reshape(C_HEADS, C_HEAD).T, (1, PER))


def _scan_body(*refs, tc, reverse, has_prev):
    r_ref, k_ref, v_ref, w_ref, a_ref, kkp_ref, kap_ref, rkp_ref, lnw_ref, lnb_ref, s0_ref = refs[:11]
    rest = refs[11:]
    if has_prev:
        prev_ref, rest = rest[0], rest[1:]
    o_ref, sfin_ref, s_scr, kka_scr, kz_scr, nkk_scr, y_scr = rest
    c = pl.program_id(1)

    @pl.when(c == 0)
    def _():
        s_scr[...] = s0_ref[0]

    used = lambda ref: ref[0, 0, :, :C_HEAD, :]
    k = used(k_ref)
    a = used(a_ref)
    kk = k * kkp_ref[...]
    kk = kk * lax.rsqrt(jnp.maximum(jnp.sum(kk * kk, axis=1, keepdims=True), 1e-12))
    kz = k * (1.0 + (a - 1.0) * kap_ref[...])
    kz_scr[...] = kz
    kka_scr[...] = kk * a
    nkk_scr[...] = -kk
    bonus = jnp.sum(used(r_ref) * kz * rkp_ref[...], axis=1, keepdims=True) * used(v_ref)

    first = tc - 1 if reverse else 0
    sa0 = jnp.zeros((C_HEAD, LANES), F32)
    for j in range(C_HEAD):
        sa0 = sa0 + s_scr[j] * nkk_scr[first, j:j + 1, :]

    def step(i, sa):
        t = (tc - 1 - i) if reverse else i
        t_next = jnp.maximum(t - 1, 0) if reverse else jnp.minimum(t + 1, tc - 1)
        v = v_ref[0, 0, t, :C_HEAD, :]
        y = jnp.zeros((C_HEAD, LANES), F32)
        sa_next = jnp.zeros((C_HEAD, LANES), F32)
        for j in range(C_HEAD):
            s = s_scr[j] * w_ref[0, 0, t, j:j + 1, :] + sa * kka_scr[t, j:j + 1, :] + v * kz_scr[t, j:j + 1, :]
            s_scr[j] = s
            y = y + s * r_ref[0, 0, t, j:j + 1, :]
            sa_next = sa_next + s * nkk_scr[t_next, j:j + 1, :]
        y_scr[t] = y
        return sa_next

    lax.fori_loop(0, tc, step, sa0)

    y = y_scr[...]
    dev = y - jnp.mean(y, axis=1, keepdims=True)
    var = jnp.mean(dev * dev, axis=1, keepdims=True)
    out = dev * lax.rsqrt(var + GN_EPS) * lnw_ref[...] + lnb_ref[...] + bonus
    if has_prev:
        out = out + prev_ref[0, :, :C_HEAD, :]
    o_ref[0, :, :C_HEAD, :] = out
    o_ref[0, :, C_HEAD:, :] = jnp.zeros((tc, PITCH - C_HEAD, LANES), F32)

    @pl.when(c == pl.num_programs(1) - 1)
    def _():
        sfin_ref[0] = s_scr[...]


def _scan(z7, z, params, s0, prev, *, reverse, tc):
    _, g, t_len = z7.shape[:3]
    n_c = t_len // tc
    tix = (lambda ci: n_c - 1 - ci) if reverse else (lambda ci: ci)
    qblk = lambda q: pl.BlockSpec((1, 1, tc, PITCH, LANES), lambda gi, ci: (q, gi, tix(ci), 0, 0))
    tile = pl.BlockSpec((C_HEAD, LANES), lambda gi, ci: (0, 0))
    st = pl.BlockSpec((1, C_HEAD, C_HEAD, LANES), lambda gi, ci: (gi, 0, 0, 0))
    oblk = pl.BlockSpec((1, tc, PITCH, LANES), lambda gi, ci: (gi, tix(ci), 0, 0))
    in_specs = [qblk(0), qblk(1), qblk(2), qblk(3 + z), qblk(5 + z)] + [tile] * len(params) + [st]
    args = [z7] * 5 + list(params) + [s0]
    if prev is not None:
        in_specs.append(oblk)
        args.append(prev)
    chunk = pltpu.VMEM((tc, C_HEAD, LANES), F32)
    return pl.pallas_call(
        functools.partial(_scan_body, tc=tc, reverse=reverse, has_prev=prev is not None),
        grid=(g, n_c),
        in_specs=in_specs,
        out_specs=[oblk, st],
        out_shape=[jax.ShapeDtypeStruct((g, t_len, PITCH, LANES), F32), jax.ShapeDtypeStruct(s0.shape, F32)],
        scratch_shapes=[pltpu.VMEM((C_HEAD, C_HEAD, LANES), F32)] + [chunk] * 4,
        compiler_params=_cparams("parallel", "arbitrary"),
        name="wkv_scan",
    )(*args)


def _state_to_lanes(s):
    per = LANES // C_HEADS
    bsz = s.shape[0]
    s = s.reshape(bsz // per, per, C_HEADS, C_HEAD, C_HEAD)
    return s.transpose(0, 4, 3, 1, 2).reshape(bsz // per, C_HEAD, C_HEAD, LANES)


def _state_from_lanes(s):
    per = LANES // C_HEADS
    g = s.shape[0]
    s = s.reshape(g, C_HEAD, C_HEAD, per, C_HEADS)
    return s.transpose(0, 3, 4, 2, 1).reshape(g * per, C_HEADS, C_HEAD, C_HEAD)


def _rwkv_out_body(y_ref, x_ref, gate_ref, w_ref, mg_ref, fg_ref, o_ref, r_scr, y_scr):
    b = pl.program_id(2)

    @pl.when(b == 0)
    def _():
        for i in range(C_HEAD):
            m = y_ref[0, pl.ds(i, TT, stride=PITCH), :]
            r_scr[pl.ds(i, LANES, stride=PITCH), :] = m.T
        for bb in range(PER):
            for c in range(C_WIDTH // LANES):
                p0 = bb * C_HEADS + 2 * c
                pair = jnp.concatenate([r_scr[pl.ds(p0 * PITCH, C_HEAD), :],
                                        r_scr[pl.ds((p0 + 1) * PITCH, C_HEAD), :]], axis=0)
                y_scr[bb, :, c * LANES:(c + 1) * LANES] = pair.T

    yg = (y_scr[b] * _silu(gate_ref[0, 0])).astype(BF16)
    xn = x_ref[0, 0] + mg_ref[0] * _dot(yg, w_ref[...])
    o_ref[0, 0] = _rms(xn, fg_ref[...])


def _rwkv_out(x, y, proj, w, gate, final_g, *, bsz, t_len):
    d = x.shape[1]
    g = bsz // PER
    n_mod = gate.shape[0]
    tok = lambda gi, ti, b: (gi, b, ti, 0)
    const = lambda gi, ti, b: (0, 0)
    out = pl.pallas_call(
        _rwkv_out_body,
        grid=(g, t_len // TT, PER),
        in_specs=[
            pl.BlockSpec((1, TT * PITCH, LANES), lambda gi, ti, b: (gi, ti, 0)),
            pl.BlockSpec((1, 1, TT, d), tok),
            pl.BlockSpec((1, 1, TT, C_WIDTH), lambda gi, ti, b: (gi, b, ti, 3)),
            pl.BlockSpec((C_WIDTH, d), const, pipeline_mode=pl.Buffered(1)),
            pl.BlockSpec((1, 1, d), lambda gi, ti, b: ((gi * PER + b) % n_mod, 0, 0)),
            pl.BlockSpec((1, d), const),
        ],
        out_specs=pl.BlockSpec((1, 1, TT, d), tok),
        out_shape=jax.ShapeDtypeStruct((g, PER, t_len, d), F32),
        scratch_shapes=[pltpu.VMEM((LANES * PITCH, LANES), F32), pltpu.VMEM((PER, TT, C_WIDTH), F32)],
        compiler_params=_cparams("parallel", "parallel", "arbitrary"),
        name="rwkv_out",
    )(y.reshape(g, t_len * PITCH, LANES), x.reshape(g, PER, t_len, d), proj.reshape(g, PER, t_len, L1_COLS),
      w, gate, final_g)
    return out.reshape(bsz * t_len, d)


def _rope_tables(t_len):
    pos = jnp.arange(t_len)
    half = 16
    freqs = ROPE_BASE ** (-jnp.arange(half, dtype=F32) / half)
    ang_r = (pos // GRID_W).astype(F32)[:, None] * freqs
    ang_c = (pos % GRID_W).astype(F32)[:, None] * freqs
    cos = jnp.concatenate([jnp.cos(ang_r)] * 2 + [jnp.cos(ang_c)] * 2, axis=-1)
    sin = jnp.concatenate([-jnp.sin(ang_r), jnp.sin(ang_r), -jnp.sin(ang_c), jnp.sin(ang_c)], axis=-1)
    return jnp.tile(cos, (1, 2)), jnp.tile(sin, (1, 2))


def _l0_w_in_layout(w):
    d = w.shape[0]
    aq, ak, av, cq, ckv, kpe, gate = jnp.split(w.astype(BF16), [1024, 2048, 3072, 3584, 3840, 3904], axis=1)
    pad = jnp.zeros((d, L0_COLS - 6016), BF16)
    return jnp.concatenate([gate, aq, ak, av, cq, ckv, kpe, kpe, pad], axis=1)


def _l1_cols_layout(w):
    main, wd, ad = jnp.split(w, [4 * C_WIDTH, 4 * C_WIDTH + 2 * C_LORA], axis=1)
    pad = jnp.zeros((w.shape[0], LORA_PAD - C_LORA), w.dtype)
    parts = [main]
    for seg in (wd, ad):
        for z in range(2):
            parts += [seg[:, z * C_LORA:(z + 1) * C_LORA], pad]
    return jnp.concatenate(parts, axis=1)


def _layer0(x, mods, rows_per_mod, seq_len, weights, ctx, tables):
    w_in, w_out, diff_lambda, subln_g, q_norm_g, w_uq, kv_norm_g, w_ukv = weights
    shift, scale, gate, norm_g = mods
    m = x.shape[0]
    bsz = m // seq_len
    proj = _inproj(x, shift, scale, norm_g, w_in, None, rows_per_mod=rows_per_mod, seq_len=seq_len, tm=1024, tn=1024)
    q_b, ckv_n = _mla_prep(proj, q_norm_g, kv_norm_g, w_uq, tm=512)
    proj3 = proj.reshape(bsz, seq_len, L0_COLS)
    q_b3 = q_b.reshape(bsz, seq_len, -1)
    ckv3 = ckv_n.reshape(bsz, seq_len, -1)
    if ctx is None:
        ya = _diff_attn(proj3, None, None, None, None, diff_lambda, subln_g, tq=256)
        yb = _mla_attn(proj3, q_b3, ckv3, None, None, None, None, w_ukv, tq=256)
    else:
        k_ctx, v_ctx, ckv_ctx, kpe_ctx = ctx
        p_len = k_ctx.shape[1]
        cos, sin = tables
        ya = _diff_attn(proj3, k_ctx.reshape(bsz, p_len, -1), v_ctx.reshape(bsz, p_len, -1), cos, sin,
                        diff_lambda, subln_g, tq=256)
        yb = _mla_attn(proj3, q_b3, ckv3, ckv_ctx, jnp.concatenate([kpe_ctx, kpe_ctx], axis=-1), cos, sin, w_ukv,
                       tq=256)
    x_new = _outproj(x, ya.reshape(m, -1), yb.reshape(m, -1), w_out, gate, rows_per_mod=rows_per_mod, tm=512)
    return x_new, proj, ckv_n


def _layer1(x, mods, rows_per_mod, seq_len, weights, states, final_g):
    w_in, w_out, mu, lora_w, lora_b, lane_params = weights
    shift, scale, gate, norm_g = mods
    m = x.shape[0]
    bsz = m // seq_len
    proj = _inproj(x, shift, scale, norm_g, w_in, mu, rows_per_mod=rows_per_mod, seq_len=seq_len, tm=1024, tn=512)
    z7 = _relayout(proj, lora_w, lora_b, bsz, seq_len)
    y, finals = None, []
    for z in range(2):
        y, s_fin = _scan(z7, z, lane_params, states[z], y, reverse=(z == 1), tc=32)
        finals.append(_state_from_lanes(s_fin))
    out = _rwkv_out(x, y, proj, w_out, gate, final_g, bsz=bsz, t_len=seq_len)
    return out, finals


def kernel(x_prompt, x_sample, cache_l0_a_k, cache_l0_a_v, cache_l0_mla_ckv, cache_l0_mla_kpe, state_l1_fwd, state_l1_bwd, c, c_ctx, mod_w, mod_b, norm_g, final_norm_g, l0_w_in, l0_w_out, l0_diff_lambda, l0_subln_g, l0_q_norm_g, l0_w_uq, l0_kv_norm_g, l0_w_ukv, l1_w_in, l1_w_out, l1_mu, l1_w0, l1_w2, l1_a0, l1_a2, l1_k_k, l1_k_a, l1_r_k, l1_ln_w, l1_ln_b):
    d = D_MODEL
    bp, tp, _ = x_prompt.shape
    bs, ts, _ = x_sample.shape

    cond = jnp.concatenate([c_ctx[None, :], c, jnp.zeros((MOD_ROWS - 1 - bs, d), F32)], axis=0)
    mods = _adaln(cond, mod_w, mod_b)

    def mod_rows(layer, lo, hi):
        rows = mods[layer, lo:hi]
        shift, scale, gate = (rows[:, i * d:(i + 1) * d].reshape(hi - lo, 1, d) for i in range(3))
        return shift, scale, gate, norm_g[layer].reshape(1, d)

    w_uq = l0_w_uq.reshape(B_Q_LORA, B_HEADS, B_NOPE + B_ROPE)
    w_uq = jnp.concatenate([w_uq[:, :, :B_NOPE].reshape(B_Q_LORA, -1), w_uq[:, :, B_NOPE:].reshape(B_Q_LORA, -1)],
                           axis=1).astype(BF16)
    l0_weights = (_l0_w_in_layout(l0_w_in), l0_w_out.astype(BF16), l0_diff_lambda, l0_subln_g.reshape(1, -1),
                  l0_q_norm_g.reshape(1, -1), w_uq, l0_kv_norm_g.reshape(1, -1), l0_w_ukv.astype(BF16))
    lora_pad = jnp.zeros((2, LORA_PAD - C_LORA, C_WIDTH), F32)
    lora_w = jnp.concatenate([jnp.concatenate([l1_w2, lora_pad], axis=1),
                              jnp.concatenate([l1_a2, lora_pad], axis=1)], axis=0).astype(BF16)
    lora_b = jnp.concatenate([l1_w0, l1_a0], axis=0).reshape(4, 1, C_WIDTH)
    lane_params = [_lane_tile(p) for p in (l1_k_k, l1_k_a, l1_r_k, l1_ln_w, l1_ln_b)]
    l1_weights = (_l1_cols_layout(l1_w_in.astype(BF16)), l1_w_out.astype(BF16), _l1_cols_layout(l1_mu),
                  lora_w, lora_b, lane_params)
    tables = _rope_tables(ts)
    final_g = final_norm_g.reshape(1, d)

    xp = x_prompt.reshape(bp * tp, d)
    xs = x_sample.reshape(bs * ts, d)
    ctx0 = (cache_l0_a_k, cache_l0_a_v, cache_l0_mla_ckv, cache_l0_mla_kpe)

    xp1, proj_p, ckv_p = _layer0(xp, mod_rows(0, 0, 1), bp * tp, tp, l0_weights, None, None)
    xs1, _, _ = _layer0(xs, mod_rows(0, 1, 1 + bs), ts, ts, l0_weights, ctx0, tables)

    zero_state = jnp.zeros((bp // PER, C_HEAD, C_HEAD, LANES), F32)
    y_prompt, finals = _layer1(xp1, mod_rows(1, 0, 1), bp * tp, tp, l1_weights, (zero_state, zero_state), final_g)
    y_sample, _ = _layer1(xs1, mod_rows(1, 1, 1 + bs), ts, ts, l1_weights,
                          (_state_to_lanes(state_l1_fwd), _state_to_lanes(state_l1_bwd)), final_g)

    new_a_k = proj_p[:, 3072:4096].reshape(bp, tp, A_HEADS, 2, A_QK_DIM)
    new_a_v = proj_p[:, 4096:5120].reshape(bp, tp, A_HEADS, A_V_DIM)
    new_ckv = ckv_p.reshape(bp, tp, B_KV_LORA)
    new_kpe = proj_p[:, 5888:5888 + B_ROPE].reshape(bp, tp, B_ROPE)
    return (y_prompt.reshape(bp, tp, d), y_sample.reshape(bs, ts, d), new_a_k, new_a_v, new_ckv, new_kpe,
            finals[0], finals[1])
```
